```python
import math
import jax
import jax.numpy as jnp
from jax import lax
import numpy as np

D_MODEL = 1024
BATCH = 8
SEQ = 2048
DEPTH = 4
DEC_BATCH = 128
DEC_SEQ = 8
PAST_LEN = 16384
PAGE_SIZE = 128

N_EVEN = (DEPTH + 1) // 2
N_ODD = DEPTH // 2

RET_HEADS = 4
RET_DK = D_MODEL // 16
RET_DV = D_MODEL // 8
RET_KEY = RET_HEADS * RET_DK
RET_VALUE = RET_HEADS * RET_DV
RET_CHUNK = 128
ROPE_BASE = 10000.0

LRU_WIDTH = D_MODEL // 2
LRU_BLOCKS = 8
LRU_BLOCK = LRU_WIDTH // LRU_BLOCKS
CONV_W = 4
LRU_C = 8.0

GLA_HEADS = 4
GLA_KEY = D_MODEL // 2
GLA_VALUE = D_MODEL
GLA_DK = GLA_KEY // GLA_HEADS
GLA_DV = GLA_VALUE // GLA_HEADS
GLA_RANK = 16
GLA_TAU = 16.0
GLA_CHUNK = 64

D_FF = 4 * D_MODEL
EPS = 1e-6

EVEN_IN = 2 * RET_KEY + 2 * RET_VALUE + 2 * LRU_WIDTH
EVEN_SPLITS = (RET_KEY, 2 * RET_KEY, 2 * RET_KEY + RET_VALUE, 2 * RET_KEY + 2 * RET_VALUE,
               2 * RET_KEY + 2 * RET_VALUE + LRU_WIDTH)
EVEN_MIX = RET_VALUE + LRU_WIDTH
ODD_IN = 2 * GLA_KEY + 2 * GLA_VALUE + GLA_RANK
ODD_SPLITS = (GLA_KEY, 2 * GLA_KEY, 2 * GLA_KEY + GLA_VALUE, 2 * GLA_KEY + 2 * GLA_VALUE)

kernel_name = 'hybrid_retention_rglru_gla_decode_step'


def rmsnorm(x, g):
    xf = x.astype(jnp.float32)
    y = xf * lax.rsqrt(jnp.mean(xf * xf, axis=-1, keepdims=True) + EPS)
    return (y * g.astype(jnp.float32)).astype(x.dtype)


def head_rms(x):
    return x * lax.rsqrt(jnp.mean(x * x, axis=-1, keepdims=True) + EPS)


def rope(x, pos):
    half = x.shape[-1] // 2
    inv = ROPE_BASE ** (-jnp.arange(half, dtype=jnp.float32) / half)
    ang = pos.astype(jnp.float32)[:, None] * inv[None, :]
    cos = jnp.cos(ang)[None, :, None, :]
    sin = jnp.sin(ang)[None, :, None, :]
    x1, x2 = x[..., :half], x[..., half:]
    return jnp.concatenate([x1 * cos - x2 * sin, x1 * sin + x2 * cos], axis=-1)


def retention_chunked(q, k, v, s0):
    B, T, H, _ = q.shape
    C = math.gcd(T, RET_CHUNK)
    n = T // C
    lg = jnp.log1p(-jnp.exp2(-5.0 - jnp.arange(H, dtype=jnp.float32)))
    idx = jnp.arange(C, dtype=jnp.float32)
    rel = idx[:, None] - idx[None, :]
    decay_mask = jnp.where(rel[None] >= 0, jnp.exp(jnp.maximum(rel, 0.0)[None] * lg[:, None, None]), 0.0)
    q_decay = jnp.exp((idx + 1.0)[:, None] * lg[None, :])
    k_decay = jnp.exp((C - 1.0 - idx)[:, None] * lg[None, :])
    chunk_decay = jnp.exp(C * lg)

    def split(a):
        return a.reshape(B, n, C, H, a.shape[-1]).transpose(1, 0, 2, 3, 4)

    def step(s, blk):
        qc, kc, vc = blk
        scores = jnp.einsum('bihd,bjhd->bhij', qc, kc) * decay_mask[None]
        o = (jnp.einsum('bhij,bjhe->bihe', scores, vc)
             + jnp.einsum('bihd,bhde->bihe', qc * q_decay[None, :, :, None], s))
        s = (s * chunk_decay[None, :, None, None]
             + jnp.einsum('bjhd,bjhe->bhde', kc * k_decay[None, :, :, None], vc))
        return s, o

    s, o = lax.scan(step, s0, (split(q), split(k), split(v)))
    return o.transpose(1, 0, 2, 3, 4).reshape(B, T, H, -1), s


def gla_chunked(q, k, v, log_a, s0):
    B, T, H, _ = q.shape
    C = math.gcd(T, GLA_CHUNK)
    n = T // C
    causal = jnp.tril(jnp.ones((C, C), dtype=bool))

    def split(a):
        return a.reshape(B, n, C, H, a.shape[-1]).transpose(1, 0, 2, 3, 4)

    def step(s, blk):
        qc, kc, vc, gc = blk
        b = jnp.cumsum(gc, axis=1)
        diff = b[:, :, None] - b[:, None, :]
        w = jnp.exp(jnp.where(causal[None, :, :, None, None], diff, -jnp.inf))
        scores = jnp.einsum('bihd,bjhd,bijhd->bhij', qc, kc, w)
        o = (jnp.einsum('bhij,bjhe->bihe', scores, vc)
             + jnp.einsum('bihd,bhde->bihe', qc * jnp.exp(b), s))
        b_last = b[:, -1]
        s = (s * jnp.exp(b_last)[..., None]
             + jnp.einsum('bjhd,bjhe->bhde', kc * jnp.exp(b_last[:, None] - b), vc))
        return s, o

    s, o = lax.scan(step, s0, (split(q), split(k), split(v), split(log_a)))
    return o.transpose(1, 0, 2, 3, 4).reshape(B, T, H, -1), s


def causal_conv(x, buf, w, bias):
    T = x.shape[1]
    full = jnp.concatenate([buf, x], axis=1)
    out = bias[None, None, :] + full[:, 0:T] * w[0]
    for i in range(1, CONV_W):
        out = out + full[:, i:i + T] * w[i]
    return out, full[:, -(CONV_W - 1):]


def rg_lru(x, h0, reset, w_a, b_a, w_i, b_i, lam):
    B, T, W = x.shape
    xb = x.reshape(B, T, LRU_BLOCKS, LRU_BLOCK)
    r = jax.nn.sigmoid(jnp.einsum('btnc,ncd->btnd', xb, w_a.astype(jnp.float32)).reshape(B, T, W) + b_a.astype(jnp.float32))
    i = jax.nn.sigmoid(jnp.einsum('btnc,ncd->btnd', xb, w_i.astype(jnp.float32)).reshape(B, T, W) + b_i.astype(jnp.float32))
    log_a = -LRU_C * r * jax.nn.softplus(-lam.astype(jnp.float32))
    a = jnp.exp(log_a)
    mult = jnp.where(reset[None, :, None], 1.0, jnp.sqrt(-jnp.expm1(2.0 * log_a)))
    b = mult * (i * x)
    b = b.at[:, 0].add(a[:, 0] * h0)

    def combine(lhs, rhs):
        a1, b1 = lhs
        a2, b2 = rhs
        return a1 * a2, a2 * b1 + b2

    _, h = lax.associative_scan(combine, (a, b), axis=1)
    return h, h[:, -1]


def even_mixer(h, pos, ret_s, lru_h, conv_buf, w_in, w_out, conv_w, conv_b, w_a, b_a, w_i, b_i, lam):
    B, T, _ = h.shape
    proj = (h @ w_in).astype(jnp.float32)
    q, k, v, g, xr, xg = jnp.split(proj, EVEN_SPLITS, axis=-1)
    q = rope(q.reshape(B, T, RET_HEADS, RET_DK), pos)
    k = rope(k.reshape(B, T, RET_HEADS, RET_DK), pos) * (RET_DK ** -0.5)
    v = v.reshape(B, T, RET_HEADS, RET_DV)
    ret_o, ret_new = retention_chunked(q, k, v, ret_s.astype(jnp.float32))
    ret_out = jax.nn.silu(g) * head_rms(ret_o).reshape(B, T, RET_VALUE)
    xc, conv_new = causal_conv(xr, conv_buf.astype(jnp.float32), conv_w.astype(jnp.float32), conv_b.astype(jnp.float32))
    hr, lru_new = rg_lru(xc, lru_h.astype(jnp.float32), pos == 0, w_a, b_a, w_i, b_i, lam)
    lru_out = hr * jax.nn.gelu(xg)
    mix = jnp.concatenate([ret_out, lru_out], axis=-1).astype(h.dtype) @ w_out
    return mix, ret_new.astype(ret_s.dtype), lru_new.astype(lru_h.dtype), conv_new.astype(conv_buf.dtype)


def odd_mixer(h, gla_s, w_in, w_gate2, b_gate, norm_w, w_out):
    B, T, _ = h.shape
    proj = (h @ w_in).astype(jnp.float32)
    q, k, v, r, glr = jnp.split(proj, ODD_SPLITS, axis=-1)
    q = q.reshape(B, T, GLA_HEADS, GLA_DK) * (GLA_DK ** -0.5)
    k = k.reshape(B, T, GLA_HEADS, GLA_DK)
    v = v.reshape(B, T, GLA_HEADS, GLA_DV)
    gate_logit = glr @ w_gate2.astype(jnp.float32) + b_gate.astype(jnp.float32)
    log_a = (jax.nn.log_sigmoid(gate_logit) / GLA_TAU).reshape(B, T, GLA_HEADS, GLA_DK)
    o, s_new = gla_chunked(q, k, v, log_a, gla_s.astype(jnp.float32))
    o = head_rms(o) * norm_w.astype(jnp.float32)
    mix = (jax.nn.silu(r) * o.reshape(B, T, GLA_VALUE)).astype(h.dtype) @ w_out
    return mix, s_new.astype(gla_s.dtype)


def trunk(x, pos, ret_s, lru_h, conv_buf, gla_s, norm_mix, norm_mlp, norm_final,
          w_in_even, w_out_even, conv_w, conv_b, lru_w_a, lru_b_a, lru_w_i, lru_b_i, lru_lambda,
          w_in_odd, gla_w_gate2, gla_b_gate, gla_norm, w_out_odd, w_up, w_down):
    rets, lrus, convs, glas = [], [], [], []
    for layer in range(DEPTH):
        h = rmsnorm(x, norm_mix[layer])
        if layer % 2 == 0:
            e = layer // 2
            m, rs, lh, cb = even_mixer(h, pos, ret_s[e], lru_h[e], conv_buf[e], w_in_even[e], w_out_even[e],
                                       conv_w[e], conv_b[e], lru_w_a[e], lru_b_a[e], lru_w_i[e], lru_b_i[e],
                                       lru_lambda[e])
            rets.append(rs)
            lrus.append(lh)
            convs.append(cb)
        else:
            o = layer // 2
            m, gs = odd_mixer(h, gla_s[o], w_in_odd[o], gla_w_gate2[o], gla_b_gate[o], gla_norm[o], w_out_odd[o])
            glas.append(gs)
        x = x + m
        h = rmsnorm(x, norm_mlp[layer])
        x = x + jnp.square(jax.nn.relu(h @ w_up[layer])) @ w_down[layer]
    y = rmsnorm(x, norm_final)
    return y, jnp.stack(rets), jnp.stack(lrus), jnp.stack(convs), jnp.stack(glas)


def setup_inputs(seed: int = 0) -> dict:
    key = jax.random.key(seed)
    ks = jax.random.split(key, 28)

    def nrm(k, shape, scale):
        return jax.random.normal(k, shape, jnp.float32) * scale

    radius = jax.random.uniform(ks[20], (N_EVEN, LRU_WIDTH), jnp.float32, 0.9, 0.999)
    return {
        'x_prompt': nrm(ks[0], (BATCH, SEQ, D_MODEL), 1.0),
        'x_sample': nrm(ks[1], (DEC_BATCH, DEC_SEQ, D_MODEL), 1.0),
        'state_ret': nrm(ks[2], (N_EVEN, DEC_BATCH, RET_HEADS, RET_DK, RET_DV), 0.1),
        'state_lru': nrm(ks[3], (N_EVEN, DEC_BATCH, LRU_WIDTH), 0.5),
        'state_conv': nrm(ks[4], (N_EVEN, DEC_BATCH, CONV_W - 1, LRU_WIDTH), 1.0),
        'state_gla': nrm(ks[5], (N_ODD, DEC_BATCH, GLA_HEADS, GLA_DK, GLA_DV), 0.1),
        'norm_mix': 1.0 + nrm(ks[6], (DEPTH, D_MODEL), 0.02),
        'norm_mlp': 1.0 + nrm(ks[7], (DEPTH, D_MODEL), 0.02),
        'norm_final': 1.0 + nrm(ks[8], (D_MODEL,), 0.02),
        'w_in_even': nrm(ks[9], (N_EVEN, D_MODEL, EVEN_IN), D_MODEL ** -0.5),
        'w_out_even': nrm(ks[10], (N_EVEN, EVEN_MIX, D_MODEL), EVEN_MIX ** -0.5),
        'conv_w': nrm(ks[11], (N_EVEN, CONV_W, LRU_WIDTH), CONV_W ** -0.5),
        'conv_b': nrm(ks[12], (N_EVEN, LRU_WIDTH), 0.02),
        'lru_w_a': nrm(ks[13], (N_EVEN, LRU_BLOCKS, LRU_BLOCK, LRU_BLOCK), LRU_BLOCK ** -0.5),
        'lru_b_a': nrm(ks[14], (N_EVEN, LRU_WIDTH), 0.02),
        'lru_w_i': nrm(ks[15], (N_EVEN, LRU_BLOCKS, LRU_BLOCK, LRU_BLOCK), LRU_BLOCK ** -0.5),
        'lru_b_i': nrm(ks[16], (N_EVEN, LRU_WIDTH), 0.02),
        'lru_lambda': jnp.log(radius) - jnp.log1p(-radius),
        'w_in_odd': nrm(ks[17], (N_ODD, D_MODEL, ODD_IN), D_MODEL ** -0.5),
        'gla_w_gate2': nrm(ks[18], (N_ODD, GLA_RANK, GLA_KEY), GLA_RANK ** -0.5),
        'gla_b_gate': nrm(ks[19], (N_ODD, GLA_KEY), 0.02),
        'gla_norm': 1.0 + nrm(ks[21], (N_ODD, GLA_DV), 0.02),
        'w_out_odd': nrm(ks[22], (N_ODD, GLA_VALUE, D_MODEL), GLA_VALUE ** -0.5),
        'w_up': nrm(ks[23], (DEPTH, D_MODEL, D_FF), D_MODEL ** -0.5),
        'w_down': nrm(ks[24], (DEPTH, D_FF, D_MODEL), 0.5 * D_FF ** -0.5),
    }


def reference(x_prompt, x_sample, state_ret, state_lru, state_conv, state_gla,
              norm_mix, norm_mlp, norm_final, w_in_even, w_out_even, conv_w, conv_b,
              lru_w_a, lru_b_a, lru_w_i, lru_b_i, lru_lambda,
              w_in_odd, gla_w_gate2, gla_b_gate, gla_norm, w_out_odd, w_up, w_down):
    n_p = x_prompt.shape[0]
    dt = x_prompt.dtype
    pos_prompt = jnp.arange(x_prompt.shape[1], dtype=jnp.int32)
    pos_sample = PAST_LEN + jnp.arange(x_sample.shape[1], dtype=jnp.int32)
    ret0 = jnp.zeros((N_EVEN, n_p, RET_HEADS, RET_DK, RET_DV), dt)
    lru0 = jnp.zeros((N_EVEN, n_p, LRU_WIDTH), dt)
    conv0 = jnp.zeros((N_EVEN, n_p, CONV_W - 1, LRU_WIDTH), dt)
    gla0 = jnp.zeros((N_ODD, n_p, GLA_HEADS, GLA_DK, GLA_DV), dt)

    y_prompt, ret_p, lru_p, conv_p, gla_p = trunk(
        x_prompt, pos_prompt, ret0, lru0, conv0, gla0, norm_mix, norm_mlp, norm_final,
        w_in_even, w_out_even, conv_w, conv_b, lru_w_a, lru_b_a, lru_w_i, lru_b_i, lru_lambda,
        w_in_odd, gla_w_gate2, gla_b_gate, gla_norm, w_out_odd, w_up, w_down)
    y_sample, ret_s, lru_s, conv_s, gla_s = trunk(
        x_sample, pos_sample, state_ret, state_lru, state_conv, state_gla, norm_mix, norm_mlp, norm_final,
        w_in_even, w_out_even, conv_w, conv_b, lru_w_a, lru_b_a, lru_w_i, lru_b_i, lru_lambda,
        w_in_odd, gla_w_gate2, gla_b_gate, gla_norm, w_out_odd, w_up, w_down)
    return (y_prompt, y_sample, ret_p, ret_s, lru_p, lru_s, conv_p, conv_s, gla_p, gla_s)
```

```python
import functools
import math

import jax
import jax.numpy as jnp
from jax import lax
from jax.experimental import pallas as pl
from jax.experimental.pallas import tpu as pltpu

F32 = jnp.float32
BF16 = jnp.bfloat16

D_MODEL = 1024
DEPTH = 4
PAST_LEN = 16384
EPS = 1e-6
ROPE_BASE = 10000.0

RET_HEADS = 4
RET_DK = 64
RET_DV = 128
RET_KEY = RET_HEADS * RET_DK
RET_VALUE = RET_HEADS * RET_DV
RET_CHUNK = 128

LRU_WIDTH = 512
LRU_BLOCKS = 8
LRU_BLOCK = LRU_WIDTH // LRU_BLOCKS
CONV_W = 4
LRU_C = 8.0

GLA_HEADS = 4
GLA_KEY = 512
GLA_VALUE = 1024
GLA_DK = GLA_KEY // GLA_HEADS
GLA_DV = GLA_VALUE // GLA_HEADS
GLA_RANK = 16
GLA_TAU = 16.0
GLA_CHUNK = 64

D_FF = 4 * D_MODEL
FF_CHUNK = 1024

EVEN_IN = 2 * RET_KEY + 2 * RET_VALUE + 2 * LRU_WIDTH
ODD_MAIN = 2 * GLA_KEY + 2 * GLA_VALUE
ODD_OUT = ODD_MAIN + GLA_KEY

LANES = 128
SUBLANES = 8
ROW_TILE = 512
VMEM_LIMIT = 52 * 1024 * 1024


def _dot(a, b):
    return jnp.dot(a.astype(BF16), b.astype(BF16), preferred_element_type=F32)


def _dot_nt(a, b):
    return lax.dot_general(a.astype(BF16), b.astype(BF16), (((1,), (1,)), ((), ())),
                           preferred_element_type=F32)


def _dot_tn(a, b):
    return lax.dot_general(a.astype(BF16), b.astype(BF16), (((0,), (0,)), ((), ())),
                           preferred_element_type=F32)


def _rmsnorm(x, g):
    return x * lax.rsqrt(jnp.mean(x * x, axis=-1, keepdims=True) + EPS) * g


def _head_rms(x):
    return x * lax.rsqrt(jnp.mean(x * x, axis=-1, keepdims=True) + EPS)


def _shift_rows(x, d, fill):
    row = lax.broadcasted_iota(jnp.int32, x.shape, 0)
    return jnp.where(row >= d, pltpu.roll(x, d, axis=0), fill)


def _const_spec(shape):
    n = len(shape)
    return pl.BlockSpec(shape, lambda *_: (0,) * n, pipeline_mode=pl.Buffered(1))


def _norm_proj_even_kernel(x_ref, g_ref, w_ref, o_ref):
    h = _rmsnorm(x_ref[...], g_ref[...])
    o_ref[...] = _dot(h, w_ref[...])


def _norm_proj_odd_kernel(x_ref, g_ref, w_ref, wg1_ref, wg2_ref, bg_ref, o_ref):
    h = _rmsnorm(x_ref[...], g_ref[...]).astype(BF16)
    o_ref[:, :ODD_MAIN] = _dot(h, w_ref[...])
    glr = _dot(h, wg1_ref[...])
    z = _dot(glr, wg2_ref[...]) + bg_ref[...]
    log_sig = jnp.minimum(z, 0.0) - jnp.log1p(jnp.exp(-jnp.abs(z)))
    o_ref[:, ODD_MAIN:] = log_sig / GLA_TAU


def _norm_proj_even(x, g, w):
    n = x.shape[0]
    tm = min(ROW_TILE, n)
    return pl.pallas_call(
        _norm_proj_even_kernel,
        grid=(n // tm,),
        in_specs=[pl.BlockSpec((tm, D_MODEL), lambda i: (i, 0)),
                  _const_spec((1, D_MODEL)),
                  _const_spec((D_MODEL, EVEN_IN))],
        out_specs=pl.BlockSpec((tm, EVEN_IN), lambda i: (i, 0)),
        out_shape=jax.ShapeDtypeStruct((n, EVEN_IN), F32),
        compiler_params=pltpu.CompilerParams(dimension_semantics=("parallel",),
                                             vmem_limit_bytes=VMEM_LIMIT),
        name="norm_proj_even",
    )(x, g, w)


def _norm_proj_odd(x, g, w, wg1, wg2, bg):
    n = x.shape[0]
    tm = min(ROW_TILE, n)
    return pl.pallas_call(
        _norm_proj_odd_kernel,
        grid=(n // tm,),
        in_specs=[pl.BlockSpec((tm, D_MODEL), lambda i: (i, 0)),
                  _const_spec((1, D_MODEL)),
                  _const_spec((D_MODEL, ODD_MAIN)),
                  _const_spec((D_MODEL, LANES)),
                  _const_spec((LANES, GLA_KEY)),
                  _const_spec((1, GLA_KEY))],
        out_specs=pl.BlockSpec((tm, ODD_OUT), lambda i: (i, 0)),
        out_shape=jax.ShapeDtypeStruct((n, ODD_OUT), F32),
        compiler_params=pltpu.CompilerParams(dimension_semantics=("parallel",),
                                             vmem_limit_bytes=VMEM_LIMIT),
        name="norm_proj_odd",
    )(x, g, w, wg1, wg2, bg)


def _out_mlp_kernel(final, x_ref, m_ref, wo_ref, g_ref, wu_ref, wd_ref, gf_ref, o_ref):
    x1 = x_ref[...] + _dot(m_ref[...], wo_ref[...])
    h = _rmsnorm(x1, g_ref[...]).astype(BF16)
    acc = None
    for c in range(D_FF // FF_CHUNK):
        cols = slice(c * FF_CHUNK, (c + 1) * FF_CHUNK)
        u = jnp.square(jnp.maximum(_dot(h, wu_ref[:, cols]), 0.0))
        part = _dot(u, wd_ref[cols, :])
        acc = part if acc is None else acc + part
    x2 = x1 + acc
    if final:
        x2 = _rmsnorm(x2, gf_ref[...])
    o_ref[...] = x2


def _out_mlp(x, mix, wo, g, wu, wd, gf, final):
    n = x.shape[0]
    tm = min(ROW_TILE, n)
    return pl.pallas_call(
        functools.partial(_out_mlp_kernel, final),
        grid=(n // tm,),
        in_specs=[pl.BlockSpec((tm, D_MODEL), lambda i: (i, 0)),
                  pl.BlockSpec((tm, D_MODEL), lambda i: (i, 0)),
                  _const_spec((D_MODEL, D_MODEL)),
                  _const_spec((1, D_MODEL)),
                  _const_spec((D_MODEL, D_FF)),
                  _const_spec((D_FF, D_MODEL)),
                  _const_spec((1, D_MODEL))],
        out_specs=pl.BlockSpec((tm, D_MODEL), lambda i: (i, 0)),
        out_shape=jax.ShapeDtypeStruct((n, D_MODEL), F32),
        compiler_params=pltpu.CompilerParams(dimension_semantics=("parallel",),
                                             vmem_limit_bytes=VMEM_LIMIT),
        name="out_mlp_final" if final else "out_mlp",
    )(x, mix, wo, g, wu, wd, gf)


def _even_mixer_kernel(pos0, chunk,
                       proj_ref, cos_ref, sin_ref, s0_ref, h0_ref, cb_ref,
                       cw_ref, cbias_ref, wa_ref, ba_ref, wi_ref, bi_ref, lam_ref,
                       mix_ref, snew_ref, hnew_ref, cnew_ref,
                       s_scr, h_scr, ext_scr):
    C = chunk
    c = pl.program_id(1)
    last = pl.num_programs(1) - 1

    @pl.when(c == 0)
    def _():
        s_scr[...] = s0_ref[0]
        h_scr[...] = h0_ref[0]
        ext_scr[0:SUBLANES, :] = jnp.zeros((SUBLANES, LRU_WIDTH), F32)
        ext_scr[SUBLANES - (CONV_W - 1):SUBLANES, :] = cb_ref[0]

    cos = cos_ref[...]
    sin = sin_ref[...]
    lane = lax.broadcasted_iota(jnp.int32, (C, RET_KEY), 1)
    first_half = (lane % RET_DK) < (RET_DK // 2)

    def rope(x):
        partner = jnp.where(first_half,
                            pltpu.roll(x, RET_KEY - RET_DK // 2, axis=1),
                            pltpu.roll(x, RET_DK // 2, axis=1))
        return x * cos + partner * sin

    q = rope(proj_ref[0, :, 0:RET_KEY])
    k = rope(proj_ref[0, :, RET_KEY:2 * RET_KEY]) * (RET_DK ** -0.5)

    ri = lax.broadcasted_iota(jnp.int32, (C, C), 0)
    ci = lax.broadcasted_iota(jnp.int32, (C, C), 1)
    rel = (ri - ci).astype(F32)
    row = lax.broadcasted_iota(jnp.int32, (C, 1), 0).astype(F32)

    v0 = 2 * RET_KEY
    g0 = v0 + RET_VALUE
    for h in range(RET_HEADS):
        lg = math.log1p(-(2.0 ** (-5.0 - h)))
        decay_mask = jnp.where(rel >= 0, jnp.exp(jnp.maximum(rel, 0.0) * lg), 0.0)
        q_decay = jnp.exp((row + 1.0) * lg)
        k_decay = jnp.exp((C - 1.0 - row) * lg)
        qh = q[:, h * RET_DK:(h + 1) * RET_DK]
        kh = k[:, h * RET_DK:(h + 1) * RET_DK]
        vh = proj_ref[0, :, v0 + h * RET_DV:v0 + (h + 1) * RET_DV]
        gh = proj_ref[0, :, g0 + h * RET_DV:g0 + (h + 1) * RET_DV]
        s = s_scr[h]
        scores = _dot_nt(qh, kh) * decay_mask
        o = _dot(scores, vh) + _dot(qh * q_decay, s)
        s_scr[h] = s * math.exp(C * lg) + _dot_tn(kh * k_decay, vh)
        mix_ref[0, :, h * RET_DV:(h + 1) * RET_DV] = jax.nn.silu(gh) * _head_rms(o)

    x0 = g0 + RET_VALUE
    xg0 = x0 + LRU_WIDTH
    ext_scr[SUBLANES:SUBLANES + C, :] = proj_ref[0, :, x0:x0 + LRU_WIDTH]
    xc = cbias_ref[...]
    for i in range(CONV_W):
        lo = SUBLANES - (CONV_W - 1) + i
        xc = xc + ext_scr[lo:lo + C, :] * cw_ref[i:i + 1, :]
    tail = ext_scr[C:C + SUBLANES, :]

    @pl.when(c == last)
    def _():
        cnew_ref[0] = ext_scr[C + SUBLANES - (CONV_W - 1):C + SUBLANES, :]

    ext_scr[0:SUBLANES, :] = tail

    r = jax.nn.sigmoid(_dot(xc, wa_ref[...]) + ba_ref[...])
    gate_i = jax.nn.sigmoid(_dot(xc, wi_ref[...]) + bi_ref[...])
    lam = lam_ref[...]
    softplus_neg_lam = jnp.maximum(-lam, 0.0) + jnp.log1p(jnp.exp(-jnp.abs(lam)))
    log_a = -LRU_C * r * softplus_neg_lam
    a = jnp.exp(log_a)
    mult = jnp.sqrt(-jnp.tanh(log_a) * (a * a + 1.0))
    trow = lax.broadcasted_iota(jnp.int32, (C, LRU_WIDTH), 0)
    pos = trow + (pos0 + c * C)
    mult = jnp.where(pos == 0, 1.0, mult)
    b = mult * (gate_i * xc)
    b = b + jnp.where(trow == 0, a * h_scr[...], 0.0)
    d = 1
    while d < C:
        b = a * _shift_rows(b, d, 0.0) + b
        a = a * _shift_rows(a, d, 1.0)
        d *= 2
    h_scr[...] = b[C - 1:C, :]
    xg = proj_ref[0, :, xg0:xg0 + LRU_WIDTH]
    mix_ref[0, :, RET_VALUE:] = b * jax.nn.gelu(xg)

    @pl.when(c == last)
    def _():
        snew_ref[0] = s_scr[...]
        hnew_ref[0] = h_scr[...]


def _even_mixer(proj, pos0, cos, sin, ret_s, lru_h, conv_buf,
                conv_w, conv_b, wa, ba, wi, bi, lam):
    B, T, _ = proj.shape
    C = math.gcd(T, RET_CHUNK)
    n = T // C
    out_shapes = (jax.ShapeDtypeStruct((B, T, D_MODEL), F32),
                  jax.ShapeDtypeStruct((B, RET_HEADS, RET_DK, RET_DV), F32),
                  jax.ShapeDtypeStruct((B, 1, LRU_WIDTH), F32),
                  jax.ShapeDtypeStruct((B, CONV_W - 1, LRU_WIDTH), F32))
    return pl.pallas_call(
        functools.partial(_even_mixer_kernel, pos0, C),
        grid=(B, n),
        in_specs=[pl.BlockSpec((1, C, EVEN_IN), lambda b, c: (b, c, 0)),
                  pl.BlockSpec((C, RET_KEY), lambda b, c: (c, 0)),
                  pl.BlockSpec((C, RET_KEY), lambda b, c: (c, 0)),
                  pl.BlockSpec((1, RET_HEADS, RET_DK, RET_DV), lambda b, c: (b, 0, 0, 0)),
                  pl.BlockSpec((1, 1, LRU_WIDTH), lambda b, c: (b, 0, 0)),
                  pl.BlockSpec((1, CONV_W - 1, LRU_WIDTH), lambda b, c: (b, 0, 0)),
                  _const_spec((CONV_W, LRU_WIDTH)),
                  _const_spec((1, LRU_WIDTH)),
                  _const_spec((LRU_WIDTH, LRU_WIDTH)),
                  _const_spec((1, LRU_WIDTH)),
                  _const_spec((LRU_WIDTH, LRU_WIDTH)),
                  _const_spec((1, LRU_WIDTH)),
                  _const_spec((1, LRU_WIDTH))],
        out_specs=(pl.BlockSpec((1, C, D_MODEL), lambda b, c: (b, c, 0)),
                   pl.BlockSpec((1, RET_HEADS, RET_DK, RET_DV), lambda b, c: (b, 0, 0, 0)),
                   pl.BlockSpec((1, 1, LRU_WIDTH), lambda b, c: (b, 0, 0)),
                   pl.BlockSpec((1, CONV_W - 1, LRU_WIDTH), lambda b, c: (b, 0, 0))),
        out_shape=out_shapes,
        scratch_shapes=[pltpu.VMEM((RET_HEADS, RET_DK, RET_DV), F32),
                        pltpu.VMEM((1, LRU_WIDTH), F32),
                        pltpu.VMEM((C + SUBLANES, LRU_WIDTH), F32)],
        compiler_params=pltpu.CompilerParams(dimension_semantics=("parallel", "arbitrary"),
                                             vmem_limit_bytes=VMEM_LIMIT),
        name="even_mixer",
    )(proj, cos, sin, ret_s, lru_h, conv_buf, conv_w, conv_b, wa, ba, wi, bi, lam)


def _gla_mixer_kernel(chunk, proj_ref, s0_ref, nw_ref, mix_ref, snew_ref, s_scr):
    C = chunk
    c = pl.program_id(1)
    last = pl.num_programs(1) - 1

    @pl.when(c == 0)
    def _():
        s_scr[...] = s0_ref[0]

    k0 = GLA_KEY
    v0 = 2 * GLA_KEY
    r0 = v0 + GLA_VALUE
    la0 = r0 + GLA_VALUE

    b = proj_ref[0, :, la0:la0 + GLA_KEY]
    d = 1
    while d < C:
        b = b + _shift_rows(b, d, 0.0)
        d *= 2
    b_mid = b[C // 2:C // 2 + 1, :]
    b_last = b[C - 1:C, :]
    q = proj_ref[0, :, 0:GLA_KEY] * (GLA_DK ** -0.5)
    k = proj_ref[0, :, k0:k0 + GLA_KEY]
    q_in = q * jnp.exp(b - b_mid)
    k_in = k * jnp.exp(b_mid - b)
    q_st = q * jnp.exp(b)
    k_st = k * jnp.exp(b_last - b)
    s_decay = jnp.exp(b_last)

    ri = lax.broadcasted_iota(jnp.int32, (C, C), 0)
    ci = lax.broadcasted_iota(jnp.int32, (C, C), 1)
    causal = ri >= ci
    di = lax.broadcasted_iota(jnp.int32, (GLA_DK, GLA_DK), 0)
    dj = lax.broadcasted_iota(jnp.int32, (GLA_DK, GLA_DK), 1)
    eye = di == dj
    nw = nw_ref[...]

    for h in range(GLA_HEADS):
        ks = slice(h * GLA_DK, (h + 1) * GLA_DK)
        vh = proj_ref[0, :, v0 + h * GLA_DV:v0 + (h + 1) * GLA_DV]
        rh = proj_ref[0, :, r0 + h * GLA_DV:r0 + (h + 1) * GLA_DV]
        s = s_scr[h]
        scores = jnp.where(causal, _dot_nt(q_in[:, ks], k_in[:, ks]), 0.0)
        o = _dot(scores, vh) + _dot(q_st[:, ks], s)
        decay_col = jnp.sum(jnp.where(eye, s_decay[:, ks], 0.0), axis=1, keepdims=True)
        s_scr[h] = s * decay_col + _dot_tn(k_st[:, ks], vh)
        mix_ref[0, :, h * GLA_DV:(h + 1) * GLA_DV] = jax.nn.silu(rh) * (_head_rms(o) * nw)

    @pl.when(c == last)
    def _():
        snew_ref[0] = s_scr[...]


def _gla_mixer(proj, gla_s, norm_w):
    B, T, _ = proj.shape
    C = math.gcd(T, GLA_CHUNK)
    n = T // C
    state = (1, GLA_HEADS, GLA_DK, GLA_DV)
    return pl.pallas_call(
        functools.partial(_gla_mixer_kernel, C),
        grid=(B, n),
        in_specs=[pl.BlockSpec((1, C, ODD_OUT), lambda b, c: (b, c, 0)),
                  pl.BlockSpec(state, lambda b, c: (b, 0, 0, 0)),
                  _const_spec((1, GLA_DV))],
        out_specs=(pl.BlockSpec((1, C, D_MODEL), lambda b, c: (b, c, 0)),
                   pl.BlockSpec(state, lambda b, c: (b, 0, 0, 0))),
        out_shape=(jax.ShapeDtypeStruct((B, T, D_MODEL), F32),
                   jax.ShapeDtypeStruct((B,) + state[1:], F32)),
        scratch_shapes=[pltpu.VMEM(state[1:], F32)],
        compiler_params=pltpu.CompilerParams(dimension_semantics=("parallel", "arbitrary"),
                                             vmem_limit_bytes=VMEM_LIMIT),
        name="gla_mixer",
    )(proj, gla_s, norm_w)


def _rope_tables(pos0, T):
    half = RET_DK // 2
    inv = ROPE_BASE ** (-jnp.arange(half, dtype=F32) / half)
    ang = (pos0 + jnp.arange(T, dtype=jnp.int32)).astype(F32)[:, None] * inv[None, :]
    cos = jnp.cos(ang)
    sin = jnp.sin(ang)
    cos_full = jnp.tile(jnp.concatenate([cos, cos], axis=-1), (1, RET_HEADS))
    sin_signed = jnp.tile(jnp.concatenate([-sin, sin], axis=-1), (1, RET_HEADS))
    return cos_full, sin_signed


def _block_diag(w):
    nb, c, d = w.shape
    eye = jnp.eye(nb, dtype=w.dtype)
    return (eye[:, None, :, None] * w[:, :, None, :]).reshape(nb * c, nb * d)


def _prepare_weights(norm_mix, norm_mlp, norm_final, w_in_even, w_out_even, conv_w, conv_b,
                     lru_w_a, lru_b_a, lru_w_i, lru_b_i, lru_lambda,
                     w_in_odd, gla_w_gate2, gla_b_gate, gla_norm, w_out_odd, w_up, w_down):
    n_even = w_in_even.shape[0]
    n_odd = w_in_odd.shape[0]
    pad1 = LANES - GLA_RANK
    return dict(
        norm_mix=norm_mix.reshape(DEPTH, 1, D_MODEL),
        norm_mlp=norm_mlp.reshape(DEPTH, 1, D_MODEL),
        norm_final=norm_final.reshape(1, D_MODEL),
        w_in_even=w_in_even.astype(BF16),
        w_out_even=w_out_even.astype(BF16),
        conv_w=conv_w,
        conv_b=conv_b.reshape(n_even, 1, LRU_WIDTH),
        wa=jnp.stack([_block_diag(lru_w_a[e]) for e in range(n_even)]).astype(BF16),
        ba=lru_b_a.reshape(n_even, 1, LRU_WIDTH),
        wi=jnp.stack([_block_diag(lru_w_i[e]) for e in range(n_even)]).astype(BF16),
        bi=lru_b_i.reshape(n_even, 1, LRU_WIDTH),
        lam=lru_lambda.reshape(n_even, 1, LRU_WIDTH),
        w_in_odd=w_in_odd[:, :, :ODD_MAIN].astype(BF16),
        wg1=jnp.pad(w_in_odd[:, :, ODD_MAIN:], ((0, 0), (0, 0), (0, pad1))).astype(BF16),
        wg2=jnp.pad(gla_w_gate2, ((0, 0), (0, pad1), (0, 0))).astype(BF16),
        bg=gla_b_gate.reshape(n_odd, 1, GLA_KEY),
        gla_norm=gla_norm.reshape(n_odd, 1, GLA_DV),
        w_out_odd=w_out_odd.astype(BF16),
        w_up=w_up.astype(BF16),
        w_down=w_down.astype(BF16),
    )


def _trunk(x, pos0, ret_s, lru_h, conv_buf, gla_s, w):
    B, T, _ = x.shape
    xf = x.reshape(B * T, D_MODEL)
    cos, sin = _rope_tables(pos0, T)
    rets, lrus, convs, glas = [], [], [], []
    for layer in range(DEPTH):
        i = layer // 2
        if layer % 2 == 0:
            proj = _norm_proj_even(xf, w["norm_mix"][layer], w["w_in_even"][i])
            mix, rs, lh, cb = _even_mixer(
                proj.reshape(B, T, EVEN_IN), pos0, cos, sin, ret_s[i],
                lru_h[i].reshape(B, 1, LRU_WIDTH), conv_buf[i],
                w["conv_w"][i], w["conv_b"][i], w["wa"][i], w["ba"][i],
                w["wi"][i], w["bi"][i], w["lam"][i])
            rets.append(rs)
            lrus.append(lh.reshape(B, LRU_WIDTH))
            convs.append(cb)
            w_out = w["w_out_even"][i]
        else:
            proj = _norm_proj_odd(xf, w["norm_mix"][layer], w["w_in_odd"][i],
                                  w["wg1"][i], w["wg2"][i], w["bg"][i])
            mix, gs = _gla_mixer(proj.reshape(B, T, ODD_OUT), gla_s[i], w["gla_norm"][i])
            glas.append(gs)
            w_out = w["w_out_odd"][i]
        xf = _out_mlp(xf, mix.reshape(B * T, D_MODEL), w_out, w["norm_mlp"][layer],
                      w["w_up"][layer], w["w_down"][layer], w["norm_final"],
                      final=(layer == DEPTH - 1))
    return (xf.reshape(B, T, D_MODEL), jnp.stack(rets), jnp.stack(lrus),
            jnp.stack(convs), jnp.stack(glas))


def kernel(x_prompt, x_sample, state_ret, state_lru, state_conv, state_gla, norm_mix, norm_mlp, norm_final, w_in_even, w_out_even, conv_w, conv_b, lru_w_a, lru_b_a, lru_w_i, lru_b_i, lru_lambda, w_in_odd, gla_w_gate2, gla_b_gate, gla_norm, w_out_odd, w_up, w_down):
    n_p = x_prompt.shape[0]
    n_even = state_ret.shape[0]
    n_odd = state_gla.shape[0]
    w = _prepare_weights(norm_mix, norm_mlp, norm_final, w_in_even, w_out_even, conv_w, conv_b,
                         lru_w_a, lru_b_a, lru_w_i, lru_b_i, lru_lambda,
                         w_in_odd, gla_w_gate2, gla_b_gate, gla_norm, w_out_odd, w_up, w_down)
    ret0 = jnp.zeros((n_even, n_p, RET_HEADS, RET_DK, RET_DV), F32)
    lru0 = jnp.zeros((n_even, n_p, LRU_WIDTH), F32)
    conv0 = jnp.zeros((n_even, n_p, CONV_W - 1, LRU_WIDTH), F32)
    gla0 = jnp.zeros((n_odd, n_p, GLA_HEADS, GLA_DK, GLA_DV), F32)
    y_p, ret_p, lru_p, conv_p, gla_p = _trunk(x_prompt, 0, ret0, lru0, conv0, gla0, w)
    y_s, ret_s, lru_s, conv_s, gla_s = _trunk(x_sample, PAST_LEN, state_ret, state_lru,
                                              state_conv, state_gla, w)
    return (y_p, y_s, ret_p, ret_s, lru_p, lru_s, conv_p, conv_s, gla_p, gla_s)
```

```python
import functools
import math

import jax
import jax.numpy as jnp
from jax import lax
from jax.experimental import pallas as pl
from jax.experimental.pallas import tpu as pltpu

F32 = jnp.float32
BF16 = jnp.bfloat16

D_MODEL = 1024
DEPTH = 4
PAST_LEN = 16384
EPS = 1e-6
ROPE_BASE = 10000.0

RET_HEADS = 4
RET_DK = 64
RET_DV = 128
RET_KEY = RET_HEADS * RET_DK
RET_VALUE = RET_HEADS * RET_DV
RET_CHUNK = 128

LRU_WIDTH = 512
CONV_W = 4
LRU_C = 8.0

GLA_HEADS = 4
GLA_KEY = 512
GLA_VALUE = 1024
GLA_DK = GLA_KEY // GLA_HEADS
GLA_DV = GLA_VALUE // GLA_HEADS
GLA_RANK = 16
GLA_TAU = 16.0
GLA_CHUNK = 64

D_FF = 4 * D_MODEL
FF_CHUNK = 1024

EVEN_IN = 2 * RET_KEY + 2 * RET_VALUE + 2 * LRU_WIDTH
ODD_MAIN = 2 * GLA_KEY + 2 * GLA_VALUE
ODD_OUT = ODD_MAIN + GLA_KEY

LANES = 128
SUBLANES = 8
ROW_TILE = 512
MIXER_ROWS = 256
SAMPLE_SEQS = 8
VMEM_LIMIT = 52 * 1024 * 1024


def _dot(a, b):
    return jnp.dot(a.astype(BF16), b.astype(BF16), preferred_element_type=F32)


def _dot_nt(a, b):
    return lax.dot_general(a.astype(BF16), b.astype(BF16), (((1,), (1,)), ((), ())),
                           preferred_element_type=F32)


def _dot_tn(a, b):
    return lax.dot_general(a.astype(BF16), b.astype(BF16), (((0,), (0,)), ((), ())),
                           preferred_element_type=F32)


def _rmsnorm(x, g):
    return x * lax.rsqrt(jnp.mean(x * x, axis=-1, keepdims=True) + EPS) * g


def _head_rms(x):
    return x * lax.rsqrt(jnp.mean(x * x, axis=-1, keepdims=True) + EPS)


def _shift_rows(x, d, fill):
    row = lax.broadcasted_iota(jnp.int32, x.shape, 0)
    return jnp.where(row >= d, pltpu.roll(x, d, axis=0), fill)


def _layer_spec(shape, layer):
    n = len(shape)
    return pl.BlockSpec((None,) + tuple(shape), lambda *_: (layer,) + (0,) * n,
                        pipeline_mode=pl.Buffered(1))


def _const_spec(shape):
    n = len(shape)
    return pl.BlockSpec(shape, lambda *_: (0,) * n, pipeline_mode=pl.Buffered(1))


def _params(*semantics):
    return pltpu.CompilerParams(dimension_semantics=semantics, vmem_limit_bytes=VMEM_LIMIT)


def _norm_proj_even_kernel(x_ref, g_ref, w_ref, o_ref):
    h = _rmsnorm(x_ref[...], g_ref[...])
    o_ref[...] = _dot(h, w_ref[...])


def _norm_proj_odd_kernel(x_ref, g_ref, w_ref, wg1_ref, wg2_ref, bg_ref, o_ref):
    h = _rmsnorm(x_ref[...], g_ref[...]).astype(BF16)
    o_ref[:, :ODD_MAIN] = _dot(h, w_ref[...])
    glr = _dot(h, wg1_ref[...])
    z = _dot(glr, wg2_ref[...]) + bg_ref[...]
    log_sig = jnp.minimum(z, 0.0) - jnp.log1p(jnp.exp(-jnp.abs(z)))
    o_ref[:, ODD_MAIN:] = log_sig / GLA_TAU


def _norm_proj_even(x, w, layer):
    n = x.shape[0]
    tm = min(ROW_TILE, n)
    return pl.pallas_call(
        _norm_proj_even_kernel,
        grid=(n // tm,),
        in_specs=[pl.BlockSpec((tm, D_MODEL), lambda i: (i, 0)),
                  _layer_spec((1, D_MODEL), layer),
                  _layer_spec((D_MODEL, EVEN_IN), layer // 2)],
        out_specs=pl.BlockSpec((tm, EVEN_IN), lambda i: (i, 0)),
        out_shape=jax.ShapeDtypeStruct((n, EVEN_IN), F32),
        compiler_params=_params("parallel"),
        name="norm_proj_even",
    )(x, w["norm_mix"], w["w_in_even"])


def _norm_proj_odd(x, w, layer):
    n = x.shape[0]
    tm = min(ROW_TILE, n)
    i = layer // 2
    return pl.pallas_call(
        _norm_proj_odd_kernel,
        grid=(n // tm,),
        in_specs=[pl.BlockSpec((tm, D_MODEL), lambda i: (i, 0)),
                  _layer_spec((1, D_MODEL), layer),
                  _layer_spec((D_MODEL, ODD_MAIN), i),
                  _layer_spec((D_MODEL, LANES), i),
                  _layer_spec((LANES, GLA_KEY), i),
                  _layer_spec((1, GLA_KEY), i)],
        out_specs=pl.BlockSpec((tm, ODD_OUT), lambda i: (i, 0)),
        out_shape=jax.ShapeDtypeStruct((n, ODD_OUT), F32),
        compiler_params=_params("parallel"),
        name="norm_proj_odd",
    )(x, w["norm_mix"], w["w_in_odd"], w["wg1"], w["wg2"], w["bg"])


def _out_mlp_kernel(final, x_ref, m_ref, wo_ref, g_ref, wu_ref, wd_ref, gf_ref, o_ref):
    x1 = x_ref[...] + _dot(m_ref[...], wo_ref[...])
    h = _rmsnorm(x1, g_ref[...]).astype(BF16)
    acc = None
    for c in range(D_FF // FF_CHUNK):
        cols = slice(c * FF_CHUNK, (c + 1) * FF_CHUNK)
        u = jnp.square(jnp.maximum(_dot(h, wu_ref[:, cols]), 0.0))
        part = _dot(u, wd_ref[cols, :])
        acc = part if acc is None else acc + part
    x2 = x1 + acc
    if final:
        x2 = _rmsnorm(x2, gf_ref[...])
    o_ref[...] = x2


def _out_mlp(x, mix, w, layer):
    n = x.shape[0]
    tm = min(ROW_TILE, n)
    final = layer == DEPTH - 1
    w_out = w["w_out_even"] if layer % 2 == 0 else w["w_out_odd"]
    return pl.pallas_call(
        functools.partial(_out_mlp_kernel, final),
        grid=(n // tm,),
        in_specs=[pl.BlockSpec((tm, D_MODEL), lambda i: (i, 0)),
                  pl.BlockSpec((tm, D_MODEL), lambda i: (i, 0)),
                  _layer_spec((D_MODEL, D_MODEL), layer // 2),
                  _layer_spec((1, D_MODEL), layer),
                  _layer_spec((D_MODEL, D_FF), layer),
                  _layer_spec((D_FF, D_MODEL), layer),
                  _const_spec((1, D_MODEL))],
        out_specs=pl.BlockSpec((tm, D_MODEL), lambda i: (i, 0)),
        out_shape=jax.ShapeDtypeStruct((n, D_MODEL), F32),
        compiler_params=_params("parallel"),
        name="out_mlp_final" if final else "out_mlp",
    )(x, mix, w_out, w["norm_mlp"], w["w_up"], w["w_down"], w["norm_final"])


def _even_mixer_kernel(pos0, chunk, chunks, seqs, single, has_prev, *refs):
    (proj_ref, cos_ref, sin_ref, s0_ref, h0_ref, cb_ref,
     cw_ref, cbias_ref, wa_ref, ba_ref, wi_ref, bi_ref, lam_ref) = refs[:13]
    refs = refs[13 + (1 if has_prev else 0):]
    mix_ref, snew_ref, hnew_ref, cnew_ref, s_scr, h_scr, ext_scr = refs
    C = chunk
    c = pl.program_id(1)
    last = pl.num_programs(1) - 1
    s_src = s0_ref if single else s_scr
    s_dst = snew_ref if single else s_scr
    h_src = h0_ref if single else h_scr
    h_dst = hnew_ref if single else h_scr

    @pl.when(c == 0)
    def _():
        if not single:
            s_scr[...] = s0_ref[...]
            h_scr[...] = h0_ref[...]
        for g in range(seqs):
            ext_scr[g, 0:SUBLANES, :] = jnp.zeros((SUBLANES, LRU_WIDTH), F32)
            ext_scr[g, SUBLANES - (CONV_W - 1):SUBLANES, :] = cb_ref[g]

    lane = lax.broadcasted_iota(jnp.int32, (C, RET_KEY), 1)
    first_half = (lane % RET_DK) < (RET_DK // 2)
    ri = lax.broadcasted_iota(jnp.int32, (C, C), 0)
    ci = lax.broadcasted_iota(jnp.int32, (C, C), 1)
    rel = (ri - ci).astype(F32)
    row = lax.broadcasted_iota(jnp.int32, (C, 1), 0).astype(F32)
    trow = lax.broadcasted_iota(jnp.int32, (C, LRU_WIDTH), 0)
    lam = lam_ref[...]
    softplus_neg_lam = jnp.maximum(-lam, 0.0) + jnp.log1p(jnp.exp(-jnp.abs(lam)))

    v0 = 2 * RET_KEY
    g0 = v0 + RET_VALUE
    x0 = g0 + RET_VALUE
    xg0 = x0 + LRU_WIDTH

    for j in range(chunks):
        rows = slice(j * C, (j + 1) * C)
        cos = cos_ref[rows, :]
        sin = sin_ref[rows, :]

        def rope(x):
            partner = jnp.where(first_half,
                                pltpu.roll(x, RET_KEY - RET_DK // 2, axis=1),
                                pltpu.roll(x, RET_DK // 2, axis=1))
            return x * cos + partner * sin

        for g in range(seqs):
            q = rope(proj_ref[g, rows, 0:RET_KEY])
            k = rope(proj_ref[g, rows, RET_KEY:2 * RET_KEY]) * (RET_DK ** -0.5)
            for h in range(RET_HEADS):
                lg = math.log1p(-(2.0 ** (-5.0 - h)))
                decay_mask = jnp.where(rel >= 0, jnp.exp(jnp.maximum(rel, 0.0) * lg), 0.0)
                q_decay = jnp.exp((row + 1.0) * lg)
                k_decay = jnp.exp((C - 1.0 - row) * lg)
                qh = q[:, h * RET_DK:(h + 1) * RET_DK]
                kh = k[:, h * RET_DK:(h + 1) * RET_DK]
                vh = proj_ref[g, rows, v0 + h * RET_DV:v0 + (h + 1) * RET_DV]
                gh = proj_ref[g, rows, g0 + h * RET_DV:g0 + (h + 1) * RET_DV]
                s = s_src[g, h]
                scores = _dot_nt(qh, kh) * decay_mask
                o = _dot(scores, vh) + _dot(qh * q_decay, s)
                s_dst[g, h] = s * math.exp(C * lg) + _dot_tn(kh * k_decay, vh)
                mix_ref[g, rows, h * RET_DV:(h + 1) * RET_DV] = jax.nn.silu(gh) * _head_rms(o)

            ext_scr[g, SUBLANES:SUBLANES + C, :] = proj_ref[g, rows, x0:x0 + LRU_WIDTH]
            xc = cbias_ref[...]
            for i in range(CONV_W):
                lo = SUBLANES - (CONV_W - 1) + i
                xc = xc + ext_scr[g, lo:lo + C, :] * cw_ref[i:i + 1, :]
            tail = ext_scr[g, C:C + SUBLANES, :]
            if j == chunks - 1:
                @pl.when(c == last)
                def _():
                    cnew_ref[g] = ext_scr[g, C + SUBLANES - (CONV_W - 1):C + SUBLANES, :]
            ext_scr[g, 0:SUBLANES, :] = tail

            r = jax.nn.sigmoid(_dot(xc, wa_ref[...]) + ba_ref[...])
            gate_i = jax.nn.sigmoid(_dot(xc, wi_ref[...]) + bi_ref[...])
            log_a = -LRU_C * r * softplus_neg_lam
            a = jnp.exp(log_a)
            mult = jnp.sqrt(-jnp.tanh(log_a) * (a * a + 1.0))
            pos = trow + (pos0 + (c * chunks + j) * C)
            mult = jnp.where(pos == 0, 1.0, mult)
            b = mult * (gate_i * xc)
            b = b + jnp.where(trow == 0, a * h_src[g], 0.0)
            d = 1
            while d < C:
                b = a * _shift_rows(b, d, 0.0) + b
                a = a * _shift_rows(a, d, 1.0)
                d *= 2
            h_dst[g] = b[C - 1:C, :]
            xg = proj_ref[g, rows, xg0:xg0 + LRU_WIDTH]
            mix_ref[g, rows, RET_VALUE:] = b * jax.nn.gelu(xg)
        s_src, h_src = s_dst, h_dst

    if not single:
        @pl.when(c == last)
        def _():
            snew_ref[...] = s_scr[...]
            hnew_ref[...] = h_scr[...]


def _even_mixer(proj, pos0, cos, sin, ret_s, lru_h, conv_buf, state_layer, w, layer, prev_ret):
    B, T, _ = proj.shape
    i = layer // 2
    n_even = w["w_in_even"].shape[0]
    C = math.gcd(T, RET_CHUNK)
    J = max(1, min(T, MIXER_ROWS) // C)
    G = SAMPLE_SEQS if T == C else 1
    n = T // (C * J)
    single = n == 1
    has_prev = prev_ret is not None
    ret_block = (None, G, RET_HEADS, RET_DK, RET_DV)
    in_specs = [pl.BlockSpec((G, J * C, EVEN_IN), lambda b, c: (b, c, 0)),
                pl.BlockSpec((J * C, RET_KEY), lambda b, c: (c, 0)),
                pl.BlockSpec((J * C, RET_KEY), lambda b, c: (c, 0)),
                pl.BlockSpec(ret_block, lambda b, c: (state_layer, b, 0, 0, 0)),
                pl.BlockSpec((None, G, 1, LRU_WIDTH), lambda b, c: (state_layer, b, 0, 0)),
                pl.BlockSpec((None, G, CONV_W - 1, LRU_WIDTH), lambda b, c: (state_layer, b, 0, 0)),
                _layer_spec((CONV_W, LRU_WIDTH), i),
                _layer_spec((1, LRU_WIDTH), i),
                _layer_spec((LRU_WIDTH, LRU_WIDTH), i),
                _layer_spec((1, LRU_WIDTH), i),
                _layer_spec((LRU_WIDTH, LRU_WIDTH), i),
                _layer_spec((1, LRU_WIDTH), i),
                _layer_spec((1, LRU_WIDTH), i)]
    args = [proj, cos, sin, ret_s, lru_h, conv_buf, w["conv_w"], w["conv_b"],
            w["wa"], w["ba"], w["wi"], w["bi"], w["lam"]]
    aliases = {}
    if has_prev:
        in_specs.append(pl.BlockSpec(memory_space=pl.ANY))
        args.append(prev_ret)
        aliases = {len(args) - 1: 1}
    out_shapes = (jax.ShapeDtypeStruct((B, T, D_MODEL), F32),
                  jax.ShapeDtypeStruct((n_even, B, RET_HEADS, RET_DK, RET_DV), F32),
                  jax.ShapeDtypeStruct((B, 1, LRU_WIDTH), F32),
                  jax.ShapeDtypeStruct((B, CONV_W - 1, LRU_WIDTH), F32))
    return pl.pallas_call(
        functools.partial(_even_mixer_kernel, pos0, C, J, G, single, has_prev),
        grid=(B // G, n),
        in_specs=in_specs,
        out_specs=(pl.BlockSpec((G, J * C, D_MODEL), lambda b, c: (b, c, 0)),
                   pl.BlockSpec(ret_block, lambda b, c: (i, b, 0, 0, 0)),
                   pl.BlockSpec((G, 1, LRU_WIDTH), lambda b, c: (b, 0, 0)),
                   pl.BlockSpec((G, CONV_W - 1, LRU_WIDTH), lambda b, c: (b, 0, 0))),
        out_shape=out_shapes,
        scratch_shapes=[pltpu.VMEM((G, RET_HEADS, RET_DK, RET_DV), F32),
                        pltpu.VMEM((G, 1, LRU_WIDTH), F32),
                        pltpu.VMEM((G, C + SUBLANES, LRU_WIDTH), F32)],
        input_output_aliases=aliases,
        compiler_params=_params("parallel", "arbitrary"),
        name="even_mixer",
    )(*args)


def _gla_mixer_kernel(chunk, chunks, seqs, single, has_prev, *refs):
    proj_ref, s0_ref, nw_ref = refs[:3]
    refs = refs[3 + (1 if has_prev else 0):]
    mix_ref, snew_ref, s_scr = refs
    C = chunk
    c = pl.program_id(1)
    last = pl.num_programs(1) - 1
    s_src = s0_ref if single else s_scr
    s_dst = snew_ref if single else s_scr

    if not single:
        @pl.when(c == 0)
        def _():
            s_scr[...] = s0_ref[...]

    k0 = GLA_KEY
    v0 = 2 * GLA_KEY
    r0 = v0 + GLA_VALUE
    la0 = r0 + GLA_VALUE

    ri = lax.broadcasted_iota(jnp.int32, (C, C), 0)
    ci = lax.broadcasted_iota(jnp.int32, (C, C), 1)
    causal = ri >= ci
    di = lax.broadcasted_iota(jnp.int32, (GLA_DK, GLA_DK), 0)
    dj = lax.broadcasted_iota(jnp.int32, (GLA_DK, GLA_DK), 1)
    eye = di == dj
    nw = nw_ref[...]

    for j in range(chunks):
        rows = slice(j * C, (j + 1) * C)
        for g in range(seqs):
            b = proj_ref[g, rows, la0:la0 + GLA_KEY]
            d = 1
            while d < C:
                b = b + _shift_rows(b, d, 0.0)
                d *= 2
            b_mid = b[C // 2:C // 2 + 1, :]
            b_last = b[C - 1:C, :]
            q = proj_ref[g, rows, 0:GLA_KEY] * (GLA_DK ** -0.5)
            k = proj_ref[g, rows, k0:k0 + GLA_KEY]
            q_in = q * jnp.exp(b - b_mid)
            k_in = k * jnp.exp(b_mid - b)
            q_st = q * jnp.exp(b)
            k_st = k * jnp.exp(b_last - b)
            s_decay = jnp.exp(b_last)

            for h in range(GLA_HEADS):
                ks = slice(h * GLA_DK, (h + 1) * GLA_DK)
                vh = proj_ref[g, rows, v0 + h * GLA_DV:v0 + (h + 1) * GLA_DV]
                rh = proj_ref[g, rows, r0 + h * GLA_DV:r0 + (h + 1) * GLA_DV]
                s = s_src[g, h]
                scores = jnp.where(causal, _dot_nt(q_in[:, ks], k_in[:, ks]), 0.0)
                o = _dot(scores, vh) + _dot(q_st[:, ks], s)
                decay_col = jnp.sum(jnp.where(eye, s_decay[:, ks], 0.0), axis=1, keepdims=True)
                s_dst[g, h] = s * decay_col + _dot_tn(k_st[:, ks], vh)
                mix_ref[g, rows, h * GLA_DV:(h + 1) * GLA_DV] = (
                    jax.nn.silu(rh) * (_head_rms(o) * nw))
        s_src = s_dst

    if not single:
        @pl.when(c == last)
        def _():
            snew_ref[...] = s_scr[...]


def _gla_mixer(proj, gla_s, state_layer, w, layer, prev_gla):
    B, T, _ = proj.shape
    i = layer // 2
    n_odd = w["w_in_odd"].shape[0]
    C = math.gcd(T, GLA_CHUNK)
    J = max(1, min(T, MIXER_ROWS) // C)
    G = SAMPLE_SEQS if T == C else 1
    n = T // (C * J)
    single = n == 1
    has_prev = prev_gla is not None
    state = (None, G, GLA_HEADS, GLA_DK, GLA_DV)
    in_specs = [pl.BlockSpec((G, J * C, ODD_OUT), lambda b, c: (b, c, 0)),
                pl.BlockSpec(state, lambda b, c: (state_layer, b, 0, 0, 0)),
                _layer_spec((1, GLA_DV), i)]
    args = [proj, gla_s, w["gla_norm"]]
    aliases = {}
    if has_prev:
        in_specs.append(pl.BlockSpec(memory_space=pl.ANY))
        args.append(prev_gla)
        aliases = {len(args) - 1: 1}
    return pl.pallas_call(
        functools.partial(_gla_mixer_kernel, C, J, G, single, has_prev),
        grid=(B // G, n),
        in_specs=in_specs,
        out_specs=(pl.BlockSpec((G, J * C, D_MODEL), lambda b, c: (b, c, 0)),
                   pl.BlockSpec(state, lambda b, c: (i, b, 0, 0, 0))),
        out_shape=(jax.ShapeDtypeStruct((B, T, D_MODEL), F32),
                   jax.ShapeDtypeStruct((n_odd, B, GLA_HEADS, GLA_DK, GLA_DV), F32)),
        scratch_shapes=[pltpu.VMEM((G, GLA_HEADS, GLA_DK, GLA_DV), F32)],
        input_output_aliases=aliases,
        compiler_params=_params("parallel", "arbitrary"),
        name="gla_mixer",
    )(*args)


def _rope_tables(pos0, T):
    half = RET_DK // 2
    inv = ROPE_BASE ** (-jnp.arange(half, dtype=F32) / half)
    ang = (pos0 + jnp.arange(T, dtype=jnp.int32)).astype(F32)[:, None] * inv[None, :]
    cos = jnp.cos(ang)
    sin = jnp.sin(ang)
    cos_full = jnp.tile(jnp.concatenate([cos, cos], axis=-1), (1, RET_HEADS))
    sin_signed = jnp.tile(jnp.concatenate([-sin, sin], axis=-1), (1, RET_HEADS))
    return cos_full, sin_signed


def _block_diag(w):
    nl, nb, c, d = w.shape
    eye = jnp.eye(nb, dtype=w.dtype)
    return (eye[None, :, None, :, None] * w[:, :, :, None, :]).reshape(nl, nb * c, nb * d)


def _prepare_weights(norm_mix, norm_mlp, norm_final, w_in_even, w_out_even, conv_w, conv_b,
                     lru_w_a, lru_b_a, lru_w_i, lru_b_i, lru_lambda,
                     w_in_odd, gla_w_gate2, gla_b_gate, gla_norm, w_out_odd, w_up, w_down):
    n_even = w_in_even.shape[0]
    n_odd = w_in_odd.shape[0]
    pad1 = LANES - GLA_RANK
    return dict(
        norm_mix=norm_mix.reshape(DEPTH, 1, D_MODEL),
        norm_mlp=norm_mlp.reshape(DEPTH, 1, D_MODEL),
        norm_final=norm_final.reshape(1, D_MODEL),
        w_in_even=w_in_even.astype(BF16),
        w_out_even=w_out_even.astype(BF16),
        conv_w=conv_w,
        conv_b=conv_b.reshape(n_even, 1, LRU_WIDTH),
        wa=_block_diag(lru_w_a).astype(BF16),
        ba=lru_b_a.reshape(n_even, 1, LRU_WIDTH),
        wi=_block_diag(lru_w_i).astype(BF16),
        bi=lru_b_i.reshape(n_even, 1, LRU_WIDTH),
        lam=lru_lambda.reshape(n_even, 1, LRU_WIDTH),
        w_in_odd=w_in_odd.astype(BF16),
        wg1=jnp.pad(w_in_odd[:, :, ODD_MAIN:], ((0, 0), (0, 0), (0, pad1))).astype(BF16),
        wg2=jnp.pad(gla_w_gate2, ((0, 0), (0, pad1), (0, 0))).astype(BF16),
        bg=gla_b_gate.reshape(n_odd, 1, GLA_KEY),
        gla_norm=gla_norm.reshape(n_odd, 1, GLA_DV),
        w_out_odd=w_out_odd.astype(BF16),
        w_up=w_up.astype(BF16),
        w_down=w_down.astype(BF16),
    )


def _trunk(x, pos0, ret_s, lru_h, conv_buf, gla_s, fresh, w):
    B, T, _ = x.shape
    xf = x.reshape(B * T, D_MODEL)
    cos, sin = _rope_tables(pos0, T)
    lru_h = lru_h.reshape(lru_h.shape[0], B, 1, LRU_WIDTH)
    ret_new, gla_new = None, None
    lrus, convs = [], []
    for layer in range(DEPTH):
        state_layer = 0 if fresh else layer // 2
        if layer % 2 == 0:
            proj = _norm_proj_even(xf, w, layer)
            mix, ret_new, lh, cb = _even_mixer(
                proj.reshape(B, T, EVEN_IN), pos0, cos, sin, ret_s, lru_h, conv_buf,
                state_layer, w, layer, ret_new)
            lrus.append(lh.reshape(B, LRU_WIDTH))
            convs.append(cb)
        else:
            proj = _norm_proj_odd(xf, w, layer)
            mix, gla_new = _gla_mixer(proj.reshape(B, T, ODD_OUT), gla_s, state_layer, w, layer,
                                      gla_new)
        xf = _out_mlp(xf, mix.reshape(B * T, D_MODEL), w, layer)
    return xf.reshape(B, T, D_MODEL), ret_new, jnp.stack(lrus), jnp.stack(convs), gla_new


def kernel(x_prompt, x_sample, state_ret, state_lru, state_conv, state_gla, norm_mix, norm_mlp, norm_final, w_in_even, w_out_even, conv_w, conv_b, lru_w_a, lru_b_a, lru_w_i, lru_b_i, lru_lambda, w_in_odd, gla_w_gate2, gla_b_gate, gla_norm, w_out_odd, w_up, w_down):
    n_p = x_prompt.shape[0]
    w = _prepare_weights(norm_mix, norm_mlp, norm_final, w_in_even, w_out_even, conv_w, conv_b,
                         lru_w_a, lru_b_a, lru_w_i, lru_b_i, lru_lambda,
                         w_in_odd, gla_w_gate2, gla_b_gate, gla_norm, w_out_odd, w_up, w_down)
    ret0 = jnp.zeros((1, n_p, RET_HEADS, RET_DK, RET_DV), F32)
    lru0 = jnp.zeros((1, n_p, LRU_WIDTH), F32)
    conv0 = jnp.zeros((1, n_p, CONV_W - 1, LRU_WIDTH), F32)
    gla0 = jnp.zeros((1, n_p, GLA_HEADS, GLA_DK, GLA_DV), F32)
    y_p, ret_p, lru_p, conv_p, gla_p = _trunk(x_prompt, 0, ret0, lru0, conv0, gla0, True, w)
    y_s, ret_s, lru_s, conv_s, gla_s = _trunk(x_sample, PAST_LEN, state_ret, state_lru,
                                              state_conv, state_gla, False, w)
    return (y_p, y_s, ret_p, ret_s, lru_p, lru_s, conv_p, conv_s, gla_p, gla_s)
```

```python
import functools
import math

import jax
import jax.numpy as jnp
from jax import lax
from jax.experimental import pallas as pl
from jax.experimental.pallas import tpu as pltpu

F32 = jnp.float32
BF16 = jnp.bfloat16

D_MODEL = 1024
DEPTH = 4
PAST_LEN = 16384
EPS = 1e-6
ROPE_BASE = 10000.0

RET_HEADS = 4
RET_DK = 64
RET_DV = 128
RET_KEY = RET_HEADS * RET_DK
RET_VALUE = RET_HEADS * RET_DV
RET_CHUNK = 128

LRU_WIDTH = 512
CONV_W = 4
LRU_C = 8.0

GLA_HEADS = 4
GLA_KEY = 512
GLA_VALUE = 1024
GLA_DK = GLA_KEY // GLA_HEADS
GLA_DV = GLA_VALUE // GLA_HEADS
GLA_RANK = 16
GLA_TAU = 16.0
GLA_CHUNK = 64

D_FF = 4 * D_MODEL
FF_CHUNK = 1024

EVEN_IN = 2 * RET_KEY + 2 * RET_VALUE + 2 * LRU_WIDTH
ODD_MAIN = 2 * GLA_KEY + 2 * GLA_VALUE
ODD_OUT = ODD_MAIN + GLA_KEY

LANES = 128
SUBLANES = 8
ROW_TILE = 512
FUSED_ROWS = 256
MXU_PIECE = 256
SAMPLE_SEQS = 8
VMEM_LIMIT = 56 * 1024 * 1024


def _dot(a, b):
    return jnp.dot(a.astype(BF16), b.astype(BF16), preferred_element_type=F32)


def _dot_nt(a, b):
    return lax.dot_general(a.astype(BF16), b.astype(BF16), (((1,), (1,)), ((), ())),
                           preferred_element_type=F32)


def _dot_tn(a, b):
    return lax.dot_general(a.astype(BF16), b.astype(BF16), (((0,), (0,)), ((), ())),
                           preferred_element_type=F32)


def _rmsnorm(x, g):
    return x * lax.rsqrt(jnp.mean(x * x, axis=-1, keepdims=True) + EPS) * g


def _head_rms(x):
    return x * lax.rsqrt(jnp.mean(x * x, axis=-1, keepdims=True) + EPS)


def _shift_rows(x, d, fill):
    row = lax.broadcasted_iota(jnp.int32, x.shape, 0)
    return jnp.where(row >= d, pltpu.roll(x, d, axis=0), fill)


def _layer_spec(shape, layer):
    n = len(shape)
    return pl.BlockSpec((None,) + tuple(shape), lambda *_: (layer,) + (0,) * n,
                        pipeline_mode=pl.Buffered(1))


def _const_spec(shape):
    n = len(shape)
    return pl.BlockSpec(shape, lambda *_: (0,) * n, pipeline_mode=pl.Buffered(1))


def _params(*semantics):
    return pltpu.CompilerParams(dimension_semantics=semantics, vmem_limit_bytes=VMEM_LIMIT)


def _log_gate(h, wg1_ref, wg2_ref, bg_ref):
    glr = _dot(h, wg1_ref[...])
    z = _dot(glr, wg2_ref[...]) + bg_ref[...]
    log_sig = jnp.minimum(z, 0.0) - jnp.log1p(jnp.exp(-jnp.abs(z)))
    return log_sig / GLA_TAU


def _out_mlp_math(final, x, m, wo_ref, g_ref, wu_ref, wd_ref, gf_ref):
    x1 = x + _dot(m, wo_ref[...])
    h = _rmsnorm(x1, g_ref[...]).astype(BF16)
    acc = None
    for c in range(D_FF // FF_CHUNK):
        cols = slice(c * FF_CHUNK, (c + 1) * FF_CHUNK)
        u = jnp.square(jnp.maximum(_dot(h, wu_ref[:, cols]), 0.0))
        part = _dot(u, wd_ref[cols, :])
        acc = part if acc is None else acc + part
    x2 = x1 + acc
    if final:
        x2 = _rmsnorm(x2, gf_ref[...])
    return x2


def _norm_proj_even_kernel(x_ref, g_ref, w_ref, o_ref):
    h = _rmsnorm(x_ref[...], g_ref[...])
    o_ref[...] = _dot(h, w_ref[...])


def _norm_proj_odd_kernel(x_ref, g_ref, w_ref, wg1_ref, wg2_ref, bg_ref, o_ref):
    h = _rmsnorm(x_ref[...], g_ref[...]).astype(BF16)
    o_ref[:, :ODD_MAIN] = _dot(h, w_ref[...])
    o_ref[:, ODD_MAIN:] = _log_gate(h, wg1_ref, wg2_ref, bg_ref)


def _out_mlp_kernel(final, x_ref, m_ref, wo_ref, g_ref, wu_ref, wd_ref, gf_ref, o_ref):
    o_ref[...] = _out_mlp_math(final, x_ref[...], m_ref[...], wo_ref, g_ref, wu_ref, wd_ref,
                               gf_ref)


def _norm_proj_even(x, w, layer):
    n = x.shape[0]
    tm = min(ROW_TILE, n)
    return pl.pallas_call(
        _norm_proj_even_kernel,
        grid=(n // tm,),
        in_specs=[pl.BlockSpec((tm, D_MODEL), lambda i: (i, 0)),
                  _layer_spec((1, D_MODEL), layer),
                  _layer_spec((D_MODEL, EVEN_IN), layer // 2)],
        out_specs=pl.BlockSpec((tm, EVEN_IN), lambda i: (i, 0)),
        out_shape=jax.ShapeDtypeStruct((n, EVEN_IN), F32),
        compiler_params=_params("parallel"),
        name="norm_proj_even",
    )(x, w["norm_mix"], w["w_in_even"])


def _norm_proj_odd(x, w, layer):
    n = x.shape[0]
    tm = min(ROW_TILE, n)
    i = layer // 2
    return pl.pallas_call(
        _norm_proj_odd_kernel,
        grid=(n // tm,),
        in_specs=[pl.BlockSpec((tm, D_MODEL), lambda i: (i, 0)),
                  _layer_spec((1, D_MODEL), layer),
                  _layer_spec((D_MODEL, ODD_MAIN), i),
                  _layer_spec((D_MODEL, LANES), i),
                  _layer_spec((LANES, GLA_KEY), i),
                  _layer_spec((1, GLA_KEY), i)],
        out_specs=pl.BlockSpec((tm, ODD_OUT), lambda i: (i, 0)),
        out_shape=jax.ShapeDtypeStruct((n, ODD_OUT), F32),
        compiler_params=_params("parallel"),
        name="norm_proj_odd",
    )(x, w["norm_mix"], w["w_in_odd"], w["wg1"], w["wg2"], w["bg"])


def _out_mlp(x, mix, w, layer):
    n = x.shape[0]
    tm = min(ROW_TILE, n)
    final = layer == DEPTH - 1
    w_out = w["w_out_even"] if layer % 2 == 0 else w["w_out_odd"]
    return pl.pallas_call(
        functools.partial(_out_mlp_kernel, final),
        grid=(n // tm,),
        in_specs=[pl.BlockSpec((tm, D_MODEL), lambda i: (i, 0)),
                  pl.BlockSpec((tm, D_MODEL), lambda i: (i, 0)),
                  _layer_spec((D_MODEL, D_MODEL), layer // 2),
                  _layer_spec((1, D_MODEL), layer),
                  _layer_spec((D_MODEL, D_FF), layer),
                  _layer_spec((D_FF, D_MODEL), layer),
                  _const_spec((1, D_MODEL))],
        out_specs=pl.BlockSpec((tm, D_MODEL), lambda i: (i, 0)),
        out_shape=jax.ShapeDtypeStruct((n, D_MODEL), F32),
        compiler_params=_params("parallel"),
        name="out_mlp_final" if final else "out_mlp",
    )(x, mix, w_out, w["norm_mlp"], w["w_up"], w["w_down"], w["norm_final"])


def _run(stream):
    for _ in stream:
        pass


def _interleave(*streams):
    done = [0.0] * len(streams)
    alive = list(range(len(streams)))
    while alive:
        i = min(alive, key=lambda j: done[j])
        try:
            done[i] += next(streams[i])
        except StopIteration:
            alive.remove(i)


def _even_mixer_tile(C, chunks, pos_base, load, store, cos_ref, sin_ref,
                     s_src, s_dst, h_src, h_dst, tail_ref, weights):
    cw_ref, cbias_ref, wa_ref, ba_ref, wi_ref, bi_ref, lam_ref = weights
    lane = lax.broadcasted_iota(jnp.int32, (C, RET_KEY), 1)
    first_half = (lane % RET_DK) < (RET_DK // 2)
    ri = lax.broadcasted_iota(jnp.int32, (C, C), 0)
    ci = lax.broadcasted_iota(jnp.int32, (C, C), 1)
    rel = (ri - ci).astype(F32)
    row = lax.broadcasted_iota(jnp.int32, (C, 1), 0).astype(F32)
    trow = lax.broadcasted_iota(jnp.int32, (C, LRU_WIDTH), 0)
    row8 = lax.broadcasted_iota(jnp.int32, (SUBLANES, LRU_WIDTH), 0)
    lam = lam_ref[...]
    softplus_neg_lam = jnp.maximum(-lam, 0.0) + jnp.log1p(jnp.exp(-jnp.abs(lam)))
    log_decay = [math.log1p(-(2.0 ** (-5.0 - h))) for h in range(RET_HEADS)]
    decay_mask = [jnp.where(rel >= 0, jnp.exp(jnp.maximum(rel, 0.0) * lg), 0.0)
                  for lg in log_decay]
    q_decay = [jnp.exp((row + 1.0) * lg) for lg in log_decay]
    k_decay = [jnp.exp((C - 1.0 - row) * lg) for lg in log_decay]
    n_scan = max(1, int(math.log2(C)))

    v0 = 2 * RET_KEY
    g0 = v0 + RET_VALUE
    x0 = g0 + RET_VALUE
    xg0 = x0 + LRU_WIDTH
    heads = range(RET_HEADS)

    for j in range(chunks):
        rows = slice(j * C, (j + 1) * C)
        cos = cos_ref[rows, :]
        sin = sin_ref[rows, :]

        def rope(x):
            partner = jnp.where(first_half,
                                pltpu.roll(x, RET_KEY - RET_DK // 2, axis=1),
                                pltpu.roll(x, RET_DK // 2, axis=1))
            return x * cos + partner * sin

        q = rope(load(rows, slice(0, RET_KEY)))
        k = rope(load(rows, slice(RET_KEY, 2 * RET_KEY))) * (RET_DK ** -0.5)
        qh = [q[:, h * RET_DK:(h + 1) * RET_DK] for h in heads]
        kh = [k[:, h * RET_DK:(h + 1) * RET_DK] for h in heads]
        q_dec = [(qh[h] * q_decay[h]).astype(BF16) for h in heads]
        k_dec = [(kh[h] * k_decay[h]).astype(BF16) for h in heads]
        vh = [load(rows, slice(v0 + h * RET_DV, v0 + (h + 1) * RET_DV)).astype(BF16)
              for h in heads]
        yield 0.08 / chunks

        x_new = load(rows, slice(x0, x0 + LRU_WIDTH))
        tail = tail_ref[...]

        def delayed(x, t, n):
            rolled = pltpu.roll(x, n, axis=0)
            head = jnp.where(row8 < n, pltpu.roll(t, n, axis=0), rolled[0:SUBLANES, :])
            if C == SUBLANES:
                return head
            return jnp.concatenate([head, rolled[SUBLANES:, :]], axis=0)

        xc = cbias_ref[...]
        for i in range(CONV_W - 1):
            xc = xc + delayed(x_new, tail, CONV_W - 1 - i) * cw_ref[i:i + 1, :]
        xc = xc + x_new * cw_ref[CONV_W - 1:CONV_W, :]
        tail_ref[...] = x_new[C - SUBLANES:C, :]
        yield 0.10 / chunks

        s_old = [s_src[h] for h in heads]
        scores = [_dot_nt(qh[h], kh[h]) for h in heads]
        cross = [_dot(q_dec[h], s_old[h]) for h in heads]
        kv = [_dot_tn(k_dec[h], vh[h]) for h in heads]
        r_pre = _dot(xc, wa_ref[...])
        i_pre = _dot(xc, wi_ref[...])
        yield 0.0

        probs = [(scores[h] * decay_mask[h]).astype(BF16) for h in heads]
        for h in heads:
            s_dst[h] = s_old[h] * math.exp(C * log_decay[h]) + kv[h]
        yield 0.06 / chunks
        r = jax.nn.sigmoid(r_pre + ba_ref[...])
        gate_i = jax.nn.sigmoid(i_pre + bi_ref[...])
        log_a = -LRU_C * r * softplus_neg_lam
        a = jnp.exp(log_a)
        y = -jnp.tanh(log_a) * (a * a + 1.0)
        mult = jnp.where(y > 0.0, y * lax.rsqrt(y), 0.0)
        pos = trow + (pos_base + j * C)
        mult = jnp.where(pos == 0, 1.0, mult)
        b = mult * (gate_i * xc)
        b = b + jnp.where(trow == 0, a * h_src[...], 0.0)
        yield 0.20 / chunks

        out = [_dot(probs[h], vh[h]) + cross[h] for h in heads]
        yield 0.0

        for h in heads:
            gh = load(rows, slice(g0 + h * RET_DV, g0 + (h + 1) * RET_DV))
            store(rows, slice(h * RET_DV, (h + 1) * RET_DV), jax.nn.silu(gh) * _head_rms(out[h]))
            yield 0.03 / chunks
        d = 1
        while d < C:
            b = a * _shift_rows(b, d, 0.0) + b
            a = a * _shift_rows(a, d, 1.0)
            d *= 2
            yield 0.36 / (chunks * n_scan)
        h_dst[...] = b[C - 1:C, :]
        xg = load(rows, slice(xg0, xg0 + LRU_WIDTH))
        store(rows, slice(RET_VALUE, RET_VALUE + LRU_WIDTH), b * jax.nn.gelu(xg))
        yield 0.08 / chunks
        s_src, h_src = s_dst, h_dst


def _gla_mixer_tile(C, chunks, load, store, s_src, s_dst, nw):
    k0 = GLA_KEY
    v0 = 2 * GLA_KEY
    r0 = v0 + GLA_VALUE
    la0 = r0 + GLA_VALUE
    ri = lax.broadcasted_iota(jnp.int32, (C, C), 0)
    ci = lax.broadcasted_iota(jnp.int32, (C, C), 1)
    causal = ri >= ci
    di = lax.broadcasted_iota(jnp.int32, (GLA_DK, GLA_DK), 0)
    dj = lax.broadcasted_iota(jnp.int32, (GLA_DK, GLA_DK), 1)
    eye = di == dj
    heads = range(GLA_HEADS)
    key = [slice(h * GLA_DK, (h + 1) * GLA_DK) for h in heads]

    for j in range(chunks):
        rows = slice(j * C, (j + 1) * C)
        b = load(rows, slice(la0, la0 + GLA_KEY))
        d = 1
        while d < C:
            b = b + _shift_rows(b, d, 0.0)
            d *= 2
        yield 0.12 / chunks
        b_mid = b[C // 2:C // 2 + 1, :]
        b_last = b[C - 1:C, :]
        q = load(rows, slice(0, GLA_KEY)) * (GLA_DK ** -0.5)
        k = load(rows, slice(k0, k0 + GLA_KEY))
        q_in = (q * jnp.exp(b - b_mid)).astype(BF16)
        k_in = (k * jnp.exp(b_mid - b)).astype(BF16)
        q_st = (q * jnp.exp(b)).astype(BF16)
        k_st = (k * jnp.exp(b_last - b)).astype(BF16)
        s_decay = jnp.exp(b_last)
        decay_col = [jnp.sum(jnp.where(eye, s_decay[:, key[h]], 0.0), axis=1, keepdims=True)
                     for h in heads]
        vh = [load(rows, slice(v0 + h * GLA_DV, v0 + (h + 1) * GLA_DV)).astype(BF16)
              for h in heads]
        yield 0.28 / chunks

        s_old = [s_src[h] for h in heads]
        scores = [_dot_nt(q_in[:, key[h]], k_in[:, key[h]]) for h in heads]
        cross = [_dot(q_st[:, key[h]], s_old[h]) for h in heads]
        kv = [_dot_tn(k_st[:, key[h]], vh[h]) for h in heads]
        yield 0.0

        probs = [jnp.where(causal, scores[h], 0.0).astype(BF16) for h in heads]
        for h in heads:
            s_dst[h] = s_old[h] * decay_col[h] + kv[h]
        yield 0.16 / chunks

        out = [_dot(probs[h], vh[h]) + cross[h] for h in heads]
        yield 0.0

        for h in heads:
            rh = load(rows, slice(r0 + h * GLA_DV, r0 + (h + 1) * GLA_DV))
            store(rows, slice(h * GLA_DV, (h + 1) * GLA_DV),
                  jax.nn.silu(rh) * (_head_rms(out[h]) * nw))
            yield 0.11 / chunks
        s_src = s_dst


def _even_mixer_kernel(pos0, chunk, seqs, has_prev, *refs):
    (proj_ref, cos_ref, sin_ref, s0_ref, h0_ref, cb_ref,
     cw_ref, cbias_ref, wa_ref, ba_ref, wi_ref, bi_ref, lam_ref) = refs[:13]
    refs = refs[13 + (1 if has_prev else 0):]
    mix_ref, snew_ref, hnew_ref, cnew_ref, tail_scr = refs
    weights = (cw_ref, cbias_ref, wa_ref, ba_ref, wi_ref, bi_ref, lam_ref)
    for g in range(seqs):
        tail = tail_scr.at[g]
        tail[...] = jnp.zeros((SUBLANES, LRU_WIDTH), F32)
        tail[SUBLANES - (CONV_W - 1):SUBLANES, :] = cb_ref[g]

        def load(rows, cols, g=g):
            return proj_ref[g, rows, cols]

        def store(rows, cols, val, g=g):
            mix_ref[g, rows, cols] = val

        _run(_even_mixer_tile(chunk, 1, pos0, load, store, cos_ref, sin_ref,
                              s0_ref.at[g], snew_ref.at[g], h0_ref.at[g], hnew_ref.at[g],
                              tail, weights))
        cnew_ref[g] = tail[SUBLANES - (CONV_W - 1):SUBLANES, :]


def _even_mixer(proj, pos0, cos, sin, ret_s, lru_h, conv_buf, w, layer, prev_ret):
    B, T, _ = proj.shape
    i = layer // 2
    n_even = w["w_in_even"].shape[0]
    G = SAMPLE_SEQS
    has_prev = prev_ret is not None
    ret_block = (None, G, RET_HEADS, RET_DK, RET_DV)
    in_specs = [pl.BlockSpec((G, T, EVEN_IN), lambda b: (b, 0, 0)),
                _const_spec((T, RET_KEY)),
                _const_spec((T, RET_KEY)),
                pl.BlockSpec(ret_block, lambda b: (i, b, 0, 0, 0)),
                pl.BlockSpec((None, G, 1, LRU_WIDTH), lambda b: (i, b, 0, 0)),
                pl.BlockSpec((None, G, CONV_W - 1, LRU_WIDTH), lambda b: (i, b, 0, 0)),
                _layer_spec((CONV_W, LRU_WIDTH), i),
                _layer_spec((1, LRU_WIDTH), i),
                _layer_spec((LRU_WIDTH, LRU_WIDTH), i),
                _layer_spec((1, LRU_WIDTH), i),
                _layer_spec((LRU_WIDTH, LRU_WIDTH), i),
                _layer_spec((1, LRU_WIDTH), i),
                _layer_spec((1, LRU_WIDTH), i)]
    args = [proj, cos, sin, ret_s, lru_h, conv_buf, w["conv_w"], w["conv_b"],
            w["wa"], w["ba"], w["wi"], w["bi"], w["lam"]]
    aliases = {}
    if has_prev:
        in_specs.append(pl.BlockSpec(memory_space=pl.ANY))
        args.append(prev_ret)
        aliases = {len(args) - 1: 1}
    out_shapes = (jax.ShapeDtypeStruct((B, T, D_MODEL), F32),
                  jax.ShapeDtypeStruct((n_even, B, RET_HEADS, RET_DK, RET_DV), F32),
                  jax.ShapeDtypeStruct((B, 1, LRU_WIDTH), F32),
                  jax.ShapeDtypeStruct((B, CONV_W - 1, LRU_WIDTH), F32))
    return pl.pallas_call(
        functools.partial(_even_mixer_kernel, pos0, T, G, has_prev),
        grid=(B // G,),
        in_specs=in_specs,
        out_specs=(pl.BlockSpec((G, T, D_MODEL), lambda b: (b, 0, 0)),
                   pl.BlockSpec(ret_block, lambda b: (i, b, 0, 0, 0)),
                   pl.BlockSpec((G, 1, LRU_WIDTH), lambda b: (b, 0, 0)),
                   pl.BlockSpec((G, CONV_W - 1, LRU_WIDTH), lambda b: (b, 0, 0))),
        out_shape=out_shapes,
        scratch_shapes=[pltpu.VMEM((G, SUBLANES, LRU_WIDTH), F32)],
        input_output_aliases=aliases,
        compiler_params=_params("parallel"),
        name="even_mixer",
    )(*args)


def _gla_mixer_kernel(chunk, seqs, has_prev, *refs):
    proj_ref, s0_ref, nw_ref = refs[:3]
    refs = refs[3 + (1 if has_prev else 0):]
    mix_ref, snew_ref = refs
    nw = nw_ref[...]
    for g in range(seqs):
        def load(rows, cols, g=g):
            return proj_ref[g, rows, cols]

        def store(rows, cols, val, g=g):
            mix_ref[g, rows, cols] = val

        _run(_gla_mixer_tile(chunk, 1, load, store, s0_ref.at[g], snew_ref.at[g], nw))


def _gla_mixer(proj, gla_s, w, layer, prev_gla):
    B, T, _ = proj.shape
    i = layer // 2
    n_odd = w["w_in_odd"].shape[0]
    G = SAMPLE_SEQS
    has_prev = prev_gla is not None
    state = (None, G, GLA_HEADS, GLA_DK, GLA_DV)
    in_specs = [pl.BlockSpec((G, T, ODD_OUT), lambda b: (b, 0, 0)),
                pl.BlockSpec(state, lambda b: (i, b, 0, 0, 0)),
                _layer_spec((1, GLA_DV), i)]
    args = [proj, gla_s, w["gla_norm"]]
    aliases = {}
    if has_prev:
        in_specs.append(pl.BlockSpec(memory_space=pl.ANY))
        args.append(prev_gla)
        aliases = {len(args) - 1: 1}
    return pl.pallas_call(
        functools.partial(_gla_mixer_kernel, T, G, has_prev),
        grid=(B // G,),
        in_specs=in_specs,
        out_specs=(pl.BlockSpec((G, T, D_MODEL), lambda b: (b, 0, 0)),
                   pl.BlockSpec(state, lambda b: (i, b, 0, 0, 0))),
        out_shape=(jax.ShapeDtypeStruct((B, T, D_MODEL), F32),
                   jax.ShapeDtypeStruct((n_odd, B, GLA_HEADS, GLA_DK, GLA_DV), F32)),
        input_output_aliases=aliases,
        compiler_params=_params("parallel"),
        name="gla_mixer",
    )(*args)


def _fused_layer_kernel(even, final, chunk, rows, tiles_per_seq, n_tiles, has_prev, *refs):
    it = iter(refs)
    xlead_ref, xlag_ref, gmix_ref, win_ref = (next(it) for _ in range(4))
    if even:
        cos_ref, sin_ref = next(it), next(it)
        mixer_weights = tuple(next(it) for _ in range(7))
    else:
        wg1_ref, wg2_ref, bg_ref, nw_ref = (next(it) for _ in range(4))
    wo_ref, gmlp_ref, wu_ref, wd_ref, gf_ref = (next(it) for _ in range(5))
    if has_prev:
        next(it)
    out_ref, snew_ref = next(it), next(it)
    if even:
        hnew_ref, cnew_ref = next(it), next(it)
    proj_scr = (next(it), next(it))
    mix_scr = (next(it), next(it))
    x1_scr, u_scr, s_scr = next(it), next(it), next(it)
    if even:
        h_scr, tail_scr = next(it), next(it)

    s = pl.program_id(0)
    tile_in_seq = lax.rem(jnp.clip(s - 1, 0, n_tiles - 1), tiles_per_seq)

    @pl.when(s == 0)
    def _():
        proj_scr[1][...] = jnp.zeros(proj_scr[1].shape, F32)
        mix_scr[0][...] = jnp.zeros(mix_scr[0].shape, BF16)

    @pl.when((s == 0) | (lax.rem(s + tiles_per_seq - 1, tiles_per_seq) == 0))
    def _():
        s_scr[...] = jnp.zeros(s_scr.shape, F32)
        if even:
            h_scr[...] = jnp.zeros(h_scr.shape, F32)
            tail_scr[...] = jnp.zeros(tail_scr.shape, F32)

    main_cols = EVEN_IN if even else ODD_MAIN
    macs_per_row = D_MODEL * (main_cols + D_MODEL + 2 * D_FF)

    def matrix_stream(proj_a, mix_c):
        P = MXU_PIECE
        h = _rmsnorm(xlead_ref[...], gmix_ref[...]).astype(BF16)
        for n0 in range(0, main_cols, P):
            proj_a[:, n0:n0 + P] = _dot(h, win_ref[:, n0:n0 + P])
            yield D_MODEL * P / macs_per_row
        if not even:
            proj_a[:, ODD_MAIN:] = _log_gate(h, wg1_ref, wg2_ref, bg_ref)
            yield 0.0

        m = mix_c[...]
        for n0 in range(0, D_MODEL, P):
            x1_scr[:, n0:n0 + P] = xlag_ref[:, n0:n0 + P] + _dot(m, wo_ref[:, n0:n0 + P])
            yield D_MODEL * P / macs_per_row
        h = _rmsnorm(x1_scr[...], gmlp_ref[...]).astype(BF16)
        for n0 in range(0, D_FF, P):
            u = jnp.square(jnp.maximum(_dot(h, wu_ref[:, n0:n0 + P]), 0.0))
            u_scr[:, n0:n0 + P] = u.astype(BF16)
            yield D_MODEL * P / macs_per_row
        for n0 in range(0, D_MODEL, P):
            x2 = x1_scr[:, n0:n0 + P] + _dot(u_scr[...], wd_ref[:, n0:n0 + P])
            if final:
                x1_scr[:, n0:n0 + P] = x2
            else:
                out_ref[:, n0:n0 + P] = x2
            yield D_FF * P / macs_per_row
        if final:
            out_ref[...] = _rmsnorm(x1_scr[...], gf_ref[...])

    def step(par):
        proj_a, proj_b = proj_scr[par], proj_scr[1 - par]
        mix_b, mix_c = mix_scr[1 - par], mix_scr[par]

        def load(r, c):
            return proj_b[r, c]

        def store(r, c, val):
            mix_b[r, c] = val.astype(BF16)

        if even:
            vector_stream = _even_mixer_tile(chunk, rows // chunk, tile_in_seq * rows, load,
                                             store, cos_ref, sin_ref, s_scr, s_scr, h_scr, h_scr,
                                             tail_scr, mixer_weights)
        else:
            vector_stream = _gla_mixer_tile(chunk, rows // chunk, load, store, s_scr, s_scr,
                                            nw_ref[...])
        _interleave(matrix_stream(proj_a, mix_c), vector_stream)

    parity = lax.rem(s, 2)
    pl.when(parity == 0)(functools.partial(step, 0))
    pl.when(parity == 1)(functools.partial(step, 1))

    @pl.when((s >= 1) & (s <= n_tiles) & (lax.rem(s, tiles_per_seq) == 0))
    def _():
        snew_ref[0] = s_scr[...]
        if even:
            hnew_ref[0] = h_scr[...]
            cnew_ref[0] = tail_scr[SUBLANES - (CONV_W - 1):SUBLANES, :]


def _fused_layer(x, w, layer, cos, sin, prev_state, seq_len):
    n_rows = x.shape[0]
    B = n_rows // seq_len
    R = FUSED_ROWS
    even = layer % 2 == 0
    final = layer == DEPTH - 1
    i = layer // 2
    C = math.gcd(seq_len, RET_CHUNK if even else GLA_CHUNK)
    tps = seq_len // R
    NT = n_rows // R
    has_prev = prev_state is not None

    def lead(s):
        return (jnp.minimum(s, NT - 1), 0)

    def mid_tile(s):
        return jnp.clip(s - 1, 0, NT - 1)

    def lag(s):
        return (jnp.clip(s - 2, 0, NT - 1), 0)

    def seq3(s):
        return (mid_tile(s) // tps, 0, 0)

    in_specs = [pl.BlockSpec((R, D_MODEL), lead),
                pl.BlockSpec((R, D_MODEL), lag),
                _layer_spec((1, D_MODEL), layer)]
    args = [x, x, w["norm_mix"]]
    if even:
        in_specs += [_layer_spec((D_MODEL, EVEN_IN), i),
                     pl.BlockSpec((R, RET_KEY), lambda s: (mid_tile(s) % tps, 0)),
                     pl.BlockSpec((R, RET_KEY), lambda s: (mid_tile(s) % tps, 0)),
                     _layer_spec((CONV_W, LRU_WIDTH), i),
                     _layer_spec((1, LRU_WIDTH), i),
                     _layer_spec((LRU_WIDTH, LRU_WIDTH), i),
                     _layer_spec((1, LRU_WIDTH), i),
                     _layer_spec((LRU_WIDTH, LRU_WIDTH), i),
                     _layer_spec((1, LRU_WIDTH), i),
                     _layer_spec((1, LRU_WIDTH), i)]
        args += [w["w_in_even"], cos, sin, w["conv_w"], w["conv_b"], w["wa"], w["ba"],
                 w["wi"], w["bi"], w["lam"]]
        w_out = w["w_out_even"]
        state_shape = (RET_HEADS, RET_DK, RET_DV)
        n_stack = w["w_in_even"].shape[0]
        proj_cols = EVEN_IN
    else:
        in_specs += [_layer_spec((D_MODEL, ODD_MAIN), i),
                     _layer_spec((D_MODEL, LANES), i),
                     _layer_spec((LANES, GLA_KEY), i),
                     _layer_spec((1, GLA_KEY), i),
                     _layer_spec((1, GLA_DV), i)]
        args += [w["w_in_odd"], w["wg1"], w["wg2"], w["bg"], w["gla_norm"]]
        w_out = w["w_out_odd"]
        state_shape = (GLA_HEADS, GLA_DK, GLA_DV)
        n_stack = w["w_in_odd"].shape[0]
        proj_cols = ODD_OUT
    in_specs += [_layer_spec((D_MODEL, D_MODEL), i),
                 _layer_spec((1, D_MODEL), layer),
                 _layer_spec((D_MODEL, D_FF), layer),
                 _layer_spec((D_FF, D_MODEL), layer),
                 _const_spec((1, D_MODEL))]
    args += [w_out, w["norm_mlp"], w["w_up"], w["w_down"], w["norm_final"]]
    aliases = {}
    if has_prev:
        in_specs.append(pl.BlockSpec(memory_space=pl.ANY))
        args.append(prev_state)
        aliases = {len(args) - 1: 1}

    out_specs = [pl.BlockSpec((R, D_MODEL), lag),
                 pl.BlockSpec((None, 1) + state_shape,
                              lambda s: (i, mid_tile(s) // tps, 0, 0, 0))]
    out_shapes = [jax.ShapeDtypeStruct((n_rows, D_MODEL), F32),
                  jax.ShapeDtypeStruct((n_stack, B) + state_shape, F32)]
    scratch = [pltpu.VMEM((R, proj_cols), F32), pltpu.VMEM((R, proj_cols), F32),
               pltpu.VMEM((R, D_MODEL), BF16), pltpu.VMEM((R, D_MODEL), BF16),
               pltpu.VMEM((R, D_MODEL), F32), pltpu.VMEM((R, D_FF), BF16),
               pltpu.VMEM(state_shape, F32)]
    if even:
        out_specs += [pl.BlockSpec((1, 1, LRU_WIDTH), seq3),
                      pl.BlockSpec((1, CONV_W - 1, LRU_WIDTH), seq3)]
        out_shapes += [jax.ShapeDtypeStruct((B, 1, LRU_WIDTH), F32),
                       jax.ShapeDtypeStruct((B, CONV_W - 1, LRU_WIDTH), F32)]
        scratch += [pltpu.VMEM((1, LRU_WIDTH), F32),
                    pltpu.VMEM((SUBLANES, LRU_WIDTH), F32)]
    return pl.pallas_call(
        functools.partial(_fused_layer_kernel, even, final, C, R, tps, NT, has_prev),
        grid=(NT + 2,),
        in_specs=in_specs,
        out_specs=tuple(out_specs),
        out_shape=tuple(out_shapes),
        scratch_shapes=scratch,
        input_output_aliases=aliases,
        compiler_params=_params("arbitrary"),
        name="layer_even" if even else "layer_odd",
    )(*args)


def _rope_tables(pos0, T):
    half = RET_DK // 2
    inv = ROPE_BASE ** (-jnp.arange(half, dtype=F32) / half)
    ang = (pos0 + jnp.arange(T, dtype=jnp.int32)).astype(F32)[:, None] * inv[None, :]
    cos = jnp.cos(ang)
    sin = jnp.sin(ang)
    cos_full = jnp.tile(jnp.concatenate([cos, cos], axis=-1), (1, RET_HEADS))
    sin_signed = jnp.tile(jnp.concatenate([-sin, sin], axis=-1), (1, RET_HEADS))
    return cos_full, sin_signed


def _block_diag(w):
    nl, nb, c, d = w.shape
    eye = jnp.eye(nb, dtype=w.dtype)
    return (eye[None, :, None, :, None] * w[:, :, :, None, :]).reshape(nl, nb * c, nb * d)


def _prepare_weights(norm_mix, norm_mlp, norm_final, w_in_even, w_out_even, conv_w, conv_b,
                     lru_w_a, lru_b_a, lru_w_i, lru_b_i, lru_lambda,
                     w_in_odd, gla_w_gate2, gla_b_gate, gla_norm, w_out_odd, w_up, w_down):
    n_even = w_in_even.shape[0]
    n_odd = w_in_odd.shape[0]
    pad1 = LANES - GLA_RANK
    return dict(
        norm_mix=norm_mix.reshape(DEPTH, 1, D_MODEL),
        norm_mlp=norm_mlp.reshape(DEPTH, 1, D_MODEL),
        norm_final=norm_final.reshape(1, D_MODEL),
        w_in_even=w_in_even.astype(BF16),
        w_out_even=w_out_even.astype(BF16),
        conv_w=conv_w,
        conv_b=conv_b.reshape(n_even, 1, LRU_WIDTH),
        wa=_block_diag(lru_w_a).astype(BF16),
        ba=lru_b_a.reshape(n_even, 1, LRU_WIDTH),
        wi=_block_diag(lru_w_i).astype(BF16),
        bi=lru_b_i.reshape(n_even, 1, LRU_WIDTH),
        lam=lru_lambda.reshape(n_even, 1, LRU_WIDTH),
        w_in_odd=w_in_odd.astype(BF16),
        wg1=jnp.pad(w_in_odd[:, :, ODD_MAIN:], ((0, 0), (0, 0), (0, pad1))).astype(BF16),
        wg2=jnp.pad(gla_w_gate2, ((0, 0), (0, pad1), (0, 0))).astype(BF16),
        bg=gla_b_gate.reshape(n_odd, 1, GLA_KEY),
        gla_norm=gla_norm.reshape(n_odd, 1, GLA_DV),
        w_out_odd=w_out_odd.astype(BF16),
        w_up=w_up.astype(BF16),
        w_down=w_down.astype(BF16),
    )


def _prompt_trunk(x, w):
    B, T, _ = x.shape
    xf = x.reshape(B * T, D_MODEL)
    cos, sin = _rope_tables(0, T)
    ret_new, gla_new = None, None
    lrus, convs = [], []
    for layer in range(DEPTH):
        if layer % 2 == 0:
            xf, ret_new, lh, cb = _fused_layer(xf, w, layer, cos, sin, ret_new, T)
            lrus.append(lh.reshape(B, LRU_WIDTH))
            convs.append(cb)
        else:
            xf, gla_new = _fused_layer(xf, w, layer, cos, sin, gla_new, T)
    return xf.reshape(B, T, D_MODEL), ret_new, jnp.stack(lrus), jnp.stack(convs), gla_new


def _sample_trunk(x, pos0, ret_s, lru_h, conv_buf, gla_s, w):
    B, T, _ = x.shape
    xf = x.reshape(B * T, D_MODEL)
    cos, sin = _rope_tables(pos0, T)
    lru_h = lru_h.reshape(lru_h.shape[0], B, 1, LRU_WIDTH)
    ret_new, gla_new = None, None
    lrus, convs = [], []
    for layer in range(DEPTH):
        if layer % 2 == 0:
            proj = _norm_proj_even(xf, w, layer)
            mix, ret_new, lh, cb = _even_mixer(proj.reshape(B, T, EVEN_IN), pos0, cos, sin,
                                               ret_s, lru_h, conv_buf, w, layer, ret_new)
            lrus.append(lh.reshape(B, LRU_WIDTH))
            convs.append(cb)
        else:
            proj = _norm_proj_odd(xf, w, layer)
            mix, gla_new = _gla_mixer(proj.reshape(B, T, ODD_OUT), gla_s, w, layer, gla_new)
        xf = _out_mlp(xf, mix.reshape(B * T, D_MODEL), w, layer)
    return xf.reshape(B, T, D_MODEL), ret_new, jnp.stack(lrus), jnp.stack(convs), gla_new


def kernel(x_prompt, x_sample, state_ret, state_lru, state_conv, state_gla, norm_mix, norm_mlp, norm_final, w_in_even, w_out_even, conv_w, conv_b, lru_w_a, lru_b_a, lru_w_i, lru_b_i, lru_lambda, w_in_odd, gla_w_gate2, gla_b_gate, gla_norm, w_out_odd, w_up, w_down):
    w = _prepare_weights(norm_mix, norm_mlp, norm_final, w_in_even, w_out_even, conv_w, conv_b,
                         lru_w_a, lru_b_a, lru_w_i, lru_b_i, lru_lambda,
                         w_in_odd, gla_w_gate2, gla_b_gate, gla_norm, w_out_odd, w_up, w_down)
    y_p, ret_p, lru_p, conv_p, gla_p = _prompt_trunk(x_prompt, w)
    y_s, ret_s, lru_s, conv_s, gla_s = _sample_trunk(x_sample, PAST_LEN, state_ret, state_lru,
                                                     state_conv, state_gla, w)
    return (y_p, y_s, ret_p, ret_s, lru_p, lru_s, conv_p, conv_s, gla_p, gla_s)
```

```python
import functools
import math

import jax
import jax.numpy as jnp
from jax import lax
from jax.experimental import pallas as pl
from jax.experimental.pallas import tpu as pltpu

F32 = jnp.float32
BF16 = jnp.bfloat16

D_MODEL = 1024
DEPTH = 4
PAST_LEN = 16384
EPS = 1e-6
ROPE_BASE = 10000.0

RET_HEADS = 4
RET_DK = 64
RET_DV = 128
RET_KEY = RET_HEADS * RET_DK
RET_VALUE = RET_HEADS * RET_DV
RET_CHUNK = 128

LRU_WIDTH = 512
CONV_W = 4
LRU_C = 8.0

GLA_HEADS = 4
GLA_KEY = 512
GLA_VALUE = 1024
GLA_DK = GLA_KEY // GLA_HEADS
GLA_DV = GLA_VALUE // GLA_HEADS
GLA_RANK = 16
GLA_TAU = 16.0
GLA_CHUNK = 64

D_FF = 4 * D_MODEL
FF_CHUNK = 1024

EVEN_IN = 2 * RET_KEY + 2 * RET_VALUE + 2 * LRU_WIDTH
ODD_MAIN = 2 * GLA_KEY + 2 * GLA_VALUE
ODD_OUT = ODD_MAIN + GLA_KEY

LANES = 128
SUBLANES = 8
ROW_TILE = 512
FUSED_ROWS = 256
MXU_PIECE = 256
SAMPLE_SEQS = 8
VMEM_LIMIT = 56 * 1024 * 1024


def _dot(a, b):
    return jnp.dot(a.astype(BF16), b.astype(BF16), preferred_element_type=F32)


def _dot_nt(a, b):
    return lax.dot_general(a.astype(BF16), b.astype(BF16), (((1,), (1,)), ((), ())),
                           preferred_element_type=F32)


def _dot_tn(a, b):
    return lax.dot_general(a.astype(BF16), b.astype(BF16), (((0,), (0,)), ((), ())),
                           preferred_element_type=F32)


def _rmsnorm(x, g):
    return x * lax.rsqrt(jnp.mean(x * x, axis=-1, keepdims=True) + EPS) * g


def _head_rms(x, eps=EPS):
    return x * lax.rsqrt(jnp.mean(x * x, axis=-1, keepdims=True) + eps)


def _shift_rows(x, d, fill):
    row = lax.broadcasted_iota(jnp.int32, x.shape, 0)
    return jnp.where(row >= d, pltpu.roll(x, d, axis=0), fill)


def _layer_spec(shape, layer):
    n = len(shape)
    return pl.BlockSpec((None,) + tuple(shape), lambda *_: (layer,) + (0,) * n,
                        pipeline_mode=pl.Buffered(1))


def _const_spec(shape):
    n = len(shape)
    return pl.BlockSpec(shape, lambda *_: (0,) * n, pipeline_mode=pl.Buffered(1))


def _params(*semantics):
    return pltpu.CompilerParams(dimension_semantics=semantics, vmem_limit_bytes=VMEM_LIMIT)


def _scaled_log_sigmoid(z):
    return (jnp.minimum(z, 0.0) - jnp.log(1.0 + jnp.exp(-jnp.abs(z)))) / GLA_TAU


def _log_gate(h, wg1_ref, wg2_ref, bg_ref):
    glr = _dot(h, wg1_ref[...])
    return _scaled_log_sigmoid(_dot(glr, wg2_ref[...]) + bg_ref[...])


def _out_mlp_math(final, x, m, wo_ref, g_ref, wu_ref, wd_ref, gf_ref):
    x1 = x + _dot(m, wo_ref[...])
    h = _rmsnorm(x1, g_ref[...]).astype(BF16)
    acc = None
    for c in range(D_FF // FF_CHUNK):
        cols = slice(c * FF_CHUNK, (c + 1) * FF_CHUNK)
        u = jnp.square(jnp.maximum(_dot(h, wu_ref[:, cols]), 0.0))
        part = _dot(u, wd_ref[cols, :])
        acc = part if acc is None else acc + part
    x2 = x1 + acc
    if final:
        x2 = _rmsnorm(x2, gf_ref[...])
    return x2


def _norm_proj_even_kernel(x_ref, g_ref, w_ref, o_ref):
    h = _rmsnorm(x_ref[...], g_ref[...])
    o_ref[...] = _dot(h, w_ref[...])


def _norm_proj_odd_kernel(x_ref, g_ref, w_ref, wg1_ref, wg2_ref, bg_ref, o_ref):
    h = _rmsnorm(x_ref[...], g_ref[...]).astype(BF16)
    o_ref[:, :ODD_MAIN] = _dot(h, w_ref[...])
    o_ref[:, ODD_MAIN:] = _log_gate(h, wg1_ref, wg2_ref, bg_ref)


def _out_mlp_kernel(final, x_ref, m_ref, wo_ref, g_ref, wu_ref, wd_ref, gf_ref, o_ref):
    o_ref[...] = _out_mlp_math(final, x_ref[...], m_ref[...], wo_ref, g_ref, wu_ref, wd_ref,
                               gf_ref)


def _norm_proj_even(x, w, layer):
    n = x.shape[0]
    tm = min(ROW_TILE, n)
    return pl.pallas_call(
        _norm_proj_even_kernel,
        grid=(n // tm,),
        in_specs=[pl.BlockSpec((tm, D_MODEL), lambda i: (i, 0)),
                  _layer_spec((1, D_MODEL), layer),
                  _layer_spec((D_MODEL, EVEN_IN), layer // 2)],
        out_specs=pl.BlockSpec((tm, EVEN_IN), lambda i: (i, 0)),
        out_shape=jax.ShapeDtypeStruct((n, EVEN_IN), F32),
        compiler_params=_params("parallel"),
        name="norm_proj_even",
    )(x, w["norm_mix"], w["w_in_even"])


def _norm_proj_odd(x, w, layer):
    n = x.shape[0]
    tm = min(ROW_TILE, n)
    i = layer // 2
    return pl.pallas_call(
        _norm_proj_odd_kernel,
        grid=(n // tm,),
        in_specs=[pl.BlockSpec((tm, D_MODEL), lambda i: (i, 0)),
                  _layer_spec((1, D_MODEL), layer),
                  _layer_spec((D_MODEL, ODD_MAIN), i),
                  _layer_spec((D_MODEL, LANES), i),
                  _layer_spec((LANES, GLA_KEY), i),
                  _layer_spec((1, GLA_KEY), i)],
        out_specs=pl.BlockSpec((tm, ODD_OUT), lambda i: (i, 0)),
        out_shape=jax.ShapeDtypeStruct((n, ODD_OUT), F32),
        compiler_params=_params("parallel"),
        name="norm_proj_odd",
    )(x, w["norm_mix"], w["w_in_odd"], w["wg1"], w["wg2"], w["bg"])


def _out_mlp(x, mix, w, layer):
    n = x.shape[0]
    tm = min(ROW_TILE, n)
    final = layer == DEPTH - 1
    w_out = w["w_out_even"] if layer % 2 == 0 else w["w_out_odd"]
    return pl.pallas_call(
        functools.partial(_out_mlp_kernel, final),
        grid=(n // tm,),
        in_specs=[pl.BlockSpec((tm, D_MODEL), lambda i: (i, 0)),
                  pl.BlockSpec((tm, D_MODEL), lambda i: (i, 0)),
                  _layer_spec((D_MODEL, D_MODEL), layer // 2),
                  _layer_spec((1, D_MODEL), layer),
                  _layer_spec((D_MODEL, D_FF), layer),
                  _layer_spec((D_FF, D_MODEL), layer),
                  _const_spec((1, D_MODEL))],
        out_specs=pl.BlockSpec((tm, D_MODEL), lambda i: (i, 0)),
        out_shape=jax.ShapeDtypeStruct((n, D_MODEL), F32),
        compiler_params=_params("parallel"),
        name="out_mlp_final" if final else "out_mlp",
    )(x, mix, w_out, w["norm_mlp"], w["w_up"], w["w_down"], w["norm_final"])


def _run(stream):
    for _ in stream:
        pass


def _interleave(*streams):
    done = [0.0] * len(streams)
    alive = list(range(len(streams)))
    while alive:
        i = min(alive, key=lambda j: done[j])
        try:
            done[i] += next(streams[i])
        except StopIteration:
            alive.remove(i)


class _Pace:
    def __init__(self):
        self.zero = None

    def after(self, value):
        t = value[0:1, 0:1]
        self.zero = jnp.where(t == t, 0.0, t)

    def __call__(self, x):
        return x if self.zero is None else x + self.zero


def _no_pace(x):
    return x


def _even_mixer_tile(C, chunks, pos_base, load, store, cos_ref, sin_ref,
                     s_src, s_dst, h_src, h_dst, tail_ref, weights, pace=_no_pace):
    cw_ref, cbias_ref, wa_ref, ba_ref, wi_ref, bi_ref, lam_ref = weights
    lane = lax.broadcasted_iota(jnp.int32, (C, RET_KEY), 1)
    first_half = (lane % RET_DK) < (RET_DK // 2)
    ri = lax.broadcasted_iota(jnp.int32, (C, C), 0)
    ci = lax.broadcasted_iota(jnp.int32, (C, C), 1)
    rel = (ri - ci).astype(F32)
    row = lax.broadcasted_iota(jnp.int32, (C, 1), 0).astype(F32)
    trow = lax.broadcasted_iota(jnp.int32, (C, LRU_WIDTH), 0)
    row8 = lax.broadcasted_iota(jnp.int32, (SUBLANES, LRU_WIDTH), 0)
    lam = lam_ref[...]
    softplus_neg_lam = jnp.maximum(-lam, 0.0) + jnp.log(1.0 + jnp.exp(-jnp.abs(lam)))
    log_decay = [math.log1p(-(2.0 ** (-5.0 - h))) for h in range(RET_HEADS)]
    decay_mask = [jnp.where(rel >= 0, jnp.exp(jnp.maximum(rel, 0.0) * lg), 0.0)
                  for lg in log_decay]
    q_decay = [jnp.exp((row + 1.0) * lg) for lg in log_decay]
    k_decay = [jnp.exp((C - 1.0 - row) * lg) for lg in log_decay]
    n_scan = max(1, int(math.log2(C)))

    v0 = 2 * RET_KEY
    g0 = v0 + RET_VALUE
    x0 = g0 + RET_VALUE
    xg0 = x0 + LRU_WIDTH
    heads = range(RET_HEADS)

    J = range(chunks)
    rows = [slice(j * C, (j + 1) * C) for j in J]

    def delayed(x, t, n):
        rolled = pltpu.roll(x, n, axis=0)
        head = jnp.where(row8 < n, pltpu.roll(t, n, axis=0), rolled[0:SUBLANES, :])
        if C == SUBLANES:
            return head
        return jnp.concatenate([head, rolled[SUBLANES:, :]], axis=0)

    qh, kh, q_dec, k_dec, vh, xc = ([None] * chunks for _ in range(6))
    tail = tail_ref[...]
    for j in J:
        cos = cos_ref[rows[j], :]
        sin = sin_ref[rows[j], :]

        def rope(x):
            partner = jnp.where(first_half,
                                pltpu.roll(x, RET_KEY - RET_DK // 2, axis=1),
                                pltpu.roll(x, RET_DK // 2, axis=1))
            return x * cos + partner * sin

        q = rope(load(rows[j], slice(0, RET_KEY)))
        k = rope(load(rows[j], slice(RET_KEY, 2 * RET_KEY))) * (RET_DK ** -0.5)
        q = [q[:, h * RET_DK:(h + 1) * RET_DK] for h in heads]
        k = [k[:, h * RET_DK:(h + 1) * RET_DK] for h in heads]
        q_dec[j] = [(q[h] * q_decay[h]).astype(BF16) for h in heads]
        k_dec[j] = [(k[h] * k_decay[h]).astype(BF16) for h in heads]
        qh[j] = [q[h].astype(BF16) for h in heads]
        kh[j] = [k[h].astype(BF16) for h in heads]
        vh[j] = [load(rows[j], slice(v0 + h * RET_DV, v0 + (h + 1) * RET_DV)).astype(BF16)
                 for h in heads]
        yield 0.08 / chunks

        x_new = load(rows[j], slice(x0, x0 + LRU_WIDTH))
        acc = pace(cbias_ref[...])
        for i in range(CONV_W - 1):
            acc = acc + delayed(x_new, tail, CONV_W - 1 - i) * cw_ref[i:i + 1, :]
        xc[j] = acc + x_new * cw_ref[CONV_W - 1:CONV_W, :]
        tail = x_new[C - SUBLANES:C, :]
        yield 0.10 / chunks
    tail_ref[...] = tail

    scores = [[_dot_nt(qh[j][h], kh[j][h]) for h in heads] for j in J]
    kv = [[_dot_tn(k_dec[j][h], vh[j][h]) for h in heads] for j in J]
    r_pre = [_dot(xc[j], wa_ref[...]) for j in J]
    i_pre = [_dot(xc[j], wi_ref[...]) for j in J]
    yield 0.0

    probs, s_in = [None] * chunks, [None] * chunks
    s_cur = [s_src[h] for h in heads]
    for j in J:
        probs[j] = [(scores[j][h] * decay_mask[h]).astype(BF16) for h in heads]
        s_in[j] = [s_cur[h].astype(BF16) for h in heads]
        s_cur = [s_cur[h] * math.exp(C * log_decay[h]) + kv[j][h] for h in heads]
        yield 0.06 / chunks
    for h in heads:
        s_dst[h] = s_cur[h]

    a_cum, b_loc = [None] * chunks, [None] * chunks
    for j in J:
        r = jax.nn.sigmoid(r_pre[j] + pace(ba_ref[...]))
        gate_i = jax.nn.sigmoid(i_pre[j] + bi_ref[...])
        log_a = -LRU_C * r * softplus_neg_lam
        a = jnp.exp(log_a)
        y = -jnp.tanh(log_a) * (a * a + 1.0)
        mult = jnp.where(y > 0.0, y * lax.rsqrt(y), 0.0)
        pos = trow + (pos_base + j * C)
        mult = jnp.where(pos == 0, 1.0, mult)
        b = mult * (gate_i * xc[j])
        yield 0.20 / chunks
        d = 1
        while d < C:
            b = a * _shift_rows(b, d, pace(0.0)) + b
            a = a * _shift_rows(a, d, pace(1.0))
            d *= 2
            yield 0.36 / (chunks * n_scan)
        a_cum[j], b_loc[j] = a, b

    out = [[_dot(probs[j][h], vh[j][h]) + _dot(q_dec[j][h], s_in[j][h]) for h in heads]
           for j in J]
    yield 0.0

    carry = h_src[...]
    for j in J:
        hidden = b_loc[j] + a_cum[j] * carry
        carry = hidden[C - 1:C, :]
        xg = load(rows[j], slice(xg0, xg0 + LRU_WIDTH))
        store(rows[j], slice(RET_VALUE, RET_VALUE + LRU_WIDTH), hidden * jax.nn.gelu(xg))
        yield 0.08 / chunks
        for h in heads:
            gh = load(rows[j], slice(g0 + h * RET_DV, g0 + (h + 1) * RET_DV))
            store(rows[j], slice(h * RET_DV, (h + 1) * RET_DV),
                  jax.nn.silu(gh) * _head_rms(out[j][h], pace(EPS)))
            yield 0.03 / chunks
    h_dst[...] = carry


def _gla_mixer_tile(C, chunks, load, store, s_src, s_dst, nw, pace=_no_pace):
    k0 = GLA_KEY
    v0 = 2 * GLA_KEY
    r0 = v0 + GLA_VALUE
    la0 = r0 + GLA_VALUE
    ri = lax.broadcasted_iota(jnp.int32, (C, C), 0)
    ci = lax.broadcasted_iota(jnp.int32, (C, C), 1)
    causal = ri >= ci
    di = lax.broadcasted_iota(jnp.int32, (GLA_DK, GLA_DK), 0)
    dj = lax.broadcasted_iota(jnp.int32, (GLA_DK, GLA_DK), 1)
    eye = di == dj
    heads = range(GLA_HEADS)
    key = [slice(h * GLA_DK, (h + 1) * GLA_DK) for h in heads]

    J = range(chunks)
    rows = [slice(j * C, (j + 1) * C) for j in J]

    q_in, k_in, q_st, k_st, decay_col, vh = ([None] * chunks for _ in range(6))
    for j in J:
        b = load(rows[j], slice(la0, la0 + GLA_KEY))
        d = 1
        while d < C:
            b = b + _shift_rows(b, d, pace(0.0))
            d *= 2
        yield 0.12 / chunks
        b_mid = pace(b[C // 2:C // 2 + 1, :])
        b_last = b[C - 1:C, :]
        q = load(rows[j], slice(0, GLA_KEY)) * (GLA_DK ** -0.5)
        k = load(rows[j], slice(k0, k0 + GLA_KEY))
        q_in[j] = (q * jnp.exp(b - b_mid)).astype(BF16)
        k_in[j] = (k * jnp.exp(b_mid - b)).astype(BF16)
        q_st[j] = (q * jnp.exp(b)).astype(BF16)
        k_st[j] = (k * jnp.exp(b_last - b)).astype(BF16)
        s_decay = jnp.exp(b_last)
        decay_col[j] = [jnp.sum(jnp.where(eye, s_decay[:, key[h]], 0.0), axis=1, keepdims=True)
                        for h in heads]
        vh[j] = [load(rows[j], slice(v0 + h * GLA_DV, v0 + (h + 1) * GLA_DV)).astype(BF16)
                 for h in heads]
        yield 0.28 / chunks

    scores = [[_dot_nt(q_in[j][:, key[h]], k_in[j][:, key[h]]) for h in heads] for j in J]
    kv = [[_dot_tn(k_st[j][:, key[h]], vh[j][h]) for h in heads] for j in J]
    yield 0.0

    probs, s_in = [None] * chunks, [None] * chunks
    s_cur = [s_src[h] for h in heads]
    for j in J:
        probs[j] = [jnp.where(causal, scores[j][h], 0.0).astype(BF16) for h in heads]
        s_in[j] = [s_cur[h].astype(BF16) for h in heads]
        s_cur = [s_cur[h] * decay_col[j][h] + kv[j][h] for h in heads]
        yield 0.16 / chunks
    for h in heads:
        s_dst[h] = s_cur[h]

    out = [[_dot(probs[j][h], vh[j][h]) + _dot(q_st[j][:, key[h]], s_in[j][h]) for h in heads]
           for j in J]
    yield 0.0

    for j in J:
        for h in heads:
            rh = load(rows[j], slice(r0 + h * GLA_DV, r0 + (h + 1) * GLA_DV))
            store(rows[j], slice(h * GLA_DV, (h + 1) * GLA_DV),
                  jax.nn.silu(rh) * (_head_rms(out[j][h], pace(EPS)) * nw))
            yield 0.11 / chunks


def _even_mixer_kernel(pos0, chunk, seqs, has_prev, *refs):
    (proj_ref, cos_ref, sin_ref, s0_ref, h0_ref, cb_ref,
     cw_ref, cbias_ref, wa_ref, ba_ref, wi_ref, bi_ref, lam_ref) = refs[:13]
    refs = refs[13 + (1 if has_prev else 0):]
    mix_ref, snew_ref, hnew_ref, cnew_ref, tail_scr = refs
    weights = (cw_ref, cbias_ref, wa_ref, ba_ref, wi_ref, bi_ref, lam_ref)
    for g in range(seqs):
        tail = tail_scr.at[g]
        tail[...] = jnp.zeros((SUBLANES, LRU_WIDTH), F32)
        tail[SUBLANES - (CONV_W - 1):SUBLANES, :] = cb_ref[g]

        def load(rows, cols, g=g):
            return proj_ref[g, rows, cols]

        def store(rows, cols, val, g=g):
            mix_ref[g, rows, cols] = val

        _run(_even_mixer_tile(chunk, 1, pos0, load, store, cos_ref, sin_ref,
                              s0_ref.at[g], snew_ref.at[g], h0_ref.at[g], hnew_ref.at[g],
                              tail, weights))
        cnew_ref[g] = tail[SUBLANES - (CONV_W - 1):SUBLANES, :]


def _even_mixer(proj, pos0, cos, sin, ret_s, lru_h, conv_buf, w, layer, prev_ret):
    B, T, _ = proj.shape
    i = layer // 2
    n_even = w["w_in_even"].shape[0]
    G = SAMPLE_SEQS
    has_prev = prev_ret is not None
    ret_block = (None, G, RET_HEADS, RET_DK, RET_DV)
    in_specs = [pl.BlockSpec((G, T, EVEN_IN), lambda b: (b, 0, 0)),
                _const_spec((T, RET_KEY)),
                _const_spec((T, RET_KEY)),
                pl.BlockSpec(ret_block, lambda b: (i, b, 0, 0, 0)),
                pl.BlockSpec((None, G, 1, LRU_WIDTH), lambda b: (i, b, 0, 0)),
                pl.BlockSpec((None, G, CONV_W - 1, LRU_WIDTH), lambda b: (i, b, 0, 0)),
                _layer_spec((CONV_W, LRU_WIDTH), i),
                _layer_spec((1, LRU_WIDTH), i),
                _layer_spec((LRU_WIDTH, LRU_WIDTH), i),
                _layer_spec((1, LRU_WIDTH), i),
                _layer_spec((LRU_WIDTH, LRU_WIDTH), i),
                _layer_spec((1, LRU_WIDTH), i),
                _layer_spec((1, LRU_WIDTH), i)]
    args = [proj, cos, sin, ret_s, lru_h, conv_buf, w["conv_w"], w["conv_b"],
            w["wa"], w["ba"], w["wi"], w["bi"], w["lam"]]
    aliases = {}
    if has_prev:
        in_specs.append(pl.BlockSpec(memory_space=pl.ANY))
        args.append(prev_ret)
        aliases = {len(args) - 1: 1}
    out_shapes = (jax.ShapeDtypeStruct((B, T, D_MODEL), F32),
                  jax.ShapeDtypeStruct((n_even, B, RET_HEADS, RET_DK, RET_DV), F32),
                  jax.ShapeDtypeStruct((B, 1, LRU_WIDTH), F32),
                  jax.ShapeDtypeStruct((B, CONV_W - 1, LRU_WIDTH), F32))
    return pl.pallas_call(
        functools.partial(_even_mixer_kernel, pos0, T, G, has_prev),
        grid=(B // G,),
        in_specs=in_specs,
        out_specs=(pl.BlockSpec((G, T, D_MODEL), lambda b: (b, 0, 0)),
                   pl.BlockSpec(ret_block, lambda b: (i, b, 0, 0, 0)),
                   pl.BlockSpec((G, 1, LRU_WIDTH), lambda b: (b, 0, 0)),
                   pl.BlockSpec((G, CONV_W - 1, LRU_WIDTH), lambda b: (b, 0, 0))),
        out_shape=out_shapes,
        scratch_shapes=[pltpu.VMEM((G, SUBLANES, LRU_WIDTH), F32)],
        input_output_aliases=aliases,
        compiler_params=_params("parallel"),
        name="even_mixer",
    )(*args)


def _gla_mixer_kernel(chunk, seqs, has_prev, *refs):
    proj_ref, s0_ref, nw_ref = refs[:3]
    refs = refs[3 + (1 if has_prev else 0):]
    mix_ref, snew_ref = refs
    nw = nw_ref[...]
    for g in range(seqs):
        def load(rows, cols, g=g):
            return proj_ref[g, rows, cols]

        def store(rows, cols, val, g=g):
            mix_ref[g, rows, cols] = val

        _run(_gla_mixer_tile(chunk, 1, load, store, s0_ref.at[g], snew_ref.at[g], nw))


def _gla_mixer(proj, gla_s, w, layer, prev_gla):
    B, T, _ = proj.shape
    i = layer // 2
    n_odd = w["w_in_odd"].shape[0]
    G = SAMPLE_SEQS
    has_prev = prev_gla is not None
    state = (None, G, GLA_HEADS, GLA_DK, GLA_DV)
    in_specs = [pl.BlockSpec((G, T, ODD_OUT), lambda b: (b, 0, 0)),
                pl.BlockSpec(state, lambda b: (i, b, 0, 0, 0)),
                _layer_spec((1, GLA_DV), i)]
    args = [proj, gla_s, w["gla_norm"]]
    aliases = {}
    if has_prev:
        in_specs.append(pl.BlockSpec(memory_space=pl.ANY))
        args.append(prev_gla)
        aliases = {len(args) - 1: 1}
    return pl.pallas_call(
        functools.partial(_gla_mixer_kernel, T, G, has_prev),
        grid=(B // G,),
        in_specs=in_specs,
        out_specs=(pl.BlockSpec((G, T, D_MODEL), lambda b: (b, 0, 0)),
                   pl.BlockSpec(state, lambda b: (i, b, 0, 0, 0))),
        out_shape=(jax.ShapeDtypeStruct((B, T, D_MODEL), F32),
                   jax.ShapeDtypeStruct((n_odd, B, GLA_HEADS, GLA_DK, GLA_DV), F32)),
        input_output_aliases=aliases,
        compiler_params=_params("parallel"),
        name="gla_mixer",
    )(*args)


def _fused_layer_kernel(even, final, chunk, rows, tiles_per_seq, n_tiles, has_prev, *refs):
    it = iter(refs)
    xlead_ref, xlag_ref, gmix_ref, win_ref = (next(it) for _ in range(4))
    if even:
        cos_ref, sin_ref = next(it), next(it)
        mixer_weights = tuple(next(it) for _ in range(7))
    else:
        wg1_ref, wg2_ref, bg_ref, nw_ref = (next(it) for _ in range(4))
    wo_ref, gmlp_ref, wu_ref, wd_ref, gf_ref = (next(it) for _ in range(5))
    if has_prev:
        next(it)
    out_ref, snew_ref = next(it), next(it)
    if even:
        hnew_ref, cnew_ref = next(it), next(it)
    proj_scr = (next(it), next(it))
    mix_scr = (next(it), next(it))
    x1_scr, u_scr, s_scr = next(it), next(it), next(it)
    if even:
        h_scr, tail_scr = next(it), next(it)

    s = pl.program_id(0)
    tile_in_seq = lax.rem(jnp.clip(s - 1, 0, n_tiles - 1), tiles_per_seq)

    @pl.when(s == 0)
    def _():
        proj_scr[1][...] = jnp.zeros(proj_scr[1].shape, F32)
        mix_scr[0][...] = jnp.zeros(mix_scr[0].shape, BF16)

    @pl.when((s == 0) | (lax.rem(s + tiles_per_seq - 1, tiles_per_seq) == 0))
    def _():
        s_scr[...] = jnp.zeros(s_scr.shape, F32)
        if even:
            h_scr[...] = jnp.zeros(h_scr.shape, F32)
            tail_scr[...] = jnp.zeros(tail_scr.shape, F32)

    main_cols = EVEN_IN if even else ODD_MAIN
    macs_per_row = D_MODEL * (main_cols + D_MODEL + 2 * D_FF)

    def matrix_stream(proj_a, mix_c, pace):
        P = MXU_PIECE
        h_in = _rmsnorm(xlead_ref[...], gmix_ref[...]).astype(BF16)
        m = mix_c[...]
        for n0 in range(0, D_MODEL, P):
            x1 = xlag_ref[:, n0:n0 + P] + _dot(m, wo_ref[:, n0:n0 + P])
            x1_scr[:, n0:n0 + P] = x1
            pace.after(x1)
            yield D_MODEL * P / macs_per_row
        h_mlp = _rmsnorm(x1_scr[...], gmlp_ref[...]).astype(BF16)
        if not even:
            glr = _dot(h_in, wg1_ref[...]).astype(BF16)
        for n0 in range(0, main_cols, P):
            p = _dot(h_in, win_ref[:, n0:n0 + P])
            proj_a[:, n0:n0 + P] = p
            pace.after(p)
            if not even and n0 == P:
                z = _dot(glr, wg2_ref[...]) + bg_ref[...]
            yield D_MODEL * P / macs_per_row
        if not even:
            proj_a[:, ODD_MAIN:] = _scaled_log_sigmoid(z)
            yield 0.0
        for n0 in range(0, D_FF, P):
            u = jnp.square(jnp.maximum(_dot(h_mlp, wu_ref[:, n0:n0 + P]), 0.0))
            u_scr[:, n0:n0 + P] = u.astype(BF16)
            pace.after(u)
            yield D_MODEL * P / macs_per_row
        for n0 in range(0, D_MODEL, P):
            x2 = x1_scr[:, n0:n0 + P] + _dot(u_scr[...], wd_ref[:, n0:n0 + P])
            if final:
                x1_scr[:, n0:n0 + P] = x2
            else:
                out_ref[:, n0:n0 + P] = x2
            pace.after(x2)
            yield D_FF * P / macs_per_row
        if final:
            out_ref[...] = _rmsnorm(x1_scr[...], gf_ref[...])

    def step(par):
        proj_a, proj_b = proj_scr[par], proj_scr[1 - par]
        mix_b, mix_c = mix_scr[1 - par], mix_scr[par]

        def load(r, c):
            return proj_b[r, c]

        def store(r, c, val):
            mix_b[r, c] = val.astype(BF16)

        pace = _Pace()
        if even:
            vector_stream = _even_mixer_tile(chunk, rows // chunk, tile_in_seq * rows, load,
                                             store, cos_ref, sin_ref, s_scr, s_scr, h_scr, h_scr,
                                             tail_scr, mixer_weights, pace)
        else:
            vector_stream = _gla_mixer_tile(chunk, rows // chunk, load, store, s_scr, s_scr,
                                            nw_ref[...], pace)
        _interleave(matrix_stream(proj_a, mix_c, pace), vector_stream)

    parity = lax.rem(s, 2)
    pl.when(parity == 0)(functools.partial(step, 0))
    pl.when(parity == 1)(functools.partial(step, 1))

    @pl.when((s >= 1) & (s <= n_tiles) & (lax.rem(s, tiles_per_seq) == 0))
    def _():
        snew_ref[0] = s_scr[...]
        if even:
            hnew_ref[0] = h_scr[...]
            cnew_ref[0] = tail_scr[SUBLANES - (CONV_W - 1):SUBLANES, :]


def _fused_layer(x, w, layer, cos, sin, prev_state, seq_len):
    n_rows = x.shape[0]
    B = n_rows // seq_len
    R = FUSED_ROWS
    even = layer % 2 == 0
    final = layer == DEPTH - 1
    i = layer // 2
    C = math.gcd(seq_len, RET_CHUNK if even else GLA_CHUNK)
    tps = seq_len // R
    NT = n_rows // R
    has_prev = prev_state is not None

    def lead(s):
        return (jnp.minimum(s, NT - 1), 0)

    def mid_tile(s):
        return jnp.clip(s - 1, 0, NT - 1)

    def lag(s):
        return (jnp.clip(s - 2, 0, NT - 1), 0)

    def seq3(s):
        return (mid_tile(s) // tps, 0, 0)

    in_specs = [pl.BlockSpec((R, D_MODEL), lead),
                pl.BlockSpec((R, D_MODEL), lag),
                _layer_spec((1, D_MODEL), layer)]
    args = [x, x, w["norm_mix"]]
    if even:
        in_specs += [_layer_spec((D_MODEL, EVEN_IN), i),
                     pl.BlockSpec((R, RET_KEY), lambda s: (mid_tile(s) % tps, 0)),
                     pl.BlockSpec((R, RET_KEY), lambda s: (mid_tile(s) % tps, 0)),
                     _layer_spec((CONV_W, LRU_WIDTH), i),
                     _layer_spec((1, LRU_WIDTH), i),
                     _layer_spec((LRU_WIDTH, LRU_WIDTH), i),
                     _layer_spec((1, LRU_WIDTH), i),
                     _layer_spec((LRU_WIDTH, LRU_WIDTH), i),
                     _layer_spec((1, LRU_WIDTH), i),
                     _layer_spec((1, LRU_WIDTH), i)]
        args += [w["w_in_even"], cos, sin, w["conv_w"], w["conv_b"], w["wa"], w["ba"],
                 w["wi"], w["bi"], w["lam"]]
        w_out = w["w_out_even"]
        state_shape = (RET_HEADS, RET_DK, RET_DV)
        n_stack = w["w_in_even"].shape[0]
        proj_cols = EVEN_IN
    else:
        in_specs += [_layer_spec((D_MODEL, ODD_MAIN), i),
                     _layer_spec((D_MODEL, LANES), i),
                     _layer_spec((LANES, GLA_KEY), i),
                     _layer_spec((1, GLA_KEY), i),
                     _layer_spec((1, GLA_DV), i)]
        args += [w["w_in_odd"], w["wg1"], w["wg2"], w["bg"], w["gla_norm"]]
        w_out = w["w_out_odd"]
        state_shape = (GLA_HEADS, GLA_DK, GLA_DV)
        n_stack = w["w_in_odd"].shape[0]
        proj_cols = ODD_OUT
    in_specs += [_layer_spec((D_MODEL, D_MODEL), i),
                 _layer_spec((1, D_MODEL), layer),
                 _layer_spec((D_MODEL, D_FF), layer),
                 _layer_spec((D_FF, D_MODEL), layer),
                 _const_spec((1, D_MODEL))]
    args += [w_out, w["norm_mlp"], w["w_up"], w["w_down"], w["norm_final"]]
    aliases = {}
    if has_prev:
        in_specs.append(pl.BlockSpec(memory_space=pl.ANY))
        args.append(prev_state)
        aliases = {len(args) - 1: 1}

    out_specs = [pl.BlockSpec((R, D_MODEL), lag),
                 pl.BlockSpec((None, 1) + state_shape,
                              lambda s: (i, mid_tile(s) // tps, 0, 0, 0))]
    out_shapes = [jax.ShapeDtypeStruct((n_rows, D_MODEL), F32),
                  jax.ShapeDtypeStruct((n_stack, B) + state_shape, F32)]
    scratch = [pltpu.VMEM((R, proj_cols), F32), pltpu.VMEM((R, proj_cols), F32),
               pltpu.VMEM((R, D_MODEL), BF16), pltpu.VMEM((R, D_MODEL), BF16),
               pltpu.VMEM((R, D_MODEL), F32), pltpu.VMEM((R, D_FF), BF16),
               pltpu.VMEM(state_shape, F32)]
    if even:
        out_specs += [pl.BlockSpec((1, 1, LRU_WIDTH), seq3),
                      pl.BlockSpec((1, CONV_W - 1, LRU_WIDTH), seq3)]
        out_shapes += [jax.ShapeDtypeStruct((B, 1, LRU_WIDTH), F32),
                       jax.ShapeDtypeStruct((B, CONV_W - 1, LRU_WIDTH), F32)]
        scratch += [pltpu.VMEM((1, LRU_WIDTH), F32),
                    pltpu.VMEM((SUBLANES, LRU_WIDTH), F32)]
    return pl.pallas_call(
        functools.partial(_fused_layer_kernel, even, final, C, R, tps, NT, has_prev),
        grid=(NT + 2,),
        in_specs=in_specs,
        out_specs=tuple(out_specs),
        out_shape=tuple(out_shapes),
        scratch_shapes=scratch,
        input_output_aliases=aliases,
        compiler_params=_params("arbitrary"),
        name="layer_even" if even else "layer_odd",
    )(*args)


def _rope_tables(pos0, T):
    half = RET_DK // 2
    inv = ROPE_BASE ** (-jnp.arange(half, dtype=F32) / half)
    ang = (pos0 + jnp.arange(T, dtype=jnp.int32)).astype(F32)[:, None] * inv[None, :]
    cos = jnp.cos(ang)
    sin = jnp.sin(ang)
    cos_full = jnp.tile(jnp.concatenate([cos, cos], axis=-1), (1, RET_HEADS))
    sin_signed = jnp.tile(jnp.concatenate([-sin, sin], axis=-1), (1, RET_HEADS))
    return cos_full, sin_signed


def _block_diag(w):
    nl, nb, c, d = w.shape
    eye = jnp.eye(nb, dtype=w.dtype)
    return (eye[None, :, None, :, None] * w[:, :, :, None, :]).reshape(nl, nb * c, nb * d)


def _prepare_weights(norm_mix, norm_mlp, norm_final, w_in_even, w_out_even, conv_w, conv_b,
                     lru_w_a, lru_b_a, lru_w_i, lru_b_i, lru_lambda,
                     w_in_odd, gla_w_gate2, gla_b_gate, gla_norm, w_out_odd, w_up, w_down):
    n_even = w_in_even.shape[0]
    n_odd = w_in_odd.shape[0]
    pad1 = LANES - GLA_RANK
    return dict(
        norm_mix=norm_mix.reshape(DEPTH, 1, D_MODEL),
        norm_mlp=norm_mlp.reshape(DEPTH, 1, D_MODEL),
        norm_final=norm_final.reshape(1, D_MODEL),
        w_in_even=w_in_even.astype(BF16),
        w_out_even=w_out_even.astype(BF16),
        conv_w=conv_w,
        conv_b=conv_b.reshape(n_even, 1, LRU_WIDTH),
        wa=_block_diag(lru_w_a).astype(BF16),
        ba=lru_b_a.reshape(n_even, 1, LRU_WIDTH),
        wi=_block_diag(lru_w_i).astype(BF16),
        bi=lru_b_i.reshape(n_even, 1, LRU_WIDTH),
        lam=lru_lambda.reshape(n_even, 1, LRU_WIDTH),
        w_in_odd=w_in_odd.astype(BF16),
        wg1=jnp.pad(w_in_odd[:, :, ODD_MAIN:], ((0, 0), (0, 0), (0, pad1))).astype(BF16),
        wg2=jnp.pad(gla_w_gate2, ((0, 0), (0, pad1), (0, 0))).astype(BF16),
        bg=gla_b_gate.reshape(n_odd, 1, GLA_KEY),
        gla_norm=gla_norm.reshape(n_odd, 1, GLA_DV),
        w_out_odd=w_out_odd.astype(BF16),
        w_up=w_up.astype(BF16),
        w_down=w_down.astype(BF16),
    )


def _prompt_trunk(x, w):
    B, T, _ = x.shape
    xf = x.reshape(B * T, D_MODEL)
    cos, sin = _rope_tables(0, T)
    ret_new, gla_new = None, None
    lrus, convs = [], []
    for layer in range(DEPTH):
        if layer % 2 == 0:
            xf, ret_new, lh, cb = _fused_layer(xf, w, layer, cos, sin, ret_new, T)
            lrus.append(lh.reshape(B, LRU_WIDTH))
            convs.append(cb)
        else:
            xf, gla_new = _fused_layer(xf, w, layer, cos, sin, gla_new, T)
    return xf.reshape(B, T, D_MODEL), ret_new, jnp.stack(lrus), jnp.stack(convs), gla_new


def _sample_trunk(x, pos0, ret_s, lru_h, conv_buf, gla_s, w):
    B, T, _ = x.shape
    xf = x.reshape(B * T, D_MODEL)
    cos, sin = _rope_tables(pos0, T)
    lru_h = lru_h.reshape(lru_h.shape[0], B, 1, LRU_WIDTH)
    ret_new, gla_new = None, None
    lrus, convs = [], []
    for layer in range(DEPTH):
        if layer % 2 == 0:
            proj = _norm_proj_even(xf, w, layer)
            mix, ret_new, lh, cb = _even_mixer(proj.reshape(B, T, EVEN_IN), pos0, cos, sin,
                                               ret_s, lru_h, conv_buf, w, layer, ret_new)
            lrus.append(lh.reshape(B, LRU_WIDTH))
            convs.append(cb)
        else:
            proj = _norm_proj_odd(xf, w, layer)
            mix, gla_new = _gla_mixer(proj.reshape(B, T, ODD_OUT), gla_s, w, layer, gla_new)
        xf = _out_mlp(xf, mix.reshape(B * T, D_MODEL), w, layer)
    return xf.reshape(B, T, D_MODEL), ret_new, jnp.stack(lrus), jnp.stack(convs), gla_new


def kernel(x_prompt, x_sample, state_ret, state_lru, state_conv, state_gla, norm_mix, norm_mlp, norm_final, w_in_even, w_out_even, conv_w, conv_b, lru_w_a, lru_b_a, lru_w_i, lru_b_i, lru_lambda, w_in_odd, gla_w_gate2, gla_b_gate, gla_norm, w_out_odd, w_up, w_down):
    w = _prepare_weights(norm_mix, norm_mlp, norm_final, w_in_even, w_out_even, conv_w, conv_b,
                         lru_w_a, lru_b_a, lru_w_i, lru_b_i, lru_lambda,
                         w_in_odd, gla_w_gate2, gla_b_gate, gla_norm, w_out_odd, w_up, w_down)
    y_p, ret_p, lru_p, conv_p, gla_p = _prompt_trunk(x_prompt, w)
    y_s, ret_s, lru_s, conv_s, gla_s = _sample_trunk(x_sample, PAST_LEN, state_ret, state_lru,
                                                     state_conv, state_gla, w)
    return (y_p, y_s, ret_p, ret_s, lru_p, lru_s, conv_p, conv_s, gla_p, gla_s)
```

```python
import functools
import math

import jax
import jax.numpy as jnp
from jax import lax
from jax.experimental import pallas as pl
from jax.experimental.pallas import tpu as pltpu

F32 = jnp.float32
BF16 = jnp.bfloat16

D_MODEL = 1024
DEPTH = 4
PAST_LEN = 16384
EPS = 1e-6
ROPE_BASE = 10000.0

RET_HEADS = 4
RET_DK = 64
RET_DV = 128
RET_KEY = RET_HEADS * RET_DK
RET_VALUE = RET_HEADS * RET_DV
RET_CHUNK = 128

LRU_WIDTH = 512
CONV_W = 4
LRU_C = 8.0

GLA_HEADS = 4
GLA_KEY = 512
GLA_VALUE = 1024
GLA_DK = GLA_KEY // GLA_HEADS
GLA_DV = GLA_VALUE // GLA_HEADS
GLA_RANK = 16
GLA_TAU = 16.0
GLA_CHUNK = 64

D_FF = 4 * D_MODEL
FF_CHUNK = 1024

EVEN_IN = 2 * RET_KEY + 2 * RET_VALUE + 2 * LRU_WIDTH
ODD_MAIN = 2 * GLA_KEY + 2 * GLA_VALUE
ODD_OUT = ODD_MAIN + GLA_KEY

LANES = 128
SUBLANES = 8
ROW_TILE = 512
FUSED_ROWS = 256
MXU_PIECE = 256
SAMPLE_SEQS = 8
VMEM_LIMIT = 56 * 1024 * 1024


def _dot(a, b):
    return jnp.dot(a.astype(BF16), b.astype(BF16), preferred_element_type=F32)


def _dot_nt(a, b):
    return lax.dot_general(a.astype(BF16), b.astype(BF16), (((1,), (1,)), ((), ())),
                           preferred_element_type=F32)


def _dot_tn(a, b):
    return lax.dot_general(a.astype(BF16), b.astype(BF16), (((0,), (0,)), ((), ())),
                           preferred_element_type=F32)


def _rmsnorm(x, g):
    return x * lax.rsqrt(jnp.mean(x * x, axis=-1, keepdims=True) + EPS) * g


def _head_rms(x, eps=EPS):
    return x * lax.rsqrt(jnp.mean(x * x, axis=-1, keepdims=True) + eps)


def _shift_rows(x, d, fill):
    row = lax.broadcasted_iota(jnp.int32, x.shape, 0)
    return jnp.where(row >= d, pltpu.roll(x, d, axis=0), fill)


def _layer_spec(shape, layer):
    n = len(shape)
    return pl.BlockSpec((None,) + tuple(shape), lambda *_: (layer,) + (0,) * n,
                        pipeline_mode=pl.Buffered(1))


def _const_spec(shape):
    n = len(shape)
    return pl.BlockSpec(shape, lambda *_: (0,) * n, pipeline_mode=pl.Buffered(1))


def _params(*semantics):
    return pltpu.CompilerParams(dimension_semantics=semantics, vmem_limit_bytes=VMEM_LIMIT)


def _scaled_log_sigmoid(z):
    return (jnp.minimum(z, 0.0) - jnp.log(1.0 + jnp.exp(-jnp.abs(z)))) / GLA_TAU


def _log_gate(h, wg1_ref, wg2_ref, bg_ref):
    glr = _dot(h, wg1_ref[...])
    return _scaled_log_sigmoid(_dot(glr, wg2_ref[...]) + bg_ref[...])


def _out_mlp_math(final, x, m, wo_ref, g_ref, wu_ref, wd_ref, gf_ref):
    x1 = x + _dot(m, wo_ref[...])
    h = _rmsnorm(x1, g_ref[...]).astype(BF16)
    acc = None
    for c in range(D_FF // FF_CHUNK):
        cols = slice(c * FF_CHUNK, (c + 1) * FF_CHUNK)
        u = jnp.square(jnp.maximum(_dot(h, wu_ref[:, cols]), 0.0))
        part = _dot(u, wd_ref[cols, :])
        acc = part if acc is None else acc + part
    x2 = x1 + acc
    if final:
        x2 = _rmsnorm(x2, gf_ref[...])
    return x2


def _norm_proj_even_kernel(x_ref, g_ref, w_ref, o_ref):
    h = _rmsnorm(x_ref[...], g_ref[...])
    o_ref[...] = _dot(h, w_ref[...])


def _norm_proj_odd_kernel(x_ref, g_ref, w_ref, wg1_ref, wg2_ref, bg_ref, o_ref):
    h = _rmsnorm(x_ref[...], g_ref[...]).astype(BF16)
    o_ref[:, :ODD_MAIN] = _dot(h, w_ref[...])
    o_ref[:, ODD_MAIN:] = _log_gate(h, wg1_ref, wg2_ref, bg_ref)


def _out_mlp_kernel(final, x_ref, m_ref, wo_ref, g_ref, wu_ref, wd_ref, gf_ref, o_ref):
    o_ref[...] = _out_mlp_math(final, x_ref[...], m_ref[...], wo_ref, g_ref, wu_ref, wd_ref,
                               gf_ref)


def _norm_proj_even(x, w, layer):
    n = x.shape[0]
    tm = min(ROW_TILE, n)
    return pl.pallas_call(
        _norm_proj_even_kernel,
        grid=(n // tm,),
        in_specs=[pl.BlockSpec((tm, D_MODEL), lambda i: (i, 0)),
                  _layer_spec((1, D_MODEL), layer),
                  _layer_spec((D_MODEL, EVEN_IN), layer // 2)],
        out_specs=pl.BlockSpec((tm, EVEN_IN), lambda i: (i, 0)),
        out_shape=jax.ShapeDtypeStruct((n, EVEN_IN), F32),
        compiler_params=_params("parallel"),
        name="norm_proj_even",
    )(x, w["norm_mix"], w["w_in_even"])


def _norm_proj_odd(x, w, layer):
    n = x.shape[0]
    tm = min(ROW_TILE, n)
    i = layer // 2
    return pl.pallas_call(
        _norm_proj_odd_kernel,
        grid=(n // tm,),
        in_specs=[pl.BlockSpec((tm, D_MODEL), lambda i: (i, 0)),
                  _layer_spec((1, D_MODEL), layer),
                  _layer_spec((D_MODEL, ODD_MAIN), i),
                  _layer_spec((D_MODEL, LANES), i),
                  _layer_spec((LANES, GLA_KEY), i),
                  _layer_spec((1, GLA_KEY), i)],
        out_specs=pl.BlockSpec((tm, ODD_OUT), lambda i: (i, 0)),
        out_shape=jax.ShapeDtypeStruct((n, ODD_OUT), F32),
        compiler_params=_params("parallel"),
        name="norm_proj_odd",
    )(x, w["norm_mix"], w["w_in_odd"], w["wg1"], w["wg2"], w["bg"])


def _out_mlp(x, mix, w, layer):
    n = x.shape[0]
    tm = min(ROW_TILE, n)
    final = layer == DEPTH - 1
    w_out = w["w_out_even"] if layer % 2 == 0 else w["w_out_odd"]
    return pl.pallas_call(
        functools.partial(_out_mlp_kernel, final),
        grid=(n // tm,),
        in_specs=[pl.BlockSpec((tm, D_MODEL), lambda i: (i, 0)),
                  pl.BlockSpec((tm, D_MODEL), lambda i: (i, 0)),
                  _layer_spec((D_MODEL, D_MODEL), layer // 2),
                  _layer_spec((1, D_MODEL), layer),
                  _layer_spec((D_MODEL, D_FF), layer),
                  _layer_spec((D_FF, D_MODEL), layer),
                  _const_spec((1, D_MODEL))],
        out_specs=pl.BlockSpec((tm, D_MODEL), lambda i: (i, 0)),
        out_shape=jax.ShapeDtypeStruct((n, D_MODEL), F32),
        compiler_params=_params("parallel"),
        name="out_mlp_final" if final else "out_mlp",
    )(x, mix, w_out, w["norm_mlp"], w["w_up"], w["w_down"], w["norm_final"])


def _run(stream):
    for _ in stream:
        pass


def _interleave(*streams):
    done = [0.0] * len(streams)
    alive = list(range(len(streams)))
    while alive:
        i = min(alive, key=lambda j: done[j])
        try:
            done[i] += next(streams[i])
        except StopIteration:
            alive.remove(i)


class _Pace:
    def __init__(self):
        self.zero = None

    def after(self, value):
        t = value[0:1, 0:1]
        self.zero = jnp.where(t == t, 0.0, t)

    def __call__(self, x):
        return x if self.zero is None else x + self.zero


def _no_pace(x):
    return x


def _even_mixer_tile(C, chunks, pos_base, load, store, cos_ref, sin_ref,
                     s_src, s_dst, h_src, h_dst, tail_ref, weights, pace=_no_pace):
    cw_ref, cbias_ref, wa_ref, ba_ref, wi_ref, bi_ref, lam_ref = weights
    lane = lax.broadcasted_iota(jnp.int32, (C, RET_KEY), 1)
    first_half = (lane % RET_DK) < (RET_DK // 2)
    ri = lax.broadcasted_iota(jnp.int32, (C, C), 0)
    ci = lax.broadcasted_iota(jnp.int32, (C, C), 1)
    rel = (ri - ci).astype(F32)
    row = lax.broadcasted_iota(jnp.int32, (C, 1), 0).astype(F32)
    trow = lax.broadcasted_iota(jnp.int32, (C, LRU_WIDTH), 0)
    row8 = lax.broadcasted_iota(jnp.int32, (SUBLANES, LRU_WIDTH), 0)
    lam = lam_ref[...]
    softplus_neg_lam = jnp.maximum(-lam, 0.0) + jnp.log(1.0 + jnp.exp(-jnp.abs(lam)))
    log_decay = [math.log1p(-(2.0 ** (-5.0 - h))) for h in range(RET_HEADS)]
    decay_mask = [jnp.where(rel >= 0, jnp.exp(jnp.maximum(rel, 0.0) * lg), 0.0)
                  for lg in log_decay]
    q_decay = [jnp.exp((row + 1.0) * lg) for lg in log_decay]
    k_decay = [jnp.exp((C - 1.0 - row) * lg) for lg in log_decay]
    n_scan = max(1, int(math.log2(C)))

    v0 = 2 * RET_KEY
    g0 = v0 + RET_VALUE
    x0 = g0 + RET_VALUE
    xg0 = x0 + LRU_WIDTH
    heads = range(RET_HEADS)

    J = range(chunks)
    rows = [slice(j * C, (j + 1) * C) for j in J]

    def delayed(x, t, n):
        rolled = pltpu.roll(x, n, axis=0)
        head = jnp.where(row8 < n, pltpu.roll(t, n, axis=0), rolled[0:SUBLANES, :])
        if C == SUBLANES:
            return head
        return jnp.concatenate([head, rolled[SUBLANES:, :]], axis=0)

    qh, kh, q_dec, k_dec, vh, xc = ([None] * chunks for _ in range(6))
    tail = tail_ref[...]
    for j in J:
        cos = cos_ref[rows[j], :]
        sin = sin_ref[rows[j], :]

        def rope(x):
            partner = jnp.where(first_half,
                                pltpu.roll(x, RET_KEY - RET_DK // 2, axis=1),
                                pltpu.roll(x, RET_DK // 2, axis=1))
            return x * cos + partner * sin

        q = rope(load(rows[j], slice(0, RET_KEY)))
        k = rope(load(rows[j], slice(RET_KEY, 2 * RET_KEY))) * (RET_DK ** -0.5)
        q = [q[:, h * RET_DK:(h + 1) * RET_DK] for h in heads]
        k = [k[:, h * RET_DK:(h + 1) * RET_DK] for h in heads]
        q_dec[j] = [(q[h] * q_decay[h]).astype(BF16) for h in heads]
        k_dec[j] = [(k[h] * k_decay[h]).astype(BF16) for h in heads]
        qh[j] = [q[h].astype(BF16) for h in heads]
        kh[j] = [k[h].astype(BF16) for h in heads]
        vh[j] = [load(rows[j], slice(v0 + h * RET_DV, v0 + (h + 1) * RET_DV)).astype(BF16)
                 for h in heads]
        yield 0.08 / chunks

        x_new = load(rows[j], slice(x0, x0 + LRU_WIDTH))
        acc = pace(cbias_ref[...])
        for i in range(CONV_W - 1):
            acc = acc + delayed(x_new, tail, CONV_W - 1 - i) * cw_ref[i:i + 1, :]
        xc[j] = acc + x_new * cw_ref[CONV_W - 1:CONV_W, :]
        tail = x_new[C - SUBLANES:C, :]
        yield 0.10 / chunks
    tail_ref[...] = tail

    scores = [[_dot_nt(qh[j][h], kh[j][h]) for h in heads] for j in J]
    kv = [[_dot_tn(k_dec[j][h], vh[j][h]) for h in heads] for j in J]
    r_pre = [_dot(xc[j], wa_ref[...]) for j in J]
    i_pre = [_dot(xc[j], wi_ref[...]) for j in J]
    yield 0.0

    probs, s_in = [None] * chunks, [None] * chunks
    s_cur = [s_src[h] for h in heads]
    for j in J:
        probs[j] = [(scores[j][h] * decay_mask[h]).astype(BF16) for h in heads]
        s_in[j] = [s_cur[h].astype(BF16) for h in heads]
        s_cur = [s_cur[h] * math.exp(C * log_decay[h]) + kv[j][h] for h in heads]
        yield 0.06 / chunks
    for h in heads:
        s_dst[h] = s_cur[h]

    a_cum, b_loc = [None] * chunks, [None] * chunks
    for j in J:
        r = jax.nn.sigmoid(r_pre[j] + pace(ba_ref[...]))
        gate_i = jax.nn.sigmoid(i_pre[j] + bi_ref[...])
        log_a = -LRU_C * r * softplus_neg_lam
        a = jnp.exp(log_a)
        y = -jnp.tanh(log_a) * (a * a + 1.0)
        mult = jnp.where(y > 0.0, y * lax.rsqrt(y), 0.0)
        pos = trow + (pos_base + j * C)
        mult = jnp.where(pos == 0, 1.0, mult)
        b = mult * (gate_i * xc[j])
        yield 0.20 / chunks
        d = 1
        while d < C:
            b = a * _shift_rows(b, d, pace(0.0)) + b
            a = a * _shift_rows(a, d, pace(1.0))
            d *= 2
            yield 0.36 / (chunks * n_scan)
        a_cum[j], b_loc[j] = a, b

    out = [[_dot(probs[j][h], vh[j][h]) + _dot(q_dec[j][h], s_in[j][h]) for h in heads]
           for j in J]
    yield 0.0

    carry = h_src[...]
    for j in J:
        hidden = b_loc[j] + a_cum[j] * carry
        carry = hidden[C - 1:C, :]
        xg = load(rows[j], slice(xg0, xg0 + LRU_WIDTH))
        store(rows[j], slice(RET_VALUE, RET_VALUE + LRU_WIDTH), hidden * jax.nn.gelu(xg))
        yield 0.08 / chunks
        for h in heads:
            gh = load(rows[j], slice(g0 + h * RET_DV, g0 + (h + 1) * RET_DV))
            store(rows[j], slice(h * RET_DV, (h + 1) * RET_DV),
                  jax.nn.silu(gh) * _head_rms(out[j][h], pace(EPS)))
            yield 0.03 / chunks
    h_dst[...] = carry


def _gla_mixer_tile(C, chunks, load, store, s_src, s_dst, nw, pace=_no_pace):
    k0 = GLA_KEY
    v0 = 2 * GLA_KEY
    r0 = v0 + GLA_VALUE
    la0 = r0 + GLA_VALUE
    ri = lax.broadcasted_iota(jnp.int32, (C, C), 0)
    ci = lax.broadcasted_iota(jnp.int32, (C, C), 1)
    causal = ri >= ci
    di = lax.broadcasted_iota(jnp.int32, (GLA_DK, GLA_DK), 0)
    dj = lax.broadcasted_iota(jnp.int32, (GLA_DK, GLA_DK), 1)
    eye = di == dj
    heads = range(GLA_HEADS)
    key = [slice(h * GLA_DK, (h + 1) * GLA_DK) for h in heads]

    J = range(chunks)
    rows = [slice(j * C, (j + 1) * C) for j in J]

    q_in, k_in, q_st, k_st, decay_col, vh = ([None] * chunks for _ in range(6))
    for j in J:
        b = load(rows[j], slice(la0, la0 + GLA_KEY))
        d = 1
        while d < C:
            b = b + _shift_rows(b, d, pace(0.0))
            d *= 2
        yield 0.12 / chunks
        b_mid = pace(b[C // 2:C // 2 + 1, :])
        b_last = b[C - 1:C, :]
        q = load(rows[j], slice(0, GLA_KEY)) * (GLA_DK ** -0.5)
        k = load(rows[j], slice(k0, k0 + GLA_KEY))
        q_in[j] = (q * jnp.exp(b - b_mid)).astype(BF16)
        k_in[j] = (k * jnp.exp(b_mid - b)).astype(BF16)
        q_st[j] = (q * jnp.exp(b)).astype(BF16)
        k_st[j] = (k * jnp.exp(b_last - b)).astype(BF16)
        s_decay = jnp.exp(b_last)
        decay_col[j] = [jnp.sum(jnp.where(eye, s_decay[:, key[h]], 0.0), axis=1, keepdims=True)
                        for h in heads]
        vh[j] = [load(rows[j], slice(v0 + h * GLA_DV, v0 + (h + 1) * GLA_DV)).astype(BF16)
                 for h in heads]
        yield 0.28 / chunks

    scores = [[_dot_nt(q_in[j][:, key[h]], k_in[j][:, key[h]]) for h in heads] for j in J]
    kv = [[_dot_tn(k_st[j][:, key[h]], vh[j][h]) for h in heads] for j in J]
    yield 0.0

    probs, s_in = [None] * chunks, [None] * chunks
    s_cur = [s_src[h] for h in heads]
    for j in J:
        probs[j] = [jnp.where(causal, scores[j][h], 0.0).astype(BF16) for h in heads]
        s_in[j] = [s_cur[h].astype(BF16) for h in heads]
        s_cur = [s_cur[h] * decay_col[j][h] + kv[j][h] for h in heads]
        yield 0.16 / chunks
    for h in heads:
        s_dst[h] = s_cur[h]

    out = [[_dot(probs[j][h], vh[j][h]) + _dot(q_st[j][:, key[h]], s_in[j][h]) for h in heads]
           for j in J]
    yield 0.0

    for j in J:
        for h in heads:
            rh = load(rows[j], slice(r0 + h * GLA_DV, r0 + (h + 1) * GLA_DV))
            store(rows[j], slice(h * GLA_DV, (h + 1) * GLA_DV),
                  jax.nn.silu(rh) * (_head_rms(out[j][h], pace(EPS)) * nw))
            yield 0.11 / chunks


def _own_state_slice(snew_ref, slot, has_prev):
    if has_prev:
        return snew_ref
    for other in range(snew_ref.shape[0]):
        if other != slot:
            snew_ref[other] = jnp.zeros(snew_ref.shape[1:], F32)
    return snew_ref.at[slot]


def _even_mixer_kernel(pos0, chunk, seqs, slot, has_prev, *refs):
    (proj_ref, cos_ref, sin_ref, s0_ref, h0_ref, cb_ref,
     cw_ref, cbias_ref, wa_ref, ba_ref, wi_ref, bi_ref, lam_ref) = refs[:13]
    refs = refs[13 + (1 if has_prev else 0):]
    mix_ref, snew_ref, hnew_ref, cnew_ref, tail_scr = refs
    weights = (cw_ref, cbias_ref, wa_ref, ba_ref, wi_ref, bi_ref, lam_ref)
    snew_ref = _own_state_slice(snew_ref, slot, has_prev)
    for g in range(seqs):
        tail = tail_scr.at[g]
        tail[...] = jnp.zeros((SUBLANES, LRU_WIDTH), F32)
        tail[SUBLANES - (CONV_W - 1):SUBLANES, :] = cb_ref[g]

        def load(rows, cols, g=g):
            return proj_ref[g, rows, cols]

        def store(rows, cols, val, g=g):
            mix_ref[g, rows, cols] = val

        _run(_even_mixer_tile(chunk, 1, pos0, load, store, cos_ref, sin_ref,
                              s0_ref.at[g], snew_ref.at[g], h0_ref.at[g], hnew_ref.at[g],
                              tail, weights))
        cnew_ref[g] = tail[SUBLANES - (CONV_W - 1):SUBLANES, :]


def _even_mixer(proj, pos0, cos, sin, ret_s, lru_h, conv_buf, w, layer, prev_ret):
    B, T, _ = proj.shape
    i = layer // 2
    n_even = w["w_in_even"].shape[0]
    G = SAMPLE_SEQS
    has_prev = prev_ret is not None
    ret_block = (None, G, RET_HEADS, RET_DK, RET_DV)
    in_specs = [pl.BlockSpec((G, T, EVEN_IN), lambda b: (b, 0, 0)),
                _const_spec((T, RET_KEY)),
                _const_spec((T, RET_KEY)),
                pl.BlockSpec(ret_block, lambda b: (i, b, 0, 0, 0)),
                pl.BlockSpec((None, G, 1, LRU_WIDTH), lambda b: (i, b, 0, 0)),
                pl.BlockSpec((None, G, CONV_W - 1, LRU_WIDTH), lambda b: (i, b, 0, 0)),
                _layer_spec((CONV_W, LRU_WIDTH), i),
                _layer_spec((1, LRU_WIDTH), i),
                _layer_spec((LRU_WIDTH, LRU_WIDTH), i),
                _layer_spec((1, LRU_WIDTH), i),
                _layer_spec((LRU_WIDTH, LRU_WIDTH), i),
                _layer_spec((1, LRU_WIDTH), i),
                _layer_spec((1, LRU_WIDTH), i)]
    args = [proj, cos, sin, ret_s, lru_h, conv_buf, w["conv_w"], w["conv_b"],
            w["wa"], w["ba"], w["wi"], w["bi"], w["lam"]]
    aliases = {}
    if has_prev:
        in_specs.append(pl.BlockSpec(memory_space=pl.ANY))
        args.append(prev_ret)
        aliases = {len(args) - 1: 1}
    if has_prev:
        ret_out = pl.BlockSpec(ret_block, lambda b: (i, b, 0, 0, 0))
    else:
        ret_out = pl.BlockSpec((n_even,) + ret_block[1:], lambda b: (0, b, 0, 0, 0))
    out_shapes = (jax.ShapeDtypeStruct((B, T, D_MODEL), F32),
                  jax.ShapeDtypeStruct((n_even, B, RET_HEADS, RET_DK, RET_DV), F32),
                  jax.ShapeDtypeStruct((B, 1, LRU_WIDTH), F32),
                  jax.ShapeDtypeStruct((B, CONV_W - 1, LRU_WIDTH), F32))
    return pl.pallas_call(
        functools.partial(_even_mixer_kernel, pos0, T, G, i, has_prev),
        grid=(B // G,),
        in_specs=in_specs,
        out_specs=(pl.BlockSpec((G, T, D_MODEL), lambda b: (b, 0, 0)),
                   ret_out,
                   pl.BlockSpec((G, 1, LRU_WIDTH), lambda b: (b, 0, 0)),
                   pl.BlockSpec((G, CONV_W - 1, LRU_WIDTH), lambda b: (b, 0, 0))),
        out_shape=out_shapes,
        scratch_shapes=[pltpu.VMEM((G, SUBLANES, LRU_WIDTH), F32)],
        input_output_aliases=aliases,
        compiler_params=_params("parallel"),
        name="even_mixer",
    )(*args)


def _gla_mixer_kernel(chunk, seqs, slot, has_prev, *refs):
    proj_ref, s0_ref, nw_ref = refs[:3]
    refs = refs[3 + (1 if has_prev else 0):]
    mix_ref, snew_ref = refs
    nw = nw_ref[...]
    snew_ref = _own_state_slice(snew_ref, slot, has_prev)
    for g in range(seqs):
        def load(rows, cols, g=g):
            return proj_ref[g, rows, cols]

        def store(rows, cols, val, g=g):
            mix_ref[g, rows, cols] = val

        _run(_gla_mixer_tile(chunk, 1, load, store, s0_ref.at[g], snew_ref.at[g], nw))


def _gla_mixer(proj, gla_s, w, layer, prev_gla):
    B, T, _ = proj.shape
    i = layer // 2
    n_odd = w["w_in_odd"].shape[0]
    G = SAMPLE_SEQS
    has_prev = prev_gla is not None
    state = (None, G, GLA_HEADS, GLA_DK, GLA_DV)
    in_specs = [pl.BlockSpec((G, T, ODD_OUT), lambda b: (b, 0, 0)),
                pl.BlockSpec(state, lambda b: (i, b, 0, 0, 0)),
                _layer_spec((1, GLA_DV), i)]
    args = [proj, gla_s, w["gla_norm"]]
    aliases = {}
    if has_prev:
        in_specs.append(pl.BlockSpec(memory_space=pl.ANY))
        args.append(prev_gla)
        aliases = {len(args) - 1: 1}
    if has_prev:
        state_out = pl.BlockSpec(state, lambda b: (i, b, 0, 0, 0))
    else:
        state_out = pl.BlockSpec((n_odd,) + state[1:], lambda b: (0, b, 0, 0, 0))
    return pl.pallas_call(
        functools.partial(_gla_mixer_kernel, T, G, i, has_prev),
        grid=(B // G,),
        in_specs=in_specs,
        out_specs=(pl.BlockSpec((G, T, D_MODEL), lambda b: (b, 0, 0)),
                   state_out),
        out_shape=(jax.ShapeDtypeStruct((B, T, D_MODEL), F32),
                   jax.ShapeDtypeStruct((n_odd, B, GLA_HEADS, GLA_DK, GLA_DV), F32)),
        input_output_aliases=aliases,
        compiler_params=_params("parallel"),
        name="gla_mixer",
    )(*args)


def _fused_layer_kernel(even, final, chunk, rows, tiles_per_seq, n_tiles, slot, has_prev, *refs):
    it = iter(refs)
    xlead_ref, xlag_ref, gmix_ref, win_ref = (next(it) for _ in range(4))
    if even:
        cos_ref, sin_ref = next(it), next(it)
        mixer_weights = tuple(next(it) for _ in range(7))
    else:
        wg1_ref, wg2_ref, bg_ref, nw_ref = (next(it) for _ in range(4))
    wo_ref, gmlp_ref, wu_ref, wd_ref, gf_ref = (next(it) for _ in range(5))
    if has_prev:
        next(it)
    out_ref, snew_ref = next(it), next(it)
    if even:
        hnew_ref, cnew_ref = next(it), next(it)
    proj_scr = (next(it), next(it))
    mix_scr = (next(it), next(it))
    x1_scr, u_scr, s_scr = next(it), next(it), next(it)
    if even:
        h_scr, tail_scr = next(it), next(it)

    k = pl.program_id(0)

    @pl.when(k == 0)
    def _():
        proj_scr[1][...] = jnp.zeros(proj_scr[1].shape, F32)
        mix_scr[0][...] = jnp.zeros(mix_scr[0].shape, BF16)
        s_scr[...] = jnp.zeros(s_scr.shape, F32)
        if even:
            h_scr[...] = jnp.zeros(h_scr.shape, F32)
            tail_scr[...] = jnp.zeros(tail_scr.shape, F32)

    main_cols = EVEN_IN if even else ODD_MAIN
    macs_per_row = D_MODEL * (main_cols + D_MODEL + 2 * D_FF)

    def matrix_stream(xlead, xlag, out, proj_a, mix_c, pace, a_first):
        P = MXU_PIECE
        h_in = _rmsnorm(xlead[...], gmix_ref[...]).astype(BF16)

        def out_projection():
            m = mix_c[...]
            for n0 in range(0, D_MODEL, P):
                x1 = xlag[:, n0:n0 + P] + _dot(m, wo_ref[:, n0:n0 + P])
                x1_scr[:, n0:n0 + P] = x1
                pace.after(x1)
                yield D_MODEL * P / macs_per_row

        def in_projection():
            if not even:
                glr = _dot(h_in, wg1_ref[...]).astype(BF16)
            for n0 in range(0, main_cols, P):
                p = _dot(h_in, win_ref[:, n0:n0 + P])
                proj_a[:, n0:n0 + P] = p
                pace.after(p)
                if not even and n0 == P:
                    z = _dot(glr, wg2_ref[...]) + bg_ref[...]
                yield D_MODEL * P / macs_per_row
            if not even:
                proj_a[:, ODD_MAIN:] = _scaled_log_sigmoid(z)
                yield 0.0

        if a_first:
            yield from in_projection()
            yield from out_projection()
            h_mlp = _rmsnorm(x1_scr[...], gmlp_ref[...]).astype(BF16)
        else:
            yield from out_projection()
            h_mlp = _rmsnorm(x1_scr[...], gmlp_ref[...]).astype(BF16)
            yield from in_projection()
        for n0 in range(0, D_FF, P):
            u = jnp.square(jnp.maximum(_dot(h_mlp, wu_ref[:, n0:n0 + P]), 0.0))
            u_scr[:, n0:n0 + P] = u.astype(BF16)
            pace.after(u)
            yield D_MODEL * P / macs_per_row
        for n0 in range(0, D_MODEL, P):
            x2 = x1_scr[:, n0:n0 + P] + _dot(u_scr[...], wd_ref[:, n0:n0 + P])
            if final:
                x1_scr[:, n0:n0 + P] = x2
            else:
                out[:, n0:n0 + P] = x2
            pace.after(x2)
            yield D_FF * P / macs_per_row
        if final:
            out[...] = _rmsnorm(x1_scr[...], gf_ref[...])

    def pipeline_step(par):
        proj_a, proj_b = proj_scr[par], proj_scr[1 - par]
        mix_b, mix_c = mix_scr[1 - par], mix_scr[par]
        half = slice(par * rows, (par + 1) * rows)
        tile_in_seq = lax.rem(jnp.clip(2 * k + par - 1, 0, n_tiles - 1), tiles_per_seq)

        def load(r, c):
            return proj_b[r, c]

        def store(r, c, val):
            mix_b[r, c] = val.astype(BF16)

        pace = _Pace()
        if even:
            start = pl.multiple_of(tile_in_seq * rows, rows)
            vector_stream = _even_mixer_tile(
                chunk, rows // chunk, tile_in_seq * rows, load, store,
                cos_ref.at[pl.ds(start, rows), :], sin_ref.at[pl.ds(start, rows), :],
                s_scr, s_scr, h_scr, h_scr, tail_scr, mixer_weights, pace)
        else:
            vector_stream = _gla_mixer_tile(chunk, rows // chunk, load, store, s_scr, s_scr,
                                            nw_ref[...], pace)
        _interleave(matrix_stream(xlead_ref.at[half, :], xlag_ref.at[half, :],
                                  out_ref.at[half, :], proj_a, mix_c, pace, a_first=par == 1),
                    vector_stream)

    pipeline_step(0)

    state_out = snew_ref if has_prev else snew_ref.at[slot]
    state_out[0] = s_scr[...]
    if not has_prev:
        for other in range(snew_ref.shape[0]):
            if other != slot:
                snew_ref[other] = jnp.zeros(snew_ref.shape[1:], F32)
    if even:
        hnew_ref[0] = h_scr[...]
        cnew_ref[0] = tail_scr[SUBLANES - (CONV_W - 1):SUBLANES, :]

    fresh = lax.rem(2 * k, tiles_per_seq) == 0
    s_scr[...] = jnp.where(fresh, 0.0, s_scr[...])
    if even:
        h_scr[...] = jnp.where(fresh, 0.0, h_scr[...])
        tail_scr[...] = jnp.where(fresh, 0.0, tail_scr[...])

    pipeline_step(1)


def _fused_layer(x, w, layer, cos, sin, prev_state, seq_len):
    n_rows = x.shape[0]
    B = n_rows // seq_len
    R = FUSED_ROWS
    even = layer % 2 == 0
    final = layer == DEPTH - 1
    i = layer // 2
    C = math.gcd(seq_len, RET_CHUNK if even else GLA_CHUNK)
    tps = seq_len // R
    NT = n_rows // R
    assert tps % 2 == 0 and NT % 2 == 0
    pairs = NT // 2
    has_prev = prev_state is not None

    def lead(k):
        return (jnp.minimum(k, pairs - 1), 0)

    def lag(k):
        return (jnp.clip(k - 1, 0, pairs - 1), 0)

    def seq_of(k):
        return jnp.clip(2 * k - 1, 0, NT - 1) // tps

    in_specs = [pl.BlockSpec((2 * R, D_MODEL), lead),
                pl.BlockSpec((2 * R, D_MODEL), lag),
                _layer_spec((1, D_MODEL), layer)]
    args = [x, x, w["norm_mix"]]
    if even:
        in_specs += [_layer_spec((D_MODEL, EVEN_IN), i),
                     _const_spec((seq_len, RET_KEY)),
                     _const_spec((seq_len, RET_KEY)),
                     _layer_spec((CONV_W, LRU_WIDTH), i),
                     _layer_spec((1, LRU_WIDTH), i),
                     _layer_spec((LRU_WIDTH, LRU_WIDTH), i),
                     _layer_spec((1, LRU_WIDTH), i),
                     _layer_spec((LRU_WIDTH, LRU_WIDTH), i),
                     _layer_spec((1, LRU_WIDTH), i),
                     _layer_spec((1, LRU_WIDTH), i)]
        args += [w["w_in_even"], cos, sin, w["conv_w"], w["conv_b"], w["wa"], w["ba"],
                 w["wi"], w["bi"], w["lam"]]
        w_out = w["w_out_even"]
        state_shape = (RET_HEADS, RET_DK, RET_DV)
        n_stack = w["w_in_even"].shape[0]
        proj_cols = EVEN_IN
    else:
        in_specs += [_layer_spec((D_MODEL, ODD_MAIN), i),
                     _layer_spec((D_MODEL, LANES), i),
                     _layer_spec((LANES, GLA_KEY), i),
                     _layer_spec((1, GLA_KEY), i),
                     _layer_spec((1, GLA_DV), i)]
        args += [w["w_in_odd"], w["wg1"], w["wg2"], w["bg"], w["gla_norm"]]
        w_out = w["w_out_odd"]
        state_shape = (GLA_HEADS, GLA_DK, GLA_DV)
        n_stack = w["w_in_odd"].shape[0]
        proj_cols = ODD_OUT
    in_specs += [_layer_spec((D_MODEL, D_MODEL), i),
                 _layer_spec((1, D_MODEL), layer),
                 _layer_spec((D_MODEL, D_FF), layer),
                 _layer_spec((D_FF, D_MODEL), layer),
                 _const_spec((1, D_MODEL))]
    args += [w_out, w["norm_mlp"], w["w_up"], w["w_down"], w["norm_final"]]
    aliases = {}
    if has_prev:
        in_specs.append(pl.BlockSpec(memory_space=pl.ANY))
        args.append(prev_state)
        aliases = {len(args) - 1: 1}
        state_spec = pl.BlockSpec((None, 1) + state_shape, lambda k: (i, seq_of(k), 0, 0, 0))
    else:
        state_spec = pl.BlockSpec((n_stack, 1) + state_shape, lambda k: (0, seq_of(k), 0, 0, 0))

    out_specs = [pl.BlockSpec((2 * R, D_MODEL), lag), state_spec]
    out_shapes = [jax.ShapeDtypeStruct((n_rows, D_MODEL), F32),
                  jax.ShapeDtypeStruct((n_stack, B) + state_shape, F32)]
    scratch = [pltpu.VMEM((R, proj_cols), F32), pltpu.VMEM((R, proj_cols), F32),
               pltpu.VMEM((R, D_MODEL), BF16), pltpu.VMEM((R, D_MODEL), BF16),
               pltpu.VMEM((R, D_MODEL), F32), pltpu.VMEM((R, D_FF), BF16),
               pltpu.VMEM(state_shape, F32)]
    if even:
        out_specs += [pl.BlockSpec((1, 1, LRU_WIDTH), lambda k: (seq_of(k), 0, 0)),
                      pl.BlockSpec((1, CONV_W - 1, LRU_WIDTH), lambda k: (seq_of(k), 0, 0))]
        out_shapes += [jax.ShapeDtypeStruct((B, 1, LRU_WIDTH), F32),
                       jax.ShapeDtypeStruct((B, CONV_W - 1, LRU_WIDTH), F32)]
        scratch += [pltpu.VMEM((1, LRU_WIDTH), F32),
                    pltpu.VMEM((SUBLANES, LRU_WIDTH), F32)]
    return pl.pallas_call(
        functools.partial(_fused_layer_kernel, even, final, C, R, tps, NT, i, has_prev),
        grid=(pairs + 1,),
        in_specs=in_specs,
        out_specs=tuple(out_specs),
        out_shape=tuple(out_shapes),
        scratch_shapes=scratch,
        input_output_aliases=aliases,
        compiler_params=_params("arbitrary"),
        name="layer_even" if even else "layer_odd",
    )(*args)


def _rope_tables(pos0, T):
    half = RET_DK // 2
    inv = ROPE_BASE ** (-jnp.arange(half, dtype=F32) / half)
    ang = (pos0 + jnp.arange(T, dtype=jnp.int32)).astype(F32)[:, None] * inv[None, :]
    cos = jnp.cos(ang)
    sin = jnp.sin(ang)
    cos_full = jnp.tile(jnp.concatenate([cos, cos], axis=-1), (1, RET_HEADS))
    sin_signed = jnp.tile(jnp.concatenate([-sin, sin], axis=-1), (1, RET_HEADS))
    return cos_full, sin_signed


def _block_diag(w):
    nl, nb, c, d = w.shape
    eye = jnp.eye(nb, dtype=w.dtype)
    return (eye[None, :, None, :, None] * w[:, :, :, None, :]).reshape(nl, nb * c, nb * d)


def _prepare_weights(norm_mix, norm_mlp, norm_final, w_in_even, w_out_even, conv_w, conv_b,
                     lru_w_a, lru_b_a, lru_w_i, lru_b_i, lru_lambda,
                     w_in_odd, gla_w_gate2, gla_b_gate, gla_norm, w_out_odd, w_up, w_down):
    n_even = w_in_even.shape[0]
    n_odd = w_in_odd.shape[0]
    pad1 = LANES - GLA_RANK
    return dict(
        norm_mix=norm_mix.reshape(DEPTH, 1, D_MODEL),
        norm_mlp=norm_mlp.reshape(DEPTH, 1, D_MODEL),
        norm_final=norm_final.reshape(1, D_MODEL),
        w_in_even=w_in_even.astype(BF16),
        w_out_even=w_out_even.astype(BF16),
        conv_w=conv_w,
        conv_b=conv_b.reshape(n_even, 1, LRU_WIDTH),
        wa=_block_diag(lru_w_a).astype(BF16),
        ba=lru_b_a.reshape(n_even, 1, LRU_WIDTH),
        wi=_block_diag(lru_w_i).astype(BF16),
        bi=lru_b_i.reshape(n_even, 1, LRU_WIDTH),
        lam=lru_lambda.reshape(n_even, 1, LRU_WIDTH),
        w_in_odd=w_in_odd[:, :, :ODD_MAIN].astype(BF16),
        wg1=jnp.pad(w_in_odd[:, :, ODD_MAIN:], ((0, 0), (0, 0), (0, pad1))).astype(BF16),
        wg2=jnp.pad(gla_w_gate2, ((0, 0), (0, pad1), (0, 0))).astype(BF16),
        bg=gla_b_gate.reshape(n_odd, 1, GLA_KEY),
        gla_norm=gla_norm.reshape(n_odd, 1, GLA_DV),
        w_out_odd=w_out_odd.astype(BF16),
        w_up=w_up.astype(BF16),
        w_down=w_down.astype(BF16),
    )


def _prompt_trunk(x, w):
    B, T, _ = x.shape
    xf = x.reshape(B * T, D_MODEL)
    cos, sin = _rope_tables(0, T)
    ret_new, gla_new = None, None
    lrus, convs = [], []
    for layer in range(DEPTH):
        if layer % 2 == 0:
            xf, ret_new, lh, cb = _fused_layer(xf, w, layer, cos, sin, ret_new, T)
            lrus.append(lh.reshape(B, LRU_WIDTH))
            convs.append(cb)
        else:
            xf, gla_new = _fused_layer(xf, w, layer, cos, sin, gla_new, T)
    return xf.reshape(B, T, D_MODEL), ret_new, jnp.stack(lrus), jnp.stack(convs), gla_new


def _sample_trunk(x, pos0, ret_s, lru_h, conv_buf, gla_s, w):
    B, T, _ = x.shape
    xf = x.reshape(B * T, D_MODEL)
    cos, sin = _rope_tables(pos0, T)
    lru_h = lru_h.reshape(lru_h.shape[0], B, 1, LRU_WIDTH)
    ret_new, gla_new = None, None
    lrus, convs = [], []
    for layer in range(DEPTH):
        if layer % 2 == 0:
            proj = _norm_proj_even(xf, w, layer)
            mix, ret_new, lh, cb = _even_mixer(proj.reshape(B, T, EVEN_IN), pos0, cos, sin,
                                               ret_s, lru_h, conv_buf, w, layer, ret_new)
            lrus.append(lh.reshape(B, LRU_WIDTH))
            convs.append(cb)
        else:
            proj = _norm_proj_odd(xf, w, layer)
            mix, gla_new = _gla_mixer(proj.reshape(B, T, ODD_OUT), gla_s, w, layer, gla_new)
        xf = _out_mlp(xf, mix.reshape(B * T, D_MODEL), w, layer)
    return xf.reshape(B, T, D_MODEL), ret_new, jnp.stack(lrus), jnp.stack(convs), gla_new


def kernel(x_prompt, x_sample, state_ret, state_lru, state_conv, state_gla, norm_mix, norm_mlp, norm_final, w_in_even, w_out_even, conv_w, conv_b, lru_w_a, lru_b_a, lru_w_i, lru_b_i, lru_lambda, w_in_odd, gla_w_gate2, gla_b_gate, gla_norm, w_out_odd, w_up, w_down):
    w = _prepare_weights(norm_mix, norm_mlp, norm_final, w_in_even, w_out_even, conv_w, conv_b,
                         lru_w_a, lru_b_a, lru_w_i, lru_b_i, lru_lambda,
                         w_in_odd, gla_w_gate2, gla_b_gate, gla_norm, w_out_odd, w_up, w_down)
    y_p, ret_p, lru_p, conv_p, gla_p = _prompt_trunk(x_prompt, w)
    y_s, ret_s, lru_s, conv_s, gla_s = _sample_trunk(x_sample, PAST_LEN, state_ret, state_lru,
                                                     state_conv, state_gla, w)
    return (y_p, y_s, ret_p, ret_s, lru_p, lru_s, conv_p, conv_s, gla_p, gla_s)
```

```python
import functools
import math

import jax
import jax.numpy as jnp
from jax import lax
from jax.experimental import pallas as pl
from jax.experimental.pallas import tpu as pltpu

F32 = jnp.float32
BF16 = jnp.bfloat16

D_MODEL = 1024
DEPTH = 4
PAST_LEN = 16384
EPS = 1e-6
ROPE_BASE = 10000.0

RET_HEADS = 4
RET_DK = 64
RET_DV = 128
RET_KEY = RET_HEADS * RET_DK
RET_VALUE = RET_HEADS * RET_DV
RET_CHUNK = 128

LRU_WIDTH = 512
CONV_W = 4
LRU_C = 8.0

GLA_HEADS = 4
GLA_KEY = 512
GLA_VALUE = 1024
GLA_DK = GLA_KEY // GLA_HEADS
GLA_DV = GLA_VALUE // GLA_HEADS
GLA_RANK = 16
GLA_TAU = 16.0
GLA_CHUNK = 64

D_FF = 4 * D_MODEL
FF_CHUNK = 1024

EVEN_IN = 2 * RET_KEY + 2 * RET_VALUE + 2 * LRU_WIDTH
ODD_MAIN = 2 * GLA_KEY + 2 * GLA_VALUE
ODD_OUT = ODD_MAIN + GLA_KEY

LANES = 128
SUBLANES = 8
ROW_TILE = 512
FUSED_ROWS = 256
MXU_PIECE = 256
SAMPLE_SEQS = 8
VMEM_LIMIT = 56 * 1024 * 1024


def _dot(a, b):
    return jnp.dot(a.astype(BF16), b.astype(BF16), preferred_element_type=F32)


def _dot_nt(a, b):
    return lax.dot_general(a.astype(BF16), b.astype(BF16), (((1,), (1,)), ((), ())),
                           preferred_element_type=F32)


def _dot_tn(a, b):
    return lax.dot_general(a.astype(BF16), b.astype(BF16), (((0,), (0,)), ((), ())),
                           preferred_element_type=F32)


def _rmsnorm(x, g):
    return x * lax.rsqrt(jnp.mean(x * x, axis=-1, keepdims=True) + EPS) * g


def _head_rms(x, eps=EPS):
    return x * lax.rsqrt(jnp.mean(x * x, axis=-1, keepdims=True) + eps)


def _shift_rows(x, d, fill):
    row = lax.broadcasted_iota(jnp.int32, x.shape, 0)
    return jnp.where(row >= d, pltpu.roll(x, d, axis=0), fill)


def _layer_spec(shape, layer):
    n = len(shape)
    return pl.BlockSpec((None,) + tuple(shape), lambda *_: (layer,) + (0,) * n,
                        pipeline_mode=pl.Buffered(1))


def _const_spec(shape):
    n = len(shape)
    return pl.BlockSpec(shape, lambda *_: (0,) * n, pipeline_mode=pl.Buffered(1))


def _params(*semantics):
    return pltpu.CompilerParams(dimension_semantics=semantics, vmem_limit_bytes=VMEM_LIMIT)


def _scaled_log_sigmoid(z):
    return (jnp.minimum(z, 0.0) - jnp.log(1.0 + jnp.exp(-jnp.abs(z)))) / GLA_TAU


def _log_gate(h, wg1_ref, wg2_ref, bg_ref):
    glr = _dot(h, wg1_ref[...])
    return _scaled_log_sigmoid(_dot(glr, wg2_ref[...]) + bg_ref[...])


def _out_mlp_math(final, x, m, wo_ref, g_ref, wu_ref, wd_ref, gf_ref):
    x1 = x + _dot(m, wo_ref[...])
    h = _rmsnorm(x1, g_ref[...]).astype(BF16)
    acc = None
    for c in range(D_FF // FF_CHUNK):
        cols = slice(c * FF_CHUNK, (c + 1) * FF_CHUNK)
        u = jnp.square(jnp.maximum(_dot(h, wu_ref[:, cols]), 0.0))
        part = _dot(u, wd_ref[cols, :])
        acc = part if acc is None else acc + part
    x2 = x1 + acc
    if final:
        x2 = _rmsnorm(x2, gf_ref[...])
    return x2


def _norm_proj_even_kernel(x_ref, g_ref, w_ref, o_ref):
    h = _rmsnorm(x_ref[...], g_ref[...])
    o_ref[...] = _dot(h, w_ref[...])


def _norm_proj_odd_kernel(x_ref, g_ref, w_ref, wg1_ref, wg2_ref, bg_ref, o_ref):
    h = _rmsnorm(x_ref[...], g_ref[...]).astype(BF16)
    o_ref[:, :ODD_MAIN] = _dot(h, w_ref[...])
    o_ref[:, ODD_MAIN:] = _log_gate(h, wg1_ref, wg2_ref, bg_ref)


def _out_mlp_kernel(final, x_ref, m_ref, wo_ref, g_ref, wu_ref, wd_ref, gf_ref, o_ref):
    o_ref[...] = _out_mlp_math(final, x_ref[...], m_ref[...], wo_ref, g_ref, wu_ref, wd_ref,
                               gf_ref)


def _norm_proj_even(x, w, layer):
    n = x.shape[0]
    tm = min(ROW_TILE, n)
    return pl.pallas_call(
        _norm_proj_even_kernel,
        grid=(n // tm,),
        in_specs=[pl.BlockSpec((tm, D_MODEL), lambda i: (i, 0)),
                  _layer_spec((1, D_MODEL), layer),
                  _layer_spec((D_MODEL, EVEN_IN), layer // 2)],
        out_specs=pl.BlockSpec((tm, EVEN_IN), lambda i: (i, 0)),
        out_shape=jax.ShapeDtypeStruct((n, EVEN_IN), F32),
        compiler_params=_params("parallel"),
        name="norm_proj_even",
    )(x, w["norm_mix"], w["w_in_even"])


def _norm_proj_odd(x, w, layer):
    n = x.shape[0]
    tm = min(ROW_TILE, n)
    i = layer // 2
    return pl.pallas_call(
        _norm_proj_odd_kernel,
        grid=(n // tm,),
        in_specs=[pl.BlockSpec((tm, D_MODEL), lambda i: (i, 0)),
                  _layer_spec((1, D_MODEL), layer),
                  _layer_spec((D_MODEL, ODD_MAIN), i),
                  _layer_spec((D_MODEL, LANES), i),
                  _layer_spec((LANES, GLA_KEY), i),
                  _layer_spec((1, GLA_KEY), i)],
        out_specs=pl.BlockSpec((tm, ODD_OUT), lambda i: (i, 0)),
        out_shape=jax.ShapeDtypeStruct((n, ODD_OUT), F32),
        compiler_params=_params("parallel"),
        name="norm_proj_odd",
    )(x, w["norm_mix"], w["w_in_odd"], w["wg1"], w["wg2"], w["bg"])


def _out_mlp(x, mix, w, layer):
    n = x.shape[0]
    tm = min(ROW_TILE, n)
    final = layer == DEPTH - 1
    w_out = w["w_out_even"] if layer % 2 == 0 else w["w_out_odd"]
    return pl.pallas_call(
        functools.partial(_out_mlp_kernel, final),
        grid=(n // tm,),
        in_specs=[pl.BlockSpec((tm, D_MODEL), lambda i: (i, 0)),
                  pl.BlockSpec((tm, D_MODEL), lambda i: (i, 0)),
                  _layer_spec((D_MODEL, D_MODEL), layer // 2),
                  _layer_spec((1, D_MODEL), layer),
                  _layer_spec((D_MODEL, D_FF), layer),
                  _layer_spec((D_FF, D_MODEL), layer),
                  _const_spec((1, D_MODEL))],
        out_specs=pl.BlockSpec((tm, D_MODEL), lambda i: (i, 0)),
        out_shape=jax.ShapeDtypeStruct((n, D_MODEL), F32),
        compiler_params=_params("parallel"),
        name="out_mlp_final" if final else "out_mlp",
    )(x, mix, w_out, w["norm_mlp"], w["w_up"], w["w_down"], w["norm_final"])


def _run(stream):
    for _ in stream:
        pass


def _interleave(*streams):
    done = [0.0] * len(streams)
    alive = list(range(len(streams)))
    while alive:
        i = min(alive, key=lambda j: done[j])
        try:
            done[i] += next(streams[i])
        except StopIteration:
            alive.remove(i)


class _Pace:
    def __init__(self):
        self.zero = None

    def after(self, value):
        t = value[0:1, 0:1]
        self.zero = jnp.where(t == t, 0.0, t)

    def __call__(self, x):
        return x if self.zero is None else x + self.zero


def _no_pace(x):
    return x


def _even_mixer_tile(C, chunks, pos_base, load, store, cos_ref, sin_ref,
                     s_src, s_dst, h_src, h_dst, tail_ref, weights, pace=_no_pace):
    cw_ref, cbias_ref, wa_ref, ba_ref, wi_ref, bi_ref, lam_ref = weights
    lane = lax.broadcasted_iota(jnp.int32, (C, RET_KEY), 1)
    first_half = (lane % RET_DK) < (RET_DK // 2)
    ri = lax.broadcasted_iota(jnp.int32, (C, C), 0)
    ci = lax.broadcasted_iota(jnp.int32, (C, C), 1)
    rel = (ri - ci).astype(F32)
    row = lax.broadcasted_iota(jnp.int32, (C, 1), 0).astype(F32)
    trow = lax.broadcasted_iota(jnp.int32, (C, LRU_WIDTH), 0)
    row8 = lax.broadcasted_iota(jnp.int32, (SUBLANES, LRU_WIDTH), 0)
    lam = lam_ref[...]
    softplus_neg_lam = jnp.maximum(-lam, 0.0) + jnp.log(1.0 + jnp.exp(-jnp.abs(lam)))
    log_decay = [math.log1p(-(2.0 ** (-5.0 - h))) for h in range(RET_HEADS)]
    decay_mask = [jnp.where(rel >= 0, jnp.exp(jnp.maximum(rel, 0.0) * lg), 0.0)
                  for lg in log_decay]
    q_decay = [jnp.exp((row + 1.0) * lg) for lg in log_decay]
    k_decay = [jnp.exp((C - 1.0 - row) * lg) for lg in log_decay]
    n_scan = max(1, int(math.log2(C)))

    v0 = 2 * RET_KEY
    g0 = v0 + RET_VALUE
    x0 = g0 + RET_VALUE
    xg0 = x0 + LRU_WIDTH
    heads = range(RET_HEADS)
    J = range(chunks)
    rows = [slice(j * C, (j + 1) * C) for j in J]

    def delayed(x, t, n):
        rolled = pltpu.roll(x, n, axis=0)
        head = jnp.where(row8 < n, pltpu.roll(t, n, axis=0), rolled[0:SUBLANES, :])
        if C == SUBLANES:
            return head
        return jnp.concatenate([head, rolled[SUBLANES:, :]], axis=0)

    qh, kh, q_dec, k_dec, vh, xc = ([None] * chunks for _ in range(6))
    tail = tail_ref[...]
    for j in J:
        cos = cos_ref[rows[j], :]
        sin = sin_ref[rows[j], :]

        def rope(x):
            partner = jnp.where(first_half,
                                pltpu.roll(x, RET_KEY - RET_DK // 2, axis=1),
                                pltpu.roll(x, RET_DK // 2, axis=1))
            return x * cos + partner * sin

        q = rope(load(rows[j], slice(0, RET_KEY)))
        k = rope(load(rows[j], slice(RET_KEY, 2 * RET_KEY))) * (RET_DK ** -0.5)
        q = [q[:, h * RET_DK:(h + 1) * RET_DK] for h in heads]
        k = [k[:, h * RET_DK:(h + 1) * RET_DK] for h in heads]
        q_dec[j] = [(q[h] * q_decay[h]).astype(BF16) for h in heads]
        k_dec[j] = [(k[h] * k_decay[h]).astype(BF16) for h in heads]
        qh[j] = [q[h].astype(BF16) for h in heads]
        kh[j] = [k[h].astype(BF16) for h in heads]
        vh[j] = [load(rows[j], slice(v0 + h * RET_DV, v0 + (h + 1) * RET_DV)).astype(BF16)
                 for h in heads]
        yield 0.08 / chunks

        x_new = load(rows[j], slice(x0, x0 + LRU_WIDTH))
        acc = pace(cbias_ref[...])
        for i in range(CONV_W - 1):
            acc = acc + delayed(x_new, tail, CONV_W - 1 - i) * cw_ref[i:i + 1, :]
        xc[j] = acc + x_new * cw_ref[CONV_W - 1:CONV_W, :]
        tail = x_new[C - SUBLANES:C, :]
        yield 0.10 / chunks
    tail_ref[...] = tail

    scores = [[_dot_nt(qh[j][h], kh[j][h]) for h in heads] for j in J]
    kv = [[_dot_tn(k_dec[j][h], vh[j][h]) for h in heads] for j in J]
    r_pre = [_dot(xc[j], wa_ref[...]) for j in J]
    i_pre = [_dot(xc[j], wi_ref[...]) for j in J]
    yield 0.0

    probs, s_in = [None] * chunks, [None] * chunks
    s_cur = [s_src[h] for h in heads]
    for j in J:
        probs[j] = [(scores[j][h] * decay_mask[h]).astype(BF16) for h in heads]
        s_in[j] = [s_cur[h].astype(BF16) for h in heads]
        s_cur = [s_cur[h] * math.exp(C * log_decay[h]) + kv[j][h] for h in heads]
        yield 0.06 / chunks
    for h in heads:
        s_dst[h] = s_cur[h]

    a_cum, b_loc = [None] * chunks, [None] * chunks
    for j in J:
        r = jax.nn.sigmoid(r_pre[j] + pace(ba_ref[...]))
        gate_i = jax.nn.sigmoid(i_pre[j] + bi_ref[...])
        log_a = -LRU_C * r * softplus_neg_lam
        a = jnp.exp(log_a)
        y = -jnp.tanh(log_a) * (a * a + 1.0)
        mult = jnp.where(y > 0.0, y * lax.rsqrt(y), 0.0)
        pos = trow + (pos_base + j * C)
        mult = jnp.where(pos == 0, 1.0, mult)
        b = mult * (gate_i * xc[j])
        yield 0.20 / chunks
        d = 1
        while d < C:
            b = a * _shift_rows(b, d, pace(0.0)) + b
            a = a * _shift_rows(a, d, pace(1.0))
            d *= 2
            yield 0.36 / (chunks * n_scan)
        a_cum[j], b_loc[j] = a, b

    out = [[_dot(probs[j][h], vh[j][h]) + _dot(q_dec[j][h], s_in[j][h]) for h in heads]
           for j in J]
    yield 0.0

    carry = h_src[...]
    for j in J:
        hidden = b_loc[j] + a_cum[j] * carry
        carry = hidden[C - 1:C, :]
        xg = load(rows[j], slice(xg0, xg0 + LRU_WIDTH))
        store(rows[j], slice(RET_VALUE, RET_VALUE + LRU_WIDTH), hidden * jax.nn.gelu(xg))
        yield 0.08 / chunks
        for h in heads:
            gh = load(rows[j], slice(g0 + h * RET_DV, g0 + (h + 1) * RET_DV))
            store(rows[j], slice(h * RET_DV, (h + 1) * RET_DV),
                  jax.nn.silu(gh) * _head_rms(out[j][h], pace(EPS)))
            yield 0.03 / chunks
    h_dst[...] = carry


def _gla_mixer_tile(C, chunks, load, store, s_src, s_dst, nw, pace=_no_pace):
    k0 = GLA_KEY
    v0 = 2 * GLA_KEY
    r0 = v0 + GLA_VALUE
    la0 = r0 + GLA_VALUE
    ri = lax.broadcasted_iota(jnp.int32, (C, C), 0)
    ci = lax.broadcasted_iota(jnp.int32, (C, C), 1)
    causal = ri >= ci
    di = lax.broadcasted_iota(jnp.int32, (GLA_DK, GLA_DK), 0)
    dj = lax.broadcasted_iota(jnp.int32, (GLA_DK, GLA_DK), 1)
    eye = di == dj
    heads = range(GLA_HEADS)
    key = [slice(h * GLA_DK, (h + 1) * GLA_DK) for h in heads]
    J = range(chunks)
    rows = [slice(j * C, (j + 1) * C) for j in J]

    q_in, k_in, q_st, k_st, decay_col, vh = ([None] * chunks for _ in range(6))
    for j in J:
        b = load(rows[j], slice(la0, la0 + GLA_KEY))
        d = 1
        while d < C:
            b = b + _shift_rows(b, d, pace(0.0))
            d *= 2
        yield 0.12 / chunks
        b_mid = pace(b[C // 2:C // 2 + 1, :])
        b_last = b[C - 1:C, :]
        q = load(rows[j], slice(0, GLA_KEY)) * (GLA_DK ** -0.5)
        k = load(rows[j], slice(k0, k0 + GLA_KEY))
        q_in[j] = (q * jnp.exp(b - b_mid)).astype(BF16)
        k_in[j] = (k * jnp.exp(b_mid - b)).astype(BF16)
        q_st[j] = (q * jnp.exp(b)).astype(BF16)
        k_st[j] = (k * jnp.exp(b_last - b)).astype(BF16)
        s_decay = jnp.exp(b_last)
        decay_col[j] = [jnp.sum(jnp.where(eye, s_decay[:, key[h]], 0.0), axis=1, keepdims=True)
                        for h in heads]
        vh[j] = [load(rows[j], slice(v0 + h * GLA_DV, v0 + (h + 1) * GLA_DV)).astype(BF16)
                 for h in heads]
        yield 0.28 / chunks

    scores = [[_dot_nt(q_in[j][:, key[h]], k_in[j][:, key[h]]) for h in heads] for j in J]
    kv = [[_dot_tn(k_st[j][:, key[h]], vh[j][h]) for h in heads] for j in J]
    yield 0.0

    probs, s_in = [None] * chunks, [None] * chunks
    s_cur = [s_src[h] for h in heads]
    for j in J:
        probs[j] = [jnp.where(causal, scores[j][h], 0.0).astype(BF16) for h in heads]
        s_in[j] = [s_cur[h].astype(BF16) for h in heads]
        s_cur = [s_cur[h] * decay_col[j][h] + kv[j][h] for h in heads]
        yield 0.16 / chunks
    for h in heads:
        s_dst[h] = s_cur[h]

    out = [[_dot(probs[j][h], vh[j][h]) + _dot(q_st[j][:, key[h]], s_in[j][h]) for h in heads]
           for j in J]
    yield 0.0

    for j in J:
        for h in heads:
            rh = load(rows[j], slice(r0 + h * GLA_DV, r0 + (h + 1) * GLA_DV))
            store(rows[j], slice(h * GLA_DV, (h + 1) * GLA_DV),
                  jax.nn.silu(rh) * (_head_rms(out[j][h], pace(EPS)) * nw))
            yield 0.11 / chunks


def _even_mixer_kernel(pos0, chunk, seqs, *refs):
    (proj_ref, cos_ref, sin_ref, s0_ref, h0_ref, cb_ref,
     cw_ref, cbias_ref, wa_ref, ba_ref, wi_ref, bi_ref, lam_ref) = refs[:13]
    refs = refs[14:]
    mix_ref, snew_ref, hnew_ref, cnew_ref, tail_scr = refs
    weights = (cw_ref, cbias_ref, wa_ref, ba_ref, wi_ref, bi_ref, lam_ref)
    for g in range(seqs):
        tail = tail_scr.at[g]
        tail[...] = jnp.zeros((SUBLANES, LRU_WIDTH), F32)
        tail[SUBLANES - (CONV_W - 1):SUBLANES, :] = cb_ref[g]

        def load(rows, cols, g=g):
            return proj_ref[g, rows, cols]

        def store(rows, cols, val, g=g):
            mix_ref[g, rows, cols] = val

        _run(_even_mixer_tile(chunk, 1, pos0, load, store, cos_ref, sin_ref,
                              s0_ref.at[g], snew_ref.at[g], h0_ref.at[g], hnew_ref.at[g],
                              tail, weights))
        cnew_ref[g] = tail[SUBLANES - (CONV_W - 1):SUBLANES, :]


def _even_mixer(proj, pos0, cos, sin, ret_s, lru_h, conv_buf, w, layer, prev_ret):
    B, T, _ = proj.shape
    i = layer // 2
    n_even = w["w_in_even"].shape[0]
    G = SAMPLE_SEQS
    ret_block = (None, G, RET_HEADS, RET_DK, RET_DV)
    in_specs = [pl.BlockSpec((G, T, EVEN_IN), lambda b: (b, 0, 0)),
                _const_spec((T, RET_KEY)),
                _const_spec((T, RET_KEY)),
                pl.BlockSpec(ret_block, lambda b: (i, b, 0, 0, 0)),
                pl.BlockSpec((None, G, 1, LRU_WIDTH), lambda b: (i, b, 0, 0)),
                pl.BlockSpec((None, G, CONV_W - 1, LRU_WIDTH), lambda b: (i, b, 0, 0)),
                _layer_spec((CONV_W, LRU_WIDTH), i),
                _layer_spec((1, LRU_WIDTH), i),
                _layer_spec((LRU_WIDTH, LRU_WIDTH), i),
                _layer_spec((1, LRU_WIDTH), i),
                _layer_spec((LRU_WIDTH, LRU_WIDTH), i),
                _layer_spec((1, LRU_WIDTH), i),
                _layer_spec((1, LRU_WIDTH), i)]
    args = [proj, cos, sin, ret_s, lru_h, conv_buf, w["conv_w"], w["conv_b"],
            w["wa"], w["ba"], w["wi"], w["bi"], w["lam"]]
    in_specs.append(pl.BlockSpec(memory_space=pl.ANY))
    args.append(prev_ret)
    aliases = {len(args) - 1: 1}
    out_shapes = (jax.ShapeDtypeStruct((B, T, D_MODEL), F32),
                  jax.ShapeDtypeStruct((n_even, B, RET_HEADS, RET_DK, RET_DV), F32),
                  jax.ShapeDtypeStruct((B, 1, LRU_WIDTH), F32),
                  jax.ShapeDtypeStruct((B, CONV_W - 1, LRU_WIDTH), F32))
    return pl.pallas_call(
        functools.partial(_even_mixer_kernel, pos0, T, G),
        grid=(B // G,),
        in_specs=in_specs,
        out_specs=(pl.BlockSpec((G, T, D_MODEL), lambda b: (b, 0, 0)),
                   pl.BlockSpec(ret_block, lambda b: (i, b, 0, 0, 0)),
                   pl.BlockSpec((G, 1, LRU_WIDTH), lambda b: (b, 0, 0)),
                   pl.BlockSpec((G, CONV_W - 1, LRU_WIDTH), lambda b: (b, 0, 0))),
        out_shape=out_shapes,
        scratch_shapes=[pltpu.VMEM((G, SUBLANES, LRU_WIDTH), F32)],
        input_output_aliases=aliases,
        compiler_params=_params("parallel"),
        name="even_mixer",
    )(*args)


def _gla_mixer_kernel(chunk, seqs, *refs):
    proj_ref, s0_ref, nw_ref = refs[:3]
    refs = refs[4:]
    mix_ref, snew_ref = refs
    nw = nw_ref[...]
    for g in range(seqs):
        def load(rows, cols, g=g):
            return proj_ref[g, rows, cols]

        def store(rows, cols, val, g=g):
            mix_ref[g, rows, cols] = val

        _run(_gla_mixer_tile(chunk, 1, load, store, s0_ref.at[g], snew_ref.at[g], nw))


def _gla_mixer(proj, gla_s, w, layer, prev_gla):
    B, T, _ = proj.shape
    i = layer // 2
    n_odd = w["w_in_odd"].shape[0]
    G = SAMPLE_SEQS
    state = (None, G, GLA_HEADS, GLA_DK, GLA_DV)
    in_specs = [pl.BlockSpec((G, T, ODD_OUT), lambda b: (b, 0, 0)),
                pl.BlockSpec(state, lambda b: (i, b, 0, 0, 0)),
                _layer_spec((1, GLA_DV), i)]
    args = [proj, gla_s, w["gla_norm"]]
    in_specs.append(pl.BlockSpec(memory_space=pl.ANY))
    args.append(prev_gla)
    aliases = {len(args) - 1: 1}
    return pl.pallas_call(
        functools.partial(_gla_mixer_kernel, T, G),
        grid=(B // G,),
        in_specs=in_specs,
        out_specs=(pl.BlockSpec((G, T, D_MODEL), lambda b: (b, 0, 0)),
                   pl.BlockSpec(state, lambda b: (i, b, 0, 0, 0))),
        out_shape=(jax.ShapeDtypeStruct((B, T, D_MODEL), F32),
                   jax.ShapeDtypeStruct((n_odd, B, GLA_HEADS, GLA_DK, GLA_DV), F32)),
        input_output_aliases=aliases,
        compiler_params=_params("parallel"),
        name="gla_mixer",
    )(*args)


def _fused_layer_kernel(even, final, chunk, rows, tiles_per_seq, n_tiles, slot, has_prev, *refs):
    it = iter(refs)
    xlead_ref, xlag_ref, gmix_ref, win_ref = (next(it) for _ in range(4))
    if even:
        cos_ref, sin_ref = next(it), next(it)
        mixer_weights = tuple(next(it) for _ in range(7))
    else:
        wg1_ref, wg2_ref, bg_ref, nw_ref = (next(it) for _ in range(4))
    wo_ref, gmlp_ref, wu_ref, wd_ref, gf_ref = (next(it) for _ in range(5))
    if has_prev:
        next(it)
    out_ref, snew_ref = next(it), next(it)
    if even:
        hnew_ref, cnew_ref = next(it), next(it)
    if not has_prev:
        zeros_ref = next(it)
    proj_scr = (next(it), next(it))
    mix_scr = (next(it), next(it))
    x1_scr, u_scr, s_scr = next(it), next(it), next(it)
    if even:
        h_scr, tail_scr = next(it), next(it)

    s = pl.program_id(0)
    tile_in_seq = lax.rem(jnp.clip(s - 1, 0, n_tiles - 1), tiles_per_seq)

    @pl.when(s == 0)
    def _():
        proj_scr[1][...] = jnp.zeros(proj_scr[1].shape, F32)
        mix_scr[0][...] = jnp.zeros(mix_scr[0].shape, BF16)

    @pl.when((s == 0) | (lax.rem(s + tiles_per_seq - 1, tiles_per_seq) == 0))
    def _():
        s_scr[...] = jnp.zeros(s_scr.shape, F32)
        if even:
            h_scr[...] = jnp.zeros(h_scr.shape, F32)
            tail_scr[...] = jnp.zeros(tail_scr.shape, F32)

    main_cols = EVEN_IN if even else ODD_MAIN
    macs_per_row = D_MODEL * (main_cols + D_MODEL + 2 * D_FF)

    def matrix_stream(proj_a, mix_c, pace):
        P = MXU_PIECE
        h_in = _rmsnorm(xlead_ref[...], gmix_ref[...]).astype(BF16)
        m = mix_c[...]
        for n0 in range(0, D_MODEL, P):
            x1 = xlag_ref[:, n0:n0 + P] + _dot(m, wo_ref[:, n0:n0 + P])
            x1_scr[:, n0:n0 + P] = x1
            pace.after(x1)
            yield D_MODEL * P / macs_per_row
        h_mlp = _rmsnorm(x1_scr[...], gmlp_ref[...]).astype(BF16)
        if not even:
            glr = _dot(h_in, wg1_ref[...]).astype(BF16)
        for n0 in range(0, main_cols, P):
            p = _dot(h_in, win_ref[:, n0:n0 + P])
            proj_a[:, n0:n0 + P] = p
            pace.after(p)
            if not even and n0 == P:
                z = _dot(glr, wg2_ref[...]) + bg_ref[...]
            yield D_MODEL * P / macs_per_row
        if not even:
            proj_a[:, ODD_MAIN:] = _scaled_log_sigmoid(z)
            yield 0.0
        for n0 in range(0, D_FF, P):
            u = jnp.square(jnp.maximum(_dot(h_mlp, wu_ref[:, n0:n0 + P]), 0.0))
            u_scr[:, n0:n0 + P] = u.astype(BF16)
            pace.after(u)
            yield D_MODEL * P / macs_per_row
        for n0 in range(0, D_MODEL, P):
            x2 = x1_scr[:, n0:n0 + P] + _dot(u_scr[...], wd_ref[:, n0:n0 + P])
            if final:
                x1_scr[:, n0:n0 + P] = x2
            else:
                out_ref[:, n0:n0 + P] = x2
            pace.after(x2)
            yield D_FF * P / macs_per_row
        if final:
            out_ref[...] = _rmsnorm(x1_scr[...], gf_ref[...])

    def step(par):
        proj_a, proj_b = proj_scr[par], proj_scr[1 - par]
        mix_b, mix_c = mix_scr[1 - par], mix_scr[par]

        def load(r, c):
            return proj_b[r, c]

        def store(r, c, val):
            mix_b[r, c] = val.astype(BF16)

        pace = _Pace()
        if even:
            vector_stream = _even_mixer_tile(chunk, rows // chunk, tile_in_seq * rows, load,
                                             store, cos_ref, sin_ref, s_scr, s_scr, h_scr, h_scr,
                                             tail_scr, mixer_weights, pace)
        else:
            vector_stream = _gla_mixer_tile(chunk, rows // chunk, load, store, s_scr, s_scr,
                                            nw_ref[...], pace)
        _interleave(matrix_stream(proj_a, mix_c, pace), vector_stream)

    parity = lax.rem(s, 2)
    pl.when(parity == 0)(functools.partial(step, 0))
    pl.when(parity == 1)(functools.partial(step, 1))

    if not has_prev:
        zeros_ref[...] = jnp.zeros(zeros_ref.shape, F32)

    @pl.when((s >= 1) & (s <= n_tiles) & (lax.rem(s, tiles_per_seq) == 0))
    def _():
        if has_prev:
            snew_ref[0] = s_scr[...]
        else:
            for other in range(snew_ref.shape[0]):
                snew_ref[other, 0] = s_scr[...] if other == slot else jnp.zeros(s_scr.shape, F32)
        if even:
            hnew_ref[0] = h_scr[...]
            cnew_ref[0] = tail_scr[SUBLANES - (CONV_W - 1):SUBLANES, :]


def _fused_layer(x, w, layer, cos, sin, prev_state, seq_len, n_short):
    n_rows = x.shape[0]
    B = n_rows // seq_len
    R = FUSED_ROWS
    even = layer % 2 == 0
    final = layer == DEPTH - 1
    i = layer // 2
    C = math.gcd(seq_len, RET_CHUNK if even else GLA_CHUNK)
    tps = seq_len // R
    NT = n_rows // R
    has_prev = prev_state is not None

    def lead(s):
        return (jnp.minimum(s, NT - 1), 0)

    def mid_tile(s):
        return jnp.clip(s - 1, 0, NT - 1)

    def lag(s):
        return (jnp.clip(s - 2, 0, NT - 1), 0)

    def seq3(s):
        return (mid_tile(s) // tps, 0, 0)

    in_specs = [pl.BlockSpec((R, D_MODEL), lead),
                pl.BlockSpec((R, D_MODEL), lag),
                _layer_spec((1, D_MODEL), layer)]
    args = [x, x, w["norm_mix"]]
    if even:
        in_specs += [_layer_spec((D_MODEL, EVEN_IN), i),
                     pl.BlockSpec((R, RET_KEY), lambda s: (mid_tile(s) % tps, 0)),
                     pl.BlockSpec((R, RET_KEY), lambda s: (mid_tile(s) % tps, 0)),
                     _layer_spec((CONV_W, LRU_WIDTH), i),
                     _layer_spec((1, LRU_WIDTH), i),
                     _layer_spec((LRU_WIDTH, LRU_WIDTH), i),
                     _layer_spec((1, LRU_WIDTH), i),
                     _layer_spec((LRU_WIDTH, LRU_WIDTH), i),
                     _layer_spec((1, LRU_WIDTH), i),
                     _layer_spec((1, LRU_WIDTH), i)]
        args += [w["w_in_even"], cos, sin, w["conv_w"], w["conv_b"], w["wa"], w["ba"],
                 w["wi"], w["bi"], w["lam"]]
        w_out = w["w_out_even"]
        state_shape = (RET_HEADS, RET_DK, RET_DV)
        n_stack = w["w_in_even"].shape[0]
        proj_cols = EVEN_IN
    else:
        in_specs += [_layer_spec((D_MODEL, ODD_MAIN), i),
                     _layer_spec((D_MODEL, LANES), i),
                     _layer_spec((LANES, GLA_KEY), i),
                     _layer_spec((1, GLA_KEY), i),
                     _layer_spec((1, GLA_DV), i)]
        args += [w["w_in_odd"], w["wg1"], w["wg2"], w["bg"], w["gla_norm"]]
        w_out = w["w_out_odd"]
        state_shape = (GLA_HEADS, GLA_DK, GLA_DV)
        n_stack = w["w_in_odd"].shape[0]
        proj_cols = ODD_OUT
    in_specs += [_layer_spec((D_MODEL, D_MODEL), i),
                 _layer_spec((1, D_MODEL), layer),
                 _layer_spec((D_MODEL, D_FF), layer),
                 _layer_spec((D_FF, D_MODEL), layer),
                 _const_spec((1, D_MODEL))]
    args += [w_out, w["norm_mlp"], w["w_up"], w["w_down"], w["norm_final"]]
    aliases = {}
    if has_prev:
        in_specs.append(pl.BlockSpec(memory_space=pl.ANY))
        args.append(prev_state)
        aliases = {len(args) - 1: 1}

    if has_prev:
        state_spec = pl.BlockSpec((None, 1) + state_shape,
                                  lambda s: (i, mid_tile(s) // tps, 0, 0, 0))
    else:
        state_spec = pl.BlockSpec((n_stack, 1) + state_shape,
                                  lambda s: (0, mid_tile(s) // tps, 0, 0, 0))
    out_specs = [pl.BlockSpec((R, D_MODEL), lag), state_spec]
    out_shapes = [jax.ShapeDtypeStruct((n_rows, D_MODEL), F32),
                  jax.ShapeDtypeStruct((n_stack, B) + state_shape, F32)]
    scratch = [pltpu.VMEM((R, proj_cols), F32), pltpu.VMEM((R, proj_cols), F32),
               pltpu.VMEM((R, D_MODEL), BF16), pltpu.VMEM((R, D_MODEL), BF16),
               pltpu.VMEM((R, D_MODEL), F32), pltpu.VMEM((R, D_FF), BF16),
               pltpu.VMEM(state_shape, F32)]
    if even:
        out_specs += [pl.BlockSpec((1, 1, LRU_WIDTH), seq3),
                      pl.BlockSpec((1, CONV_W - 1, LRU_WIDTH), seq3)]
        out_shapes += [jax.ShapeDtypeStruct((B, 1, LRU_WIDTH), F32),
                       jax.ShapeDtypeStruct((B, CONV_W - 1, LRU_WIDTH), F32)]
        scratch += [pltpu.VMEM((1, LRU_WIDTH), F32),
                    pltpu.VMEM((SUBLANES, LRU_WIDTH), F32)]
    if not has_prev:
        per_step = n_short // NT
        assert per_step * NT == n_short
        out_specs.append(pl.BlockSpec((n_stack, per_step) + state_shape,
                                      lambda s: (0, jnp.minimum(s, NT - 1), 0, 0, 0)))
        out_shapes.append(jax.ShapeDtypeStruct((n_stack, n_short) + state_shape, F32))
    return pl.pallas_call(
        functools.partial(_fused_layer_kernel, even, final, C, R, tps, NT, i, has_prev),
        grid=(NT + 2,),
        in_specs=in_specs,
        out_specs=tuple(out_specs),
        out_shape=tuple(out_shapes),
        scratch_shapes=scratch,
        input_output_aliases=aliases,
        compiler_params=_params("arbitrary"),
        name="layer_even" if even else "layer_odd",
    )(*args)


def _rope_tables(pos0, T):
    half = RET_DK // 2
    inv = ROPE_BASE ** (-jnp.arange(half, dtype=F32) / half)
    ang = (pos0 + jnp.arange(T, dtype=jnp.int32)).astype(F32)[:, None] * inv[None, :]
    cos = jnp.cos(ang)
    sin = jnp.sin(ang)
    cos_full = jnp.tile(jnp.concatenate([cos, cos], axis=-1), (1, RET_HEADS))
    sin_signed = jnp.tile(jnp.concatenate([-sin, sin], axis=-1), (1, RET_HEADS))
    return cos_full, sin_signed


def _block_diag(w):
    nl, nb, c, d = w.shape
    eye = jnp.eye(nb, dtype=w.dtype)
    return (eye[None, :, None, :, None] * w[:, :, :, None, :]).reshape(nl, nb * c, nb * d)


def _prepare_weights(norm_mix, norm_mlp, norm_final, w_in_even, w_out_even, conv_w, conv_b,
                     lru_w_a, lru_b_a, lru_w_i, lru_b_i, lru_lambda,
                     w_in_odd, gla_w_gate2, gla_b_gate, gla_norm, w_out_odd, w_up, w_down):
    n_even = w_in_even.shape[0]
    n_odd = w_in_odd.shape[0]
    pad1 = LANES - GLA_RANK
    return dict(
        norm_mix=norm_mix.reshape(DEPTH, 1, D_MODEL),
        norm_mlp=norm_mlp.reshape(DEPTH, 1, D_MODEL),
        norm_final=norm_final.reshape(1, D_MODEL),
        w_in_even=w_in_even.astype(BF16),
        w_out_even=w_out_even.astype(BF16),
        conv_w=conv_w,
        conv_b=conv_b.reshape(n_even, 1, LRU_WIDTH),
        wa=_block_diag(lru_w_a).astype(BF16),
        ba=lru_b_a.reshape(n_even, 1, LRU_WIDTH),
        wi=_block_diag(lru_w_i).astype(BF16),
        bi=lru_b_i.reshape(n_even, 1, LRU_WIDTH),
        lam=lru_lambda.reshape(n_even, 1, LRU_WIDTH),
        w_in_odd=w_in_odd.astype(BF16),
        wg1=jnp.pad(w_in_odd[:, :, ODD_MAIN:], ((0, 0), (0, 0), (0, pad1))).astype(BF16),
        wg2=jnp.pad(gla_w_gate2, ((0, 0), (0, pad1), (0, 0))).astype(BF16),
        bg=gla_b_gate.reshape(n_odd, 1, GLA_KEY),
        gla_norm=gla_norm.reshape(n_odd, 1, GLA_DV),
        w_out_odd=w_out_odd.astype(BF16),
        w_up=w_up.astype(BF16),
        w_down=w_down.astype(BF16),
    )


def _prompt_trunk(x, w, n_short):
    B, T, _ = x.shape
    xf = x.reshape(B * T, D_MODEL)
    cos, sin = _rope_tables(0, T)
    ret_new, gla_new = None, None
    lrus, convs, zeros = [], [], []
    for layer in range(DEPTH):
        prev = ret_new if layer % 2 == 0 else gla_new
        outs = _fused_layer(xf, w, layer, cos, sin, prev, T, n_short)
        if prev is None:
            zeros.append(outs[-1])
            outs = outs[:-1]
        if layer % 2 == 0:
            xf, ret_new, lh, cb = outs
            lrus.append(lh.reshape(B, LRU_WIDTH))
            convs.append(cb)
        else:
            xf, gla_new = outs
    states = (ret_new, jnp.stack(lrus), jnp.stack(convs), gla_new)
    return xf.reshape(B, T, D_MODEL), states, zeros


def _sample_trunk(x, pos0, ret_s, lru_h, conv_buf, gla_s, ret_new, gla_new, w):
    B, T, _ = x.shape
    xf = x.reshape(B * T, D_MODEL)
    cos, sin = _rope_tables(pos0, T)
    lru_h = lru_h.reshape(lru_h.shape[0], B, 1, LRU_WIDTH)
    lrus, convs = [], []
    for layer in range(DEPTH):
        if layer % 2 == 0:
            proj = _norm_proj_even(xf, w, layer)
            mix, ret_new, lh, cb = _even_mixer(proj.reshape(B, T, EVEN_IN), pos0, cos, sin,
                                               ret_s, lru_h, conv_buf, w, layer, ret_new)
            lrus.append(lh.reshape(B, LRU_WIDTH))
            convs.append(cb)
        else:
            proj = _norm_proj_odd(xf, w, layer)
            mix, gla_new = _gla_mixer(proj.reshape(B, T, ODD_OUT), gla_s, w, layer, gla_new)
        xf = _out_mlp(xf, mix.reshape(B * T, D_MODEL), w, layer)
    return xf.reshape(B, T, D_MODEL), ret_new, jnp.stack(lrus), jnp.stack(convs), gla_new


def kernel(x_prompt, x_sample, state_ret, state_lru, state_conv, state_gla, norm_mix, norm_mlp, norm_final, w_in_even, w_out_even, conv_w, conv_b, lru_w_a, lru_b_a, lru_w_i, lru_b_i, lru_lambda, w_in_odd, gla_w_gate2, gla_b_gate, gla_norm, w_out_odd, w_up, w_down):
    w = _prepare_weights(norm_mix, norm_mlp, norm_final, w_in_even, w_out_even, conv_w, conv_b,
                         lru_w_a, lru_b_a, lru_w_i, lru_b_i, lru_lambda,
                         w_in_odd, gla_w_gate2, gla_b_gate, gla_norm, w_out_odd, w_up, w_down)
    y_p, (ret_p, lru_p, conv_p, gla_p), (ret_zero, gla_zero) = _prompt_trunk(
        x_prompt, w, x_sample.shape[0])
    y_s, ret_s, lru_s, conv_s, gla_s = _sample_trunk(x_sample, PAST_LEN, state_ret, state_lru,
                                                     state_conv, state_gla, ret_zero, gla_zero, w)
    return (y_p, y_s, ret_p, ret_s, lru_p, lru_s, conv_p, conv_s, gla_p, gla_s)
```

```python
import functools
import math

import jax
import jax.numpy as jnp
from jax import lax
from jax.experimental import pallas as pl
from jax.experimental.pallas import tpu as pltpu

F32 = jnp.float32
BF16 = jnp.bfloat16

D_MODEL = 1024
DEPTH = 4
PAST_LEN = 16384
EPS = 1e-6
ROPE_BASE = 10000.0

RET_HEADS = 4
RET_DK = 64
RET_DV = 128
RET_KEY = RET_HEADS * RET_DK
RET_VALUE = RET_HEADS * RET_DV
RET_CHUNK = 128

LRU_WIDTH = 512
CONV_W = 4
LRU_C = 8.0

GLA_HEADS = 4
GLA_KEY = 512
GLA_VALUE = 1024
GLA_DK = GLA_KEY // GLA_HEADS
GLA_DV = GLA_VALUE // GLA_HEADS
GLA_RANK = 16
GLA_TAU = 16.0
GLA_CHUNK = 64

D_FF = 4 * D_MODEL
FF_CHUNK = 1024

EVEN_IN = 2 * RET_KEY + 2 * RET_VALUE + 2 * LRU_WIDTH
ODD_MAIN = 2 * GLA_KEY + 2 * GLA_VALUE
ODD_OUT = ODD_MAIN + GLA_KEY

LANES = 128
SUBLANES = 8
ROW_TILE = 512
FUSED_ROWS = 256
MXU_PIECE = 256
SAMPLE_SEQS = 8
VMEM_LIMIT = 56 * 1024 * 1024


def _dot(a, b):
    return jnp.dot(a.astype(BF16), b.astype(BF16), preferred_element_type=F32)


def _dot_nt(a, b):
    return lax.dot_general(a.astype(BF16), b.astype(BF16), (((1,), (1,)), ((), ())),
                           preferred_element_type=F32)


def _dot_tn(a, b):
    return lax.dot_general(a.astype(BF16), b.astype(BF16), (((0,), (0,)), ((), ())),
                           preferred_element_type=F32)


def _rmsnorm(x, g):
    return x * lax.rsqrt(jnp.mean(x * x, axis=-1, keepdims=True) + EPS) * g


def _head_rms(x, eps=EPS):
    return x * lax.rsqrt(jnp.mean(x * x, axis=-1, keepdims=True) + eps)


def _shift_rows(x, d, fill):
    row = lax.broadcasted_iota(jnp.int32, x.shape, 0)
    return jnp.where(row >= d, pltpu.roll(x, d, axis=0), fill)


def _layer_spec(shape, layer):
    n = len(shape)
    return pl.BlockSpec((None,) + tuple(shape), lambda *_: (layer,) + (0,) * n,
                        pipeline_mode=pl.Buffered(1))


def _const_spec(shape):
    n = len(shape)
    return pl.BlockSpec(shape, lambda *_: (0,) * n, pipeline_mode=pl.Buffered(1))


def _params(*semantics):
    return pltpu.CompilerParams(dimension_semantics=semantics, vmem_limit_bytes=VMEM_LIMIT)


def _scaled_log_sigmoid(z):
    return (jnp.minimum(z, 0.0) - jnp.log(1.0 + jnp.exp(-jnp.abs(z)))) / GLA_TAU


def _log_gate(h, wg1_ref, wg2_ref, bg_ref):
    glr = _dot(h, wg1_ref[...])
    return _scaled_log_sigmoid(_dot(glr, wg2_ref[...]) + bg_ref[...])


def _norm_proj_even_kernel(x_ref, g_ref, w_ref, o_ref):
    h = _rmsnorm(x_ref[...], g_ref[...])
    o_ref[...] = _dot(h, w_ref[...])


def _norm_proj_odd_kernel(x_ref, g_ref, w_ref, wg1_ref, wg2_ref, bg_ref, o_ref):
    h = _rmsnorm(x_ref[...], g_ref[...]).astype(BF16)
    o_ref[:, :ODD_MAIN] = _dot(h, w_ref[...])
    o_ref[:, ODD_MAIN:] = _log_gate(h, wg1_ref, wg2_ref, bg_ref)


def _out_mlp_kernel(final, x_ref, m_ref, wo_ref, g_ref, wu_ref, wd_ref, gf_ref, o_ref,
                    h_scr, acc_scr):
    c = pl.program_id(0)

    @pl.when(c == 0)
    def _():
        x1 = x_ref[...] + _dot(m_ref[...], wo_ref[...])
        acc_scr[...] = x1
        h_scr[...] = _rmsnorm(x1, g_ref[...]).astype(BF16)

    u = jnp.square(jnp.maximum(_dot(h_scr[...], wu_ref[...]), 0.0))
    acc_scr[...] += _dot(u, wd_ref[...])

    @pl.when(c == pl.num_programs(0) - 1)
    def _():
        x2 = acc_scr[...]
        o_ref[...] = _rmsnorm(x2, gf_ref[...]) if final else x2


def _norm_proj_even(x, w, layer):
    n = x.shape[0]
    tm = min(ROW_TILE, n)
    return pl.pallas_call(
        _norm_proj_even_kernel,
        grid=(n // tm,),
        in_specs=[pl.BlockSpec((tm, D_MODEL), lambda i: (i, 0)),
                  _layer_spec((1, D_MODEL), layer),
                  _layer_spec((D_MODEL, EVEN_IN), layer // 2)],
        out_specs=pl.BlockSpec((tm, EVEN_IN), lambda i: (i, 0)),
        out_shape=jax.ShapeDtypeStruct((n, EVEN_IN), F32),
        compiler_params=_params("parallel"),
        name="norm_proj_even",
    )(x, w["norm_mix"], w["w_in_even"])


def _norm_proj_odd(x, w, layer):
    n = x.shape[0]
    tm = min(ROW_TILE, n)
    i = layer // 2
    return pl.pallas_call(
        _norm_proj_odd_kernel,
        grid=(n // tm,),
        in_specs=[pl.BlockSpec((tm, D_MODEL), lambda i: (i, 0)),
                  _layer_spec((1, D_MODEL), layer),
                  _layer_spec((D_MODEL, ODD_MAIN), i),
                  _layer_spec((D_MODEL, LANES), i),
                  _layer_spec((LANES, GLA_KEY), i),
                  _layer_spec((1, GLA_KEY), i)],
        out_specs=pl.BlockSpec((tm, ODD_OUT), lambda i: (i, 0)),
        out_shape=jax.ShapeDtypeStruct((n, ODD_OUT), F32),
        compiler_params=_params("parallel"),
        name="norm_proj_odd",
    )(x, w["norm_mix"], w["w_in_odd"], w["wg1"], w["wg2"], w["bg"])


def _out_mlp(x, mix, w, layer):
    n = x.shape[0]
    final = layer == DEPTH - 1
    w_out = w["w_out_even"] if layer % 2 == 0 else w["w_out_odd"]
    return pl.pallas_call(
        functools.partial(_out_mlp_kernel, final),
        grid=(D_FF // FF_CHUNK,),
        in_specs=[_const_spec((n, D_MODEL)),
                  _const_spec((n, D_MODEL)),
                  _layer_spec((D_MODEL, D_MODEL), layer // 2),
                  _layer_spec((1, D_MODEL), layer),
                  pl.BlockSpec((None, D_MODEL, FF_CHUNK), lambda c: (layer, 0, c)),
                  pl.BlockSpec((None, FF_CHUNK, D_MODEL), lambda c: (layer, c, 0)),
                  _const_spec((1, D_MODEL))],
        out_specs=pl.BlockSpec((n, D_MODEL), lambda c: (0, 0)),
        out_shape=jax.ShapeDtypeStruct((n, D_MODEL), F32),
        scratch_shapes=[pltpu.VMEM((n, D_MODEL), BF16), pltpu.VMEM((n, D_MODEL), F32)],
        compiler_params=_params("arbitrary"),
        name="out_mlp_final" if final else "out_mlp",
    )(x, mix, w_out, w["norm_mlp"], w["w_up"], w["w_down"], w["norm_final"])


def _run(stream):
    for _ in stream:
        pass


def _interleave(*streams):
    done = [0.0] * len(streams)
    alive = list(range(len(streams)))
    while alive:
        i = min(alive, key=lambda j: done[j])
        try:
            done[i] += next(streams[i])
        except StopIteration:
            alive.remove(i)


class _Pace:
    def __init__(self):
        self.zero = None

    def after(self, value):
        t = value[0:1, 0:1]
        self.zero = jnp.where(t == t, 0.0, t)

    def __call__(self, x):
        return x if self.zero is None else x + self.zero


def _no_pace(x):
    return x


def _even_mixer_tile(C, chunks, pos_base, load, store, cos_ref, sin_ref,
                     s_src, s_dst, h_src, h_dst, tail_ref, weights, pace=_no_pace):
    cw_ref, cbias_ref, wa_ref, ba_ref, wi_ref, bi_ref, lam_ref = weights
    lane = lax.broadcasted_iota(jnp.int32, (C, RET_KEY), 1)
    first_half = (lane % RET_DK) < (RET_DK // 2)
    ri = lax.broadcasted_iota(jnp.int32, (C, C), 0)
    ci = lax.broadcasted_iota(jnp.int32, (C, C), 1)
    rel = (ri - ci).astype(F32)
    row = lax.broadcasted_iota(jnp.int32, (C, 1), 0).astype(F32)
    trow = lax.broadcasted_iota(jnp.int32, (C, LRU_WIDTH), 0)
    row8 = lax.broadcasted_iota(jnp.int32, (SUBLANES, LRU_WIDTH), 0)
    lam = lam_ref[...]
    softplus_neg_lam = jnp.maximum(-lam, 0.0) + jnp.log(1.0 + jnp.exp(-jnp.abs(lam)))
    log_decay = [math.log1p(-(2.0 ** (-5.0 - h))) for h in range(RET_HEADS)]
    decay_mask = [jnp.where(rel >= 0, jnp.exp(jnp.maximum(rel, 0.0) * lg), 0.0)
                  for lg in log_decay]
    q_decay = [jnp.exp((row + 1.0) * lg) for lg in log_decay]
    k_decay = [jnp.exp((C - 1.0 - row) * lg) for lg in log_decay]
    n_scan = max(1, int(math.log2(C)))

    v0 = 2 * RET_KEY
    g0 = v0 + RET_VALUE
    x0 = g0 + RET_VALUE
    xg0 = x0 + LRU_WIDTH
    heads = range(RET_HEADS)
    J = range(chunks)
    rows = [slice(j * C, (j + 1) * C) for j in J]

    def delayed(x, t, n):
        rolled = pltpu.roll(x, n, axis=0)
        head = jnp.where(row8 < n, pltpu.roll(t, n, axis=0), rolled[0:SUBLANES, :])
        if C == SUBLANES:
            return head
        return jnp.concatenate([head, rolled[SUBLANES:, :]], axis=0)

    qh, kh, q_dec, k_dec, vh, xc = ([None] * chunks for _ in range(6))
    tail = tail_ref[...]
    for j in J:
        cos = cos_ref[rows[j], :]
        sin = sin_ref[rows[j], :]

        def rope(x):
            partner = jnp.where(first_half,
                                pltpu.roll(x, RET_KEY - RET_DK // 2, axis=1),
                                pltpu.roll(x, RET_DK // 2, axis=1))
            return x * cos + partner * sin

        q = rope(load(rows[j], slice(0, RET_KEY)))
        k = rope(load(rows[j], slice(RET_KEY, 2 * RET_KEY))) * (RET_DK ** -0.5)
        q = [q[:, h * RET_DK:(h + 1) * RET_DK] for h in heads]
        k = [k[:, h * RET_DK:(h + 1) * RET_DK] for h in heads]
        q_dec[j] = [(q[h] * q_decay[h]).astype(BF16) for h in heads]
        k_dec[j] = [(k[h] * k_decay[h]).astype(BF16) for h in heads]
        qh[j] = [q[h].astype(BF16) for h in heads]
        kh[j] = [k[h].astype(BF16) for h in heads]
        vh[j] = [load(rows[j], slice(v0 + h * RET_DV, v0 + (h + 1) * RET_DV)).astype(BF16)
                 for h in heads]
        yield 0.08 / chunks

        x_new = load(rows[j], slice(x0, x0 + LRU_WIDTH))
        acc = pace(cbias_ref[...])
        for i in range(CONV_W - 1):
            acc = acc + delayed(x_new, tail, CONV_W - 1 - i) * cw_ref[i:i + 1, :]
        xc[j] = acc + x_new * cw_ref[CONV_W - 1:CONV_W, :]
        tail = x_new[C - SUBLANES:C, :]
        yield 0.10 / chunks
    tail_ref[...] = tail

    scores = [[_dot_nt(qh[j][h], kh[j][h]) for h in heads] for j in J]
    kv = [[_dot_tn(k_dec[j][h], vh[j][h]) for h in heads] for j in J]
    r_pre = [_dot(xc[j], wa_ref[...]) for j in J]
    i_pre = [_dot(xc[j], wi_ref[...]) for j in J]
    yield 0.0

    probs, s_in = [None] * chunks, [None] * chunks
    s_cur = [s_src[h] for h in heads]
    for j in J:
        probs[j] = [(scores[j][h] * decay_mask[h]).astype(BF16) for h in heads]
        s_in[j] = [s_cur[h].astype(BF16) for h in heads]
        s_cur = [s_cur[h] * math.exp(C * log_decay[h]) + kv[j][h] for h in heads]
        yield 0.06 / chunks
    for h in heads:
        s_dst[h] = s_cur[h]

    a_cum, b_loc = [None] * chunks, [None] * chunks
    for j in J:
        r = jax.nn.sigmoid(r_pre[j] + pace(ba_ref[...]))
        gate_i = jax.nn.sigmoid(i_pre[j] + bi_ref[...])
        log_a = -LRU_C * r * softplus_neg_lam
        a = jnp.exp(log_a)
        y = -jnp.tanh(log_a) * (a * a + 1.0)
        mult = jnp.where(y > 0.0, y * lax.rsqrt(y), 0.0)
        pos = trow + (pos_base + j * C)
        mult = jnp.where(pos == 0, 1.0, mult)
        b = mult * (gate_i * xc[j])
        yield 0.20 / chunks
        d = 1
        while d < C:
            b = a * _shift_rows(b, d, pace(0.0)) + b
            a = a * _shift_rows(a, d, pace(1.0))
            d *= 2
            yield 0.36 / (chunks * n_scan)
        a_cum[j], b_loc[j] = a, b

    out = [[_dot(probs[j][h], vh[j][h]) + _dot(q_dec[j][h], s_in[j][h]) for h in heads]
           for j in J]
    yield 0.0

    carry = h_src[...]
    for j in J:
        hidden = b_loc[j] + a_cum[j] * carry
        carry = hidden[C - 1:C, :]
        xg = load(rows[j], slice(xg0, xg0 + LRU_WIDTH))
        store(rows[j], slice(RET_VALUE, RET_VALUE + LRU_WIDTH), hidden * jax.nn.gelu(xg))
        yield 0.08 / chunks
        for h in heads:
            gh = load(rows[j], slice(g0 + h * RET_DV, g0 + (h + 1) * RET_DV))
            store(rows[j], slice(h * RET_DV, (h + 1) * RET_DV),
                  jax.nn.silu(gh) * _head_rms(out[j][h], pace(EPS)))
            yield 0.03 / chunks
    h_dst[...] = carry


def _gla_mixer_tile(C, chunks, load, store, s_src, s_dst, nw, pace=_no_pace):
    k0 = GLA_KEY
    v0 = 2 * GLA_KEY
    r0 = v0 + GLA_VALUE
    la0 = r0 + GLA_VALUE
    ri = lax.broadcasted_iota(jnp.int32, (C, C), 0)
    ci = lax.broadcasted_iota(jnp.int32, (C, C), 1)
    causal = ri >= ci
    di = lax.broadcasted_iota(jnp.int32, (GLA_DK, GLA_DK), 0)
    dj = lax.broadcasted_iota(jnp.int32, (GLA_DK, GLA_DK), 1)
    eye = di == dj
    heads = range(GLA_HEADS)
    key = [slice(h * GLA_DK, (h + 1) * GLA_DK) for h in heads]
    J = range(chunks)
    rows = [slice(j * C, (j + 1) * C) for j in J]

    q_in, k_in, q_st, k_st, decay_col, vh = ([None] * chunks for _ in range(6))
    for j in J:
        b = load(rows[j], slice(la0, la0 + GLA_KEY))
        d = 1
        while d < C:
            b = b + _shift_rows(b, d, pace(0.0))
            d *= 2
        yield 0.12 / chunks
        b_mid = pace(b[C // 2:C // 2 + 1, :])
        b_last = b[C - 1:C, :]
        q = load(rows[j], slice(0, GLA_KEY)) * (GLA_DK ** -0.5)
        k = load(rows[j], slice(k0, k0 + GLA_KEY))
        q_in[j] = (q * jnp.exp(b - b_mid)).astype(BF16)
        k_in[j] = (k * jnp.exp(b_mid - b)).astype(BF16)
        q_st[j] = (q * jnp.exp(b)).astype(BF16)
        k_st[j] = (k * jnp.exp(b_last - b)).astype(BF16)
        s_decay = jnp.exp(b_last)
        decay_col[j] = [jnp.sum(jnp.where(eye, s_decay[:, key[h]], 0.0), axis=1, keepdims=True)
                        for h in heads]
        vh[j] = [load(rows[j], slice(v0 + h * GLA_DV, v0 + (h + 1) * GLA_DV)).astype(BF16)
                 for h in heads]
        yield 0.28 / chunks

    scores = [[_dot_nt(q_in[j][:, key[h]], k_in[j][:, key[h]]) for h in heads] for j in J]
    kv = [[_dot_tn(k_st[j][:, key[h]], vh[j][h]) for h in heads] for j in J]
    yield 0.0

    probs, s_in = [None] * chunks, [None] * chunks
    s_cur = [s_src[h] for h in heads]
    for j in J:
        probs[j] = [jnp.where(causal, scores[j][h], 0.0).astype(BF16) for h in heads]
        s_in[j] = [s_cur[h].astype(BF16) for h in heads]
        s_cur = [s_cur[h] * decay_col[j][h] + kv[j][h] for h in heads]
        yield 0.16 / chunks
    for h in heads:
        s_dst[h] = s_cur[h]

    out = [[_dot(probs[j][h], vh[j][h]) + _dot(q_st[j][:, key[h]], s_in[j][h]) for h in heads]
           for j in J]
    yield 0.0

    for j in J:
        for h in heads:
            rh = load(rows[j], slice(r0 + h * GLA_DV, r0 + (h + 1) * GLA_DV))
            store(rows[j], slice(h * GLA_DV, (h + 1) * GLA_DV),
                  jax.nn.silu(rh) * (_head_rms(out[j][h], pace(EPS)) * nw))
            yield 0.11 / chunks


def _even_mixer_kernel(pos0, chunk, seqs, *refs):
    (proj_ref, cos_ref, sin_ref, s0_ref, h0_ref, cb_ref,
     cw_ref, cbias_ref, wa_ref, ba_ref, wi_ref, bi_ref, lam_ref) = refs[:13]
    refs = refs[14:]
    mix_ref, snew_ref, hnew_ref, cnew_ref, tail_scr = refs
    weights = (cw_ref, cbias_ref, wa_ref, ba_ref, wi_ref, bi_ref, lam_ref)
    for g in range(seqs):
        tail = tail_scr.at[g]
        tail[...] = jnp.zeros((SUBLANES, LRU_WIDTH), F32)
        tail[SUBLANES - (CONV_W - 1):SUBLANES, :] = cb_ref[g]

        def load(rows, cols, g=g):
            return proj_ref[g, rows, cols]

        def store(rows, cols, val, g=g):
            mix_ref[g, rows, cols] = val

        _run(_even_mixer_tile(chunk, 1, pos0, load, store, cos_ref, sin_ref,
                              s0_ref.at[g], snew_ref.at[g], h0_ref.at[g], hnew_ref.at[g],
                              tail, weights))
        cnew_ref[g] = tail[SUBLANES - (CONV_W - 1):SUBLANES, :]


def _even_mixer(proj, pos0, cos, sin, ret_s, lru_h, conv_buf, w, layer, prev_ret):
    B, T, _ = proj.shape
    i = layer // 2
    n_even = w["w_in_even"].shape[0]
    G = SAMPLE_SEQS
    ret_block = (None, G, RET_HEADS, RET_DK, RET_DV)
    in_specs = [pl.BlockSpec((G, T, EVEN_IN), lambda b: (b, 0, 0)),
                _const_spec((T, RET_KEY)),
                _const_spec((T, RET_KEY)),
                pl.BlockSpec(ret_block, lambda b: (i, b, 0, 0, 0)),
                pl.BlockSpec((None, G, 1, LRU_WIDTH), lambda b: (i, b, 0, 0)),
                pl.BlockSpec((None, G, CONV_W - 1, LRU_WIDTH), lambda b: (i, b, 0, 0)),
                _layer_spec((CONV_W, LRU_WIDTH), i),
                _layer_spec((1, LRU_WIDTH), i),
                _layer_spec((LRU_WIDTH, LRU_WIDTH), i),
                _layer_spec((1, LRU_WIDTH), i),
                _layer_spec((LRU_WIDTH, LRU_WIDTH), i),
                _layer_spec((1, LRU_WIDTH), i),
                _layer_spec((1, LRU_WIDTH), i)]
    args = [proj, cos, sin, ret_s, lru_h, conv_buf, w["conv_w"], w["conv_b"],
            w["wa"], w["ba"], w["wi"], w["bi"], w["lam"]]
    in_specs.append(pl.BlockSpec(memory_space=pl.ANY))
    args.append(prev_ret)
    aliases = {len(args) - 1: 1}
    out_shapes = (jax.ShapeDtypeStruct((B, T, D_MODEL), F32),
                  jax.ShapeDtypeStruct((n_even, B, RET_HEADS, RET_DK, RET_DV), F32),
                  jax.ShapeDtypeStruct((B, 1, LRU_WIDTH), F32),
                  jax.ShapeDtypeStruct((B, CONV_W - 1, LRU_WIDTH), F32))
    return pl.pallas_call(
        functools.partial(_even_mixer_kernel, pos0, T, G),
        grid=(B // G,),
        in_specs=in_specs,
        out_specs=(pl.BlockSpec((G, T, D_MODEL), lambda b: (b, 0, 0)),
                   pl.BlockSpec(ret_block, lambda b: (i, b, 0, 0, 0)),
                   pl.BlockSpec((G, 1, LRU_WIDTH), lambda b: (b, 0, 0)),
                   pl.BlockSpec((G, CONV_W - 1, LRU_WIDTH), lambda b: (b, 0, 0))),
        out_shape=out_shapes,
        scratch_shapes=[pltpu.VMEM((G, SUBLANES, LRU_WIDTH), F32)],
        input_output_aliases=aliases,
        compiler_params=_params("parallel"),
        name="even_mixer",
    )(*args)


def _gla_mixer_kernel(chunk, seqs, *refs):
    proj_ref, s0_ref, nw_ref = refs[:3]
    refs = refs[4:]
    mix_ref, snew_ref = refs
    nw = nw_ref[...]
    for g in range(seqs):
        def load(rows, cols, g=g):
            return proj_ref[g, rows, cols]

        def store(rows, cols, val, g=g):
            mix_ref[g, rows, cols] = val

        _run(_gla_mixer_tile(chunk, 1, load, store, s0_ref.at[g], snew_ref.at[g], nw))


def _gla_mixer(proj, gla_s, w, layer, prev_gla):
    B, T, _ = proj.shape
    i = layer // 2
    n_odd = w["w_in_odd"].shape[0]
    G = SAMPLE_SEQS
    state = (None, G, GLA_HEADS, GLA_DK, GLA_DV)
    in_specs = [pl.BlockSpec((G, T, ODD_OUT), lambda b: (b, 0, 0)),
                pl.BlockSpec(state, lambda b: (i, b, 0, 0, 0)),
                _layer_spec((1, GLA_DV), i)]
    args = [proj, gla_s, w["gla_norm"]]
    in_specs.append(pl.BlockSpec(memory_space=pl.ANY))
    args.append(prev_gla)
    aliases = {len(args) - 1: 1}
    return pl.pallas_call(
        functools.partial(_gla_mixer_kernel, T, G),
        grid=(B // G,),
        in_specs=in_specs,
        out_specs=(pl.BlockSpec((G, T, D_MODEL), lambda b: (b, 0, 0)),
                   pl.BlockSpec(state, lambda b: (i, b, 0, 0, 0))),
        out_shape=(jax.ShapeDtypeStruct((B, T, D_MODEL), F32),
                   jax.ShapeDtypeStruct((n_odd, B, GLA_HEADS, GLA_DK, GLA_DV), F32)),
        input_output_aliases=aliases,
        compiler_params=_params("parallel"),
        name="gla_mixer",
    )(*args)


def _fused_layer_kernel(even, final, chunk, rows, tiles_per_seq, n_tiles, slot, has_prev, *refs):
    it = iter(refs)
    xlead_ref, xlag_ref, gmix_ref, win_ref = (next(it) for _ in range(4))
    if even:
        cos_ref, sin_ref = next(it), next(it)
        mixer_weights = tuple(next(it) for _ in range(7))
    else:
        wg1_ref, wg2_ref, bg_ref, nw_ref = (next(it) for _ in range(4))
    wo_ref, gmlp_ref, wu_ref, wd_ref, gf_ref = (next(it) for _ in range(5))
    if has_prev:
        next(it)
    out_ref, snew_ref = next(it), next(it)
    if even:
        hnew_ref, cnew_ref = next(it), next(it)
    if not has_prev:
        zeros_ref = next(it)
    proj_scr = (next(it), next(it))
    mix_scr = (next(it), next(it))
    x1_scr, u_scr, s_scr = next(it), next(it), next(it)
    if even:
        h_scr, tail_scr = next(it), next(it)

    s = pl.program_id(0)
    tile_in_seq = lax.rem(jnp.clip(s - 1, 0, n_tiles - 1), tiles_per_seq)

    @pl.when(s == 0)
    def _():
        proj_scr[1][...] = jnp.zeros(proj_scr[1].shape, F32)
        mix_scr[0][...] = jnp.zeros(mix_scr[0].shape, BF16)

    @pl.when((s == 0) | (lax.rem(s + tiles_per_seq - 1, tiles_per_seq) == 0))
    def _():
        s_scr[...] = jnp.zeros(s_scr.shape, F32)
        if even:
            h_scr[...] = jnp.zeros(h_scr.shape, F32)
            tail_scr[...] = jnp.zeros(tail_scr.shape, F32)

    main_cols = EVEN_IN if even else ODD_MAIN
    macs_per_row = D_MODEL * (main_cols + D_MODEL + 2 * D_FF)

    def matrix_stream(proj_a, mix_c, pace):
        P = MXU_PIECE
        h_in = _rmsnorm(xlead_ref[...], gmix_ref[...]).astype(BF16)
        m = mix_c[...]
        for n0 in range(0, D_MODEL, P):
            x1 = xlag_ref[:, n0:n0 + P] + _dot(m, wo_ref[:, n0:n0 + P])
            x1_scr[:, n0:n0 + P] = x1
            pace.after(x1)
            yield D_MODEL * P / macs_per_row
        h_mlp = _rmsnorm(x1_scr[...], gmlp_ref[...]).astype(BF16)
        if not even:
            glr = _dot(h_in, wg1_ref[...]).astype(BF16)
        for n0 in range(0, main_cols, P):
            p = _dot(h_in, win_ref[:, n0:n0 + P])
            proj_a[:, n0:n0 + P] = p
            pace.after(p)
            if not even and n0 == P:
                z = _dot(glr, wg2_ref[...]) + bg_ref[...]
            yield D_MODEL * P / macs_per_row
        if not even:
            proj_a[:, ODD_MAIN:] = _scaled_log_sigmoid(z)
            yield 0.0
        for n0 in range(0, D_FF, P):
            u = jnp.square(jnp.maximum(_dot(h_mlp, wu_ref[:, n0:n0 + P]), 0.0))
            u_scr[:, n0:n0 + P] = u.astype(BF16)
            pace.after(u)
            yield D_MODEL * P / macs_per_row
        for n0 in range(0, D_MODEL, P):
            x2 = x1_scr[:, n0:n0 + P] + _dot(u_scr[...], wd_ref[:, n0:n0 + P])
            if final:
                x1_scr[:, n0:n0 + P] = x2
            else:
                out_ref[:, n0:n0 + P] = x2
            pace.after(x2)
            yield D_FF * P / macs_per_row
        if final:
            out_ref[...] = _rmsnorm(x1_scr[...], gf_ref[...])

    def step(par):
        proj_a, proj_b = proj_scr[par], proj_scr[1 - par]
        mix_b, mix_c = mix_scr[1 - par], mix_scr[par]

        def load(r, c):
            return proj_b[r, c]

        def store(r, c, val):
            mix_b[r, c] = val.astype(BF16)

        pace = _Pace()
        if even:
            vector_stream = _even_mixer_tile(chunk, rows // chunk, tile_in_seq * rows, load,
                                             store, cos_ref, sin_ref, s_scr, s_scr, h_scr, h_scr,
                                             tail_scr, mixer_weights, pace)
        else:
            vector_stream = _gla_mixer_tile(chunk, rows // chunk, load, store, s_scr, s_scr,
                                            nw_ref[...], pace)
        _interleave(matrix_stream(proj_a, mix_c, pace), vector_stream)

    parity = lax.rem(s, 2)
    pl.when(parity == 0)(functools.partial(step, 0))
    pl.when(parity == 1)(functools.partial(step, 1))

    if not has_prev:
        zeros_ref[...] = jnp.zeros(zeros_ref.shape, F32)

    @pl.when((s >= 1) & (s <= n_tiles) & (lax.rem(s, tiles_per_seq) == 0))
    def _():
        if has_prev:
            snew_ref[0] = s_scr[...]
        else:
            for other in range(snew_ref.shape[0]):
                snew_ref[other, 0] = s_scr[...] if other == slot else jnp.zeros(s_scr.shape, F32)
        if even:
            hnew_ref[0] = h_scr[...]
            cnew_ref[0] = tail_scr[SUBLANES - (CONV_W - 1):SUBLANES, :]


def _fused_layer(x, w, layer, cos, sin, prev_state, seq_len, n_short):
    n_rows = x.shape[0]
    B = n_rows // seq_len
    R = FUSED_ROWS
    even = layer % 2 == 0
    final = layer == DEPTH - 1
    i = layer // 2
    C = math.gcd(seq_len, RET_CHUNK if even else GLA_CHUNK)
    tps = seq_len // R
    NT = n_rows // R
    has_prev = prev_state is not None

    def lead(s):
        return (jnp.minimum(s, NT - 1), 0)

    def mid_tile(s):
        return jnp.clip(s - 1, 0, NT - 1)

    def lag(s):
        return (jnp.clip(s - 2, 0, NT - 1), 0)

    def seq3(s):
        return (mid_tile(s) // tps, 0, 0)

    in_specs = [pl.BlockSpec((R, D_MODEL), lead),
                pl.BlockSpec((R, D_MODEL), lag),
                _layer_spec((1, D_MODEL), layer)]
    args = [x, x, w["norm_mix"]]
    if even:
        in_specs += [_layer_spec((D_MODEL, EVEN_IN), i),
                     pl.BlockSpec((R, RET_KEY), lambda s: (mid_tile(s) % tps, 0)),
                     pl.BlockSpec((R, RET_KEY), lambda s: (mid_tile(s) % tps, 0)),
                     _layer_spec((CONV_W, LRU_WIDTH), i),
                     _layer_spec((1, LRU_WIDTH), i),
                     _layer_spec((LRU_WIDTH, LRU_WIDTH), i),
                     _layer_spec((1, LRU_WIDTH), i),
                     _layer_spec((LRU_WIDTH, LRU_WIDTH), i),
                     _layer_spec((1, LRU_WIDTH), i),
                     _layer_spec((1, LRU_WIDTH), i)]
        args += [w["w_in_even"], cos, sin, w["conv_w"], w["conv_b"], w["wa"], w["ba"],
                 w["wi"], w["bi"], w["lam"]]
        w_out = w["w_out_even"]
        state_shape = (RET_HEADS, RET_DK, RET_DV)
        n_stack = w["w_in_even"].shape[0]
        proj_cols = EVEN_IN
    else:
        in_specs += [_layer_spec((D_MODEL, ODD_MAIN), i),
                     _layer_spec((D_MODEL, LANES), i),
                     _layer_spec((LANES, GLA_KEY), i),
                     _layer_spec((1, GLA_KEY), i),
                     _layer_spec((1, GLA_DV), i)]
        args += [w["w_in_odd"], w["wg1"], w["wg2"], w["bg"], w["gla_norm"]]
        w_out = w["w_out_odd"]
        state_shape = (GLA_HEADS, GLA_DK, GLA_DV)
        n_stack = w["w_in_odd"].shape[0]
        proj_cols = ODD_OUT
    in_specs += [_layer_spec((D_MODEL, D_MODEL), i),
                 _layer_spec((1, D_MODEL), layer),
                 _layer_spec((D_MODEL, D_FF), layer),
                 _layer_spec((D_FF, D_MODEL), layer),
                 _const_spec((1, D_MODEL))]
    args += [w_out, w["norm_mlp"], w["w_up"], w["w_down"], w["norm_final"]]
    aliases = {}
    if has_prev:
        in_specs.append(pl.BlockSpec(memory_space=pl.ANY))
        args.append(prev_state)
        aliases = {len(args) - 1: 1}

    if has_prev:
        state_spec = pl.BlockSpec((None, 1) + state_shape,
                                  lambda s: (i, mid_tile(s) // tps, 0, 0, 0))
    else:
        state_spec = pl.BlockSpec((n_stack, 1) + state_shape,
                                  lambda s: (0, mid_tile(s) // tps, 0, 0, 0))
    out_specs = [pl.BlockSpec((R, D_MODEL), lag), state_spec]
    out_shapes = [jax.ShapeDtypeStruct((n_rows, D_MODEL), F32),
                  jax.ShapeDtypeStruct((n_stack, B) + state_shape, F32)]
    scratch = [pltpu.VMEM((R, proj_cols), F32), pltpu.VMEM((R, proj_cols), F32),
               pltpu.VMEM((R, D_MODEL), BF16), pltpu.VMEM((R, D_MODEL), BF16),
               pltpu.VMEM((R, D_MODEL), F32), pltpu.VMEM((R, D_FF), BF16),
               pltpu.VMEM(state_shape, F32)]
    if even:
        out_specs += [pl.BlockSpec((1, 1, LRU_WIDTH), seq3),
                      pl.BlockSpec((1, CONV_W - 1, LRU_WIDTH), seq3)]
        out_shapes += [jax.ShapeDtypeStruct((B, 1, LRU_WIDTH), F32),
                       jax.ShapeDtypeStruct((B, CONV_W - 1, LRU_WIDTH), F32)]
        scratch += [pltpu.VMEM((1, LRU_WIDTH), F32),
                    pltpu.VMEM((SUBLANES, LRU_WIDTH), F32)]
    if not has_prev:
        per_step = n_short // NT
        assert per_step * NT == n_short
        out_specs.append(pl.BlockSpec((n_stack, per_step) + state_shape,
                                      lambda s: (0, jnp.minimum(s, NT - 1), 0, 0, 0)))
        out_shapes.append(jax.ShapeDtypeStruct((n_stack, n_short) + state_shape, F32))
    return pl.pallas_call(
        functools.partial(_fused_layer_kernel, even, final, C, R, tps, NT, i, has_prev),
        grid=(NT + 2,),
        in_specs=in_specs,
        out_specs=tuple(out_specs),
        out_shape=tuple(out_shapes),
        scratch_shapes=scratch,
        input_output_aliases=aliases,
        compiler_params=_params("arbitrary"),
        name="layer_even" if even else "layer_odd",
    )(*args)


def _rope_tables(pos0, T):
    half = RET_DK // 2
    inv = ROPE_BASE ** (-jnp.arange(half, dtype=F32) / half)
    ang = (pos0 + jnp.arange(T, dtype=jnp.int32)).astype(F32)[:, None] * inv[None, :]
    cos = jnp.cos(ang)
    sin = jnp.sin(ang)
    cos_full = jnp.tile(jnp.concatenate([cos, cos], axis=-1), (1, RET_HEADS))
    sin_signed = jnp.tile(jnp.concatenate([-sin, sin], axis=-1), (1, RET_HEADS))
    return cos_full, sin_signed


def _block_diag(w):
    nl, nb, c, d = w.shape
    eye = jnp.eye(nb, dtype=w.dtype)
    return (eye[None, :, None, :, None] * w[:, :, :, None, :]).reshape(nl, nb * c, nb * d)


def _prepare_weights(norm_mix, norm_mlp, norm_final, w_in_even, w_out_even, conv_w, conv_b,
                     lru_w_a, lru_b_a, lru_w_i, lru_b_i, lru_lambda,
                     w_in_odd, gla_w_gate2, gla_b_gate, gla_norm, w_out_odd, w_up, w_down):
    n_even = w_in_even.shape[0]
    n_odd = w_in_odd.shape[0]
    pad1 = LANES - GLA_RANK
    return dict(
        norm_mix=norm_mix.reshape(DEPTH, 1, D_MODEL),
        norm_mlp=norm_mlp.reshape(DEPTH, 1, D_MODEL),
        norm_final=norm_final.reshape(1, D_MODEL),
        w_in_even=w_in_even.astype(BF16),
        w_out_even=w_out_even.astype(BF16),
        conv_w=conv_w,
        conv_b=conv_b.reshape(n_even, 1, LRU_WIDTH),
        wa=_block_diag(lru_w_a).astype(BF16),
        ba=lru_b_a.reshape(n_even, 1, LRU_WIDTH),
        wi=_block_diag(lru_w_i).astype(BF16),
        bi=lru_b_i.reshape(n_even, 1, LRU_WIDTH),
        lam=lru_lambda.reshape(n_even, 1, LRU_WIDTH),
        w_in_odd=w_in_odd.astype(BF16),
        wg1=jnp.pad(w_in_odd[:, :, ODD_MAIN:], ((0, 0), (0, 0), (0, pad1))).astype(BF16),
        wg2=jnp.pad(gla_w_gate2, ((0, 0), (0, pad1), (0, 0))).astype(BF16),
        bg=gla_b_gate.reshape(n_odd, 1, GLA_KEY),
        gla_norm=gla_norm.reshape(n_odd, 1, GLA_DV),
        w_out_odd=w_out_odd.astype(BF16),
        w_up=w_up.astype(BF16),
        w_down=w_down.astype(BF16),
    )


def _prompt_trunk(x, w, n_short):
    B, T, _ = x.shape
    xf = x.reshape(B * T, D_MODEL)
    cos, sin = _rope_tables(0, T)
    ret_new, gla_new = None, None
    lrus, convs, zeros = [], [], []
    for layer in range(DEPTH):
        prev = ret_new if layer % 2 == 0 else gla_new
        outs = _fused_layer(xf, w, layer, cos, sin, prev, T, n_short)
        if prev is None:
            zeros.append(outs[-1])
            outs = outs[:-1]
        if layer % 2 == 0:
            xf, ret_new, lh, cb = outs
            lrus.append(lh.reshape(B, LRU_WIDTH))
            convs.append(cb)
        else:
            xf, gla_new = outs
    states = (ret_new, jnp.stack(lrus), jnp.stack(convs), gla_new)
    return xf.reshape(B, T, D_MODEL), states, zeros


def _sample_trunk(x, pos0, ret_s, lru_h, conv_buf, gla_s, ret_new, gla_new, w):
    B, T, _ = x.shape
    xf = x.reshape(B * T, D_MODEL)
    cos, sin = _rope_tables(pos0, T)
    lru_h = lru_h.reshape(lru_h.shape[0], B, 1, LRU_WIDTH)
    lrus, convs = [], []
    for layer in range(DEPTH):
        if layer % 2 == 0:
            proj = _norm_proj_even(xf, w, layer)
            mix, ret_new, lh, cb = _even_mixer(proj.reshape(B, T, EVEN_IN), pos0, cos, sin,
                                               ret_s, lru_h, conv_buf, w, layer, ret_new)
            lrus.append(lh.reshape(B, LRU_WIDTH))
            convs.append(cb)
        else:
            proj = _norm_proj_odd(xf, w, layer)
            mix, gla_new = _gla_mixer(proj.reshape(B, T, ODD_OUT), gla_s, w, layer, gla_new)
        xf = _out_mlp(xf, mix.reshape(B * T, D_MODEL), w, layer)
    return xf.reshape(B, T, D_MODEL), ret_new, jnp.stack(lrus), jnp.stack(convs), gla_new


def kernel(x_prompt, x_sample, state_ret, state_lru, state_conv, state_gla, norm_mix, norm_mlp, norm_final, w_in_even, w_out_even, conv_w, conv_b, lru_w_a, lru_b_a, lru_w_i, lru_b_i, lru_lambda, w_in_odd, gla_w_gate2, gla_b_gate, gla_norm, w_out_odd, w_up, w_down):
    w = _prepare_weights(norm_mix, norm_mlp, norm_final, w_in_even, w_out_even, conv_w, conv_b,
                         lru_w_a, lru_b_a, lru_w_i, lru_b_i, lru_lambda,
                         w_in_odd, gla_w_gate2, gla_b_gate, gla_norm, w_out_odd, w_up, w_down)
    y_p, (ret_p, lru_p, conv_p, gla_p), (ret_zero, gla_zero) = _prompt_trunk(
        x_prompt, w, x_sample.shape[0])
    y_s, ret_s, lru_s, conv_s, gla_s = _sample_trunk(x_sample, PAST_LEN, state_ret, state_lru,
                                                     state_conv, state_gla, ret_zero, gla_zero, w)
    return (y_p, y_s, ret_p, ret_s, lru_p, lru_s, conv_p, conv_s, gla_p, gla_s)
```

```python
import functools
import math

import jax
import jax.numpy as jnp
from jax import lax
from jax.experimental import pallas as pl
from jax.experimental.pallas import tpu as pltpu

F32 = jnp.float32
BF16 = jnp.bfloat16

D_MODEL = 1024
DEPTH = 4
PAST_LEN = 16384
EPS = 1e-6
ROPE_BASE = 10000.0

RET_HEADS = 4
RET_DK = 64
RET_DV = 128
RET_KEY = RET_HEADS * RET_DK
RET_VALUE = RET_HEADS * RET_DV
RET_CHUNK = 128

LRU_WIDTH = 512
CONV_W = 4
LRU_C = 8.0

GLA_HEADS = 4
GLA_KEY = 512
GLA_VALUE = 1024
GLA_DK = GLA_KEY // GLA_HEADS
GLA_DV = GLA_VALUE // GLA_HEADS
GLA_RANK = 16
GLA_TAU = 16.0
GLA_CHUNK = 64

D_FF = 4 * D_MODEL
FF_CHUNK = 1024

EVEN_IN = 2 * RET_KEY + 2 * RET_VALUE + 2 * LRU_WIDTH
ODD_MAIN = 2 * GLA_KEY + 2 * GLA_VALUE
ODD_OUT = ODD_MAIN + GLA_KEY

LANES = 128
SUBLANES = 8
ROW_TILE = 512
FUSED_ROWS = 256
MXU_PIECE = 256
SAMPLE_SEQS = 8
VMEM_LIMIT = 56 * 1024 * 1024


def _dot(a, b):
    return jnp.dot(a.astype(BF16), b.astype(BF16), preferred_element_type=F32)


def _dot_nt(a, b):
    return lax.dot_general(a.astype(BF16), b.astype(BF16), (((1,), (1,)), ((), ())),
                           preferred_element_type=F32)


def _dot_tn(a, b):
    return lax.dot_general(a.astype(BF16), b.astype(BF16), (((0,), (0,)), ((), ())),
                           preferred_element_type=F32)


def _rmsnorm(x, g):
    return x * lax.rsqrt(jnp.mean(x * x, axis=-1, keepdims=True) + EPS) * g


def _head_rms(x, eps=EPS):
    return x * lax.rsqrt(jnp.mean(x * x, axis=-1, keepdims=True) + eps)


def _shift_rows(x, d, fill):
    row = lax.broadcasted_iota(jnp.int32, x.shape, 0)
    return jnp.where(row >= d, pltpu.roll(x, d, axis=0), fill)


def _layer_spec(shape, layer):
    n = len(shape)
    return pl.BlockSpec((None,) + tuple(shape), lambda *_: (layer,) + (0,) * n,
                        pipeline_mode=pl.Buffered(1))


def _const_spec(shape):
    n = len(shape)
    return pl.BlockSpec(shape, lambda *_: (0,) * n, pipeline_mode=pl.Buffered(1))


def _params(*semantics):
    return pltpu.CompilerParams(dimension_semantics=semantics, vmem_limit_bytes=VMEM_LIMIT)


def _scaled_log_sigmoid(z):
    return (jnp.minimum(z, 0.0) - jnp.log(1.0 + jnp.exp(-jnp.abs(z)))) / GLA_TAU


def _log_gate(h, wg1_ref, wg2_ref, bg_ref):
    glr = _dot(h, wg1_ref[...])
    return _scaled_log_sigmoid(_dot(glr, wg2_ref[...]) + bg_ref[...])


def _norm_proj_even_kernel(x_ref, g_ref, w_ref, o_ref):
    h = _rmsnorm(x_ref[...], g_ref[...])
    o_ref[...] = _dot(h, w_ref[...])


def _norm_proj_odd_kernel(x_ref, g_ref, w_ref, wg1_ref, wg2_ref, bg_ref, o_ref):
    h = _rmsnorm(x_ref[...], g_ref[...]).astype(BF16)
    o_ref[:, :ODD_MAIN] = _dot(h, w_ref[...])
    o_ref[:, ODD_MAIN:] = _log_gate(h, wg1_ref, wg2_ref, bg_ref)


def _out_mlp_kernel(final, x_ref, m_ref, wo_ref, g_ref, wu_ref, wd_ref, gf_ref, o_ref,
                    h_scr, acc_scr):
    c = pl.program_id(0)

    @pl.when(c == 0)
    def _():
        x1 = x_ref[...] + _dot(m_ref[...], wo_ref[...])
        acc_scr[...] = x1
        h_scr[...] = _rmsnorm(x1, g_ref[...]).astype(BF16)

    u = jnp.square(jnp.maximum(_dot(h_scr[...], wu_ref[...]), 0.0))
    acc_scr[...] += _dot(u, wd_ref[...])

    @pl.when(c == pl.num_programs(0) - 1)
    def _():
        x2 = acc_scr[...]
        o_ref[...] = _rmsnorm(x2, gf_ref[...]) if final else x2


def _norm_proj_even(x, w, layer):
    n = x.shape[0]
    tm = min(ROW_TILE, n)
    return pl.pallas_call(
        _norm_proj_even_kernel,
        grid=(n // tm,),
        in_specs=[pl.BlockSpec((tm, D_MODEL), lambda i: (i, 0)),
                  _layer_spec((1, D_MODEL), layer),
                  _layer_spec((D_MODEL, EVEN_IN), layer // 2)],
        out_specs=pl.BlockSpec((tm, EVEN_IN), lambda i: (i, 0)),
        out_shape=jax.ShapeDtypeStruct((n, EVEN_IN), F32),
        compiler_params=_params("parallel"),
        name="norm_proj_even",
    )(x, w["norm_mix"], w["w_in_even"])


def _norm_proj_odd(x, w, layer):
    n = x.shape[0]
    tm = min(ROW_TILE, n)
    i = layer // 2
    return pl.pallas_call(
        _norm_proj_odd_kernel,
        grid=(n // tm,),
        in_specs=[pl.BlockSpec((tm, D_MODEL), lambda i: (i, 0)),
                  _layer_spec((1, D_MODEL), layer),
                  _layer_spec((D_MODEL, ODD_MAIN), i),
                  _layer_spec((D_MODEL, LANES), i),
                  _layer_spec((LANES, GLA_KEY), i),
                  _layer_spec((1, GLA_KEY), i)],
        out_specs=pl.BlockSpec((tm, ODD_OUT), lambda i: (i, 0)),
        out_shape=jax.ShapeDtypeStruct((n, ODD_OUT), F32),
        compiler_params=_params("parallel"),
        name="norm_proj_odd",
    )(x, w["norm_mix"], w["w_in_odd"], w["wg1"], w["wg2"], w["bg"])


def _out_mlp(x, mix, w, layer):
    n = x.shape[0]
    final = layer == DEPTH - 1
    w_out = w["w_out_even"] if layer % 2 == 0 else w["w_out_odd"]
    return pl.pallas_call(
        functools.partial(_out_mlp_kernel, final),
        grid=(D_FF // FF_CHUNK,),
        in_specs=[_const_spec((n, D_MODEL)),
                  _const_spec((n, D_MODEL)),
                  _layer_spec((D_MODEL, D_MODEL), layer // 2),
                  _layer_spec((1, D_MODEL), layer),
                  pl.BlockSpec((None, D_MODEL, FF_CHUNK), lambda c: (layer, 0, c)),
                  pl.BlockSpec((None, FF_CHUNK, D_MODEL), lambda c: (layer, c, 0)),
                  _const_spec((1, D_MODEL))],
        out_specs=pl.BlockSpec((n, D_MODEL), lambda c: (0, 0)),
        out_shape=jax.ShapeDtypeStruct((n, D_MODEL), F32),
        scratch_shapes=[pltpu.VMEM((n, D_MODEL), BF16), pltpu.VMEM((n, D_MODEL), F32)],
        compiler_params=_params("arbitrary"),
        name="out_mlp_final" if final else "out_mlp",
    )(x, mix, w_out, w["norm_mlp"], w["w_up"], w["w_down"], w["norm_final"])


def _run(stream):
    for _ in stream:
        pass


def _interleave(*streams):
    done = [0.0] * len(streams)
    alive = list(range(len(streams)))
    while alive:
        i = min(alive, key=lambda j: done[j])
        try:
            done[i] += next(streams[i])
        except StopIteration:
            alive.remove(i)


class _Pace:
    def __init__(self):
        self.zero = None

    def after(self, value):
        t = value[0:1, 0:1]
        self.zero = jnp.where(t == t, 0.0, t)

    def __call__(self, x):
        return x if self.zero is None else x + self.zero


def _no_pace(x):
    return x


def _even_mixer_tile(C, chunks, pos_base, load, store, cos_ref, sin_ref,
                     s_src, s_dst, h_src, h_dst, tail_ref, weights, pace=_no_pace):
    cw_ref, cbias_ref, wa_ref, ba_ref, wi_ref, bi_ref, lam_ref = weights
    lane = lax.broadcasted_iota(jnp.int32, (C, RET_KEY), 1)
    first_half = (lane % RET_DK) < (RET_DK // 2)
    ri = lax.broadcasted_iota(jnp.int32, (C, C), 0)
    ci = lax.broadcasted_iota(jnp.int32, (C, C), 1)
    rel = (ri - ci).astype(F32)
    row = lax.broadcasted_iota(jnp.int32, (C, 1), 0).astype(F32)
    trow = lax.broadcasted_iota(jnp.int32, (C, LRU_WIDTH), 0)
    row8 = lax.broadcasted_iota(jnp.int32, (SUBLANES, LRU_WIDTH), 0)
    lam = lam_ref[...]
    softplus_neg_lam = jnp.maximum(-lam, 0.0) + jnp.log(1.0 + jnp.exp(-jnp.abs(lam)))
    log_decay = [math.log1p(-(2.0 ** (-5.0 - h))) for h in range(RET_HEADS)]
    decay_mask = [jnp.where(rel >= 0, jnp.exp(jnp.maximum(rel, 0.0) * lg), 0.0)
                  for lg in log_decay]
    q_decay = [jnp.exp((row + 1.0) * lg) for lg in log_decay]
    k_decay = [jnp.exp((C - 1.0 - row) * lg) for lg in log_decay]
    n_scan = max(1, int(math.log2(C)))

    v0 = 2 * RET_KEY
    g0 = v0 + RET_VALUE
    x0 = g0 + RET_VALUE
    xg0 = x0 + LRU_WIDTH
    heads = range(RET_HEADS)
    J = range(chunks)
    rows = [slice(j * C, (j + 1) * C) for j in J]

    def delayed(x, t, n):
        rolled = pltpu.roll(x, n, axis=0)
        head = jnp.where(row8 < n, pltpu.roll(t, n, axis=0), rolled[0:SUBLANES, :])
        if C == SUBLANES:
            return head
        return jnp.concatenate([head, rolled[SUBLANES:, :]], axis=0)

    qh, kh, q_dec, k_dec, vh, xc = ([None] * chunks for _ in range(6))
    tail = tail_ref[...]
    for j in J:
        cos = cos_ref[rows[j], :]
        sin = sin_ref[rows[j], :]

        def rope(x):
            partner = jnp.where(first_half,
                                pltpu.roll(x, RET_KEY - RET_DK // 2, axis=1),
                                pltpu.roll(x, RET_DK // 2, axis=1))
            return x * cos + partner * sin

        q = rope(load(rows[j], slice(0, RET_KEY)))
        k = rope(load(rows[j], slice(RET_KEY, 2 * RET_KEY))) * (RET_DK ** -0.5)
        q = [q[:, h * RET_DK:(h + 1) * RET_DK] for h in heads]
        k = [k[:, h * RET_DK:(h + 1) * RET_DK] for h in heads]
        q_dec[j] = [(q[h] * q_decay[h]).astype(BF16) for h in heads]
        k_dec[j] = [(k[h] * k_decay[h]).astype(BF16) for h in heads]
        qh[j] = [q[h].astype(BF16) for h in heads]
        kh[j] = [k[h].astype(BF16) for h in heads]
        vh[j] = [load(rows[j], slice(v0 + h * RET_DV, v0 + (h + 1) * RET_DV)).astype(BF16)
                 for h in heads]
        yield 0.08 / chunks

        x_new = load(rows[j], slice(x0, x0 + LRU_WIDTH))
        acc = pace(cbias_ref[...])
        for i in range(CONV_W - 1):
            acc = acc + delayed(x_new, tail, CONV_W - 1 - i) * cw_ref[i:i + 1, :]
        xc[j] = acc + x_new * cw_ref[CONV_W - 1:CONV_W, :]
        tail = x_new[C - SUBLANES:C, :]
        yield 0.10 / chunks
    tail_ref[...] = tail

    scores = [[_dot_nt(qh[j][h], kh[j][h]) for h in heads] for j in J]
    kv = [[_dot_tn(k_dec[j][h], vh[j][h]) for h in heads] for j in J]
    xc_all = xc[0] if chunks == 1 else jnp.concatenate(xc, axis=0)
    xc_all = xc_all.astype(BF16)
    diag = [slice(c, c + MXU_PIECE) for c in range(0, LRU_WIDTH, MXU_PIECE)]
    r_all = jnp.concatenate([_dot(xc_all[:, d], wa_ref[d, d]) for d in diag], axis=1)
    i_all = jnp.concatenate([_dot(xc_all[:, d], wi_ref[d, d]) for d in diag], axis=1)
    r_pre = [r_all[rows[j], :] for j in J]
    i_pre = [i_all[rows[j], :] for j in J]
    yield 0.0

    probs, s_in = [None] * chunks, [None] * chunks
    s_cur = [s_src[h] for h in heads]
    for j in J:
        probs[j] = [(scores[j][h] * decay_mask[h]).astype(BF16) for h in heads]
        s_in[j] = [s_cur[h].astype(BF16) for h in heads]
        s_cur = [s_cur[h] * math.exp(C * log_decay[h]) + kv[j][h] for h in heads]
        yield 0.06 / chunks
    for h in heads:
        s_dst[h] = s_cur[h]

    a_cum, b_loc = [None] * chunks, [None] * chunks
    for j in J:
        r = jax.nn.sigmoid(r_pre[j] + pace(ba_ref[...]))
        gate_i = jax.nn.sigmoid(i_pre[j] + bi_ref[...])
        log_a = -LRU_C * r * softplus_neg_lam
        a = jnp.exp(log_a)
        y = -jnp.tanh(log_a) * (a * a + 1.0)
        mult = jnp.where(y > 0.0, y * lax.rsqrt(y), 0.0)
        pos = trow + (pos_base + j * C)
        mult = jnp.where(pos == 0, 1.0, mult)
        b = mult * (gate_i * xc[j])
        yield 0.20 / chunks
        d = 1
        while d < C:
            b = a * _shift_rows(b, d, pace(0.0)) + b
            a = a * _shift_rows(a, d, pace(1.0))
            d *= 2
            yield 0.36 / (chunks * n_scan)
        a_cum[j], b_loc[j] = a, b

    out = [[_dot(probs[j][h], vh[j][h]) + _dot(q_dec[j][h], s_in[j][h]) for h in heads]
           for j in J]
    yield 0.0

    carry = h_src[...]
    for j in J:
        hidden = b_loc[j] + a_cum[j] * carry
        carry = hidden[C - 1:C, :]
        xg = load(rows[j], slice(xg0, xg0 + LRU_WIDTH))
        store(rows[j], slice(RET_VALUE, RET_VALUE + LRU_WIDTH), hidden * jax.nn.gelu(xg))
        yield 0.08 / chunks
        for h in heads:
            gh = load(rows[j], slice(g0 + h * RET_DV, g0 + (h + 1) * RET_DV))
            store(rows[j], slice(h * RET_DV, (h + 1) * RET_DV),
                  jax.nn.silu(gh) * _head_rms(out[j][h], pace(EPS)))
            yield 0.03 / chunks
    h_dst[...] = carry


def _gla_mixer_tile(C, chunks, load, store, s_src, s_dst, nw, pace=_no_pace):
    k0 = GLA_KEY
    v0 = 2 * GLA_KEY
    r0 = v0 + GLA_VALUE
    la0 = r0 + GLA_VALUE
    ri = lax.broadcasted_iota(jnp.int32, (C, C), 0)
    ci = lax.broadcasted_iota(jnp.int32, (C, C), 1)
    causal = ri >= ci
    di = lax.broadcasted_iota(jnp.int32, (GLA_DK, GLA_DK), 0)
    dj = lax.broadcasted_iota(jnp.int32, (GLA_DK, GLA_DK), 1)
    eye = di == dj
    heads = range(GLA_HEADS)
    key = [slice(h * GLA_DK, (h + 1) * GLA_DK) for h in heads]
    J = range(chunks)
    rows = [slice(j * C, (j + 1) * C) for j in J]

    q_in, k_in, q_st, k_st, decay_col, vh = ([None] * chunks for _ in range(6))
    for j in J:
        b = load(rows[j], slice(la0, la0 + GLA_KEY))
        d = 1
        while d < C:
            b = b + _shift_rows(b, d, pace(0.0))
            d *= 2
        yield 0.12 / chunks
        b_mid = pace(b[C // 2:C // 2 + 1, :])
        b_last = b[C - 1:C, :]
        q = load(rows[j], slice(0, GLA_KEY)) * (GLA_DK ** -0.5)
        k = load(rows[j], slice(k0, k0 + GLA_KEY))
        q_in[j] = (q * jnp.exp(b - b_mid)).astype(BF16)
        k_in[j] = (k * jnp.exp(b_mid - b)).astype(BF16)
        q_st[j] = (q * jnp.exp(b)).astype(BF16)
        k_st[j] = (k * jnp.exp(b_last - b)).astype(BF16)
        s_decay = jnp.exp(b_last)
        decay_col[j] = [jnp.sum(jnp.where(eye, s_decay[:, key[h]], 0.0), axis=1, keepdims=True)
                        for h in heads]
        vh[j] = [load(rows[j], slice(v0 + h * GLA_DV, v0 + (h + 1) * GLA_DV)).astype(BF16)
                 for h in heads]
        yield 0.28 / chunks

    scores = [[_dot_nt(q_in[j][:, key[h]], k_in[j][:, key[h]]) for h in heads] for j in J]
    kv = [[_dot_tn(k_st[j][:, key[h]], vh[j][h]) for h in heads] for j in J]
    yield 0.0

    probs, s_in = [None] * chunks, [None] * chunks
    s_cur = [s_src[h] for h in heads]
    for j in J:
        probs[j] = [jnp.where(causal, scores[j][h], 0.0).astype(BF16) for h in heads]
        s_in[j] = [s_cur[h].astype(BF16) for h in heads]
        s_cur = [s_cur[h] * decay_col[j][h] + kv[j][h] for h in heads]
        yield 0.16 / chunks
    for h in heads:
        s_dst[h] = s_cur[h]

    out = [[_dot(probs[j][h], vh[j][h]) + _dot(q_st[j][:, key[h]], s_in[j][h]) for h in heads]
           for j in J]
    yield 0.0

    for j in J:
        for h in heads:
            rh = load(rows[j], slice(r0 + h * GLA_DV, r0 + (h + 1) * GLA_DV))
            store(rows[j], slice(h * GLA_DV, (h + 1) * GLA_DV),
                  jax.nn.silu(rh) * (_head_rms(out[j][h], pace(EPS)) * nw))
            yield 0.11 / chunks


def _even_mixer_kernel(pos0, chunk, seqs, *refs):
    (proj_ref, cos_ref, sin_ref, s0_ref, h0_ref, cb_ref,
     cw_ref, cbias_ref, wa_ref, ba_ref, wi_ref, bi_ref, lam_ref) = refs[:13]
    refs = refs[14:]
    mix_ref, snew_ref, hnew_ref, cnew_ref, tail_scr = refs
    weights = (cw_ref, cbias_ref, wa_ref, ba_ref, wi_ref, bi_ref, lam_ref)
    for g in range(seqs):
        tail = tail_scr.at[g]
        tail[...] = jnp.zeros((SUBLANES, LRU_WIDTH), F32)
        tail[SUBLANES - (CONV_W - 1):SUBLANES, :] = cb_ref[g]

        def load(rows, cols, g=g):
            return proj_ref[g, rows, cols]

        def store(rows, cols, val, g=g):
            mix_ref[g, rows, cols] = val

        _run(_even_mixer_tile(chunk, 1, pos0, load, store, cos_ref, sin_ref,
                              s0_ref.at[g], snew_ref.at[g], h0_ref.at[g], hnew_ref.at[g],
                              tail, weights))
        cnew_ref[g] = tail[SUBLANES - (CONV_W - 1):SUBLANES, :]


def _even_mixer(proj, pos0, cos, sin, ret_s, lru_h, conv_buf, w, layer, prev_ret):
    B, T, _ = proj.shape
    i = layer // 2
    n_even = w["w_in_even"].shape[0]
    G = SAMPLE_SEQS
    ret_block = (None, G, RET_HEADS, RET_DK, RET_DV)
    in_specs = [pl.BlockSpec((G, T, EVEN_IN), lambda b: (b, 0, 0)),
                _const_spec((T, RET_KEY)),
                _const_spec((T, RET_KEY)),
                pl.BlockSpec(ret_block, lambda b: (i, b, 0, 0, 0)),
                pl.BlockSpec((None, G, 1, LRU_WIDTH), lambda b: (i, b, 0, 0)),
                pl.BlockSpec((None, G, CONV_W - 1, LRU_WIDTH), lambda b: (i, b, 0, 0)),
                _layer_spec((CONV_W, LRU_WIDTH), i),
                _layer_spec((1, LRU_WIDTH), i),
                _layer_spec((LRU_WIDTH, LRU_WIDTH), i),
                _layer_spec((1, LRU_WIDTH), i),
                _layer_spec((LRU_WIDTH, LRU_WIDTH), i),
                _layer_spec((1, LRU_WIDTH), i),
                _layer_spec((1, LRU_WIDTH), i)]
    args = [proj, cos, sin, ret_s, lru_h, conv_buf, w["conv_w"], w["conv_b"],
            w["wa"], w["ba"], w["wi"], w["bi"], w["lam"]]
    in_specs.append(pl.BlockSpec(memory_space=pl.ANY))
    args.append(prev_ret)
    aliases = {len(args) - 1: 1}
    out_shapes = (jax.ShapeDtypeStruct((B, T, D_MODEL), F32),
                  jax.ShapeDtypeStruct((n_even, B, RET_HEADS, RET_DK, RET_DV), F32),
                  jax.ShapeDtypeStruct((B, 1, LRU_WIDTH), F32),
                  jax.ShapeDtypeStruct((B, CONV_W - 1, LRU_WIDTH), F32))
    return pl.pallas_call(
        functools.partial(_even_mixer_kernel, pos0, T, G),
        grid=(B // G,),
        in_specs=in_specs,
        out_specs=(pl.BlockSpec((G, T, D_MODEL), lambda b: (b, 0, 0)),
                   pl.BlockSpec(ret_block, lambda b: (i, b, 0, 0, 0)),
                   pl.BlockSpec((G, 1, LRU_WIDTH), lambda b: (b, 0, 0)),
                   pl.BlockSpec((G, CONV_W - 1, LRU_WIDTH), lambda b: (b, 0, 0))),
        out_shape=out_shapes,
        scratch_shapes=[pltpu.VMEM((G, SUBLANES, LRU_WIDTH), F32)],
        input_output_aliases=aliases,
        compiler_params=_params("parallel"),
        name="even_mixer",
    )(*args)


def _gla_mixer_kernel(chunk, seqs, *refs):
    proj_ref, s0_ref, nw_ref = refs[:3]
    refs = refs[4:]
    mix_ref, snew_ref = refs
    nw = nw_ref[...]
    for g in range(seqs):
        def load(rows, cols, g=g):
            return proj_ref[g, rows, cols]

        def store(rows, cols, val, g=g):
            mix_ref[g, rows, cols] = val

        _run(_gla_mixer_tile(chunk, 1, load, store, s0_ref.at[g], snew_ref.at[g], nw))


def _gla_mixer(proj, gla_s, w, layer, prev_gla):
    B, T, _ = proj.shape
    i = layer // 2
    n_odd = w["w_in_odd"].shape[0]
    G = SAMPLE_SEQS
    state = (None, G, GLA_HEADS, GLA_DK, GLA_DV)
    in_specs = [pl.BlockSpec((G, T, ODD_OUT), lambda b: (b, 0, 0)),
                pl.BlockSpec(state, lambda b: (i, b, 0, 0, 0)),
                _layer_spec((1, GLA_DV), i)]
    args = [proj, gla_s, w["gla_norm"]]
    in_specs.append(pl.BlockSpec(memory_space=pl.ANY))
    args.append(prev_gla)
    aliases = {len(args) - 1: 1}
    return pl.pallas_call(
        functools.partial(_gla_mixer_kernel, T, G),
        grid=(B // G,),
        in_specs=in_specs,
        out_specs=(pl.BlockSpec((G, T, D_MODEL), lambda b: (b, 0, 0)),
                   pl.BlockSpec(state, lambda b: (i, b, 0, 0, 0))),
        out_shape=(jax.ShapeDtypeStruct((B, T, D_MODEL), F32),
                   jax.ShapeDtypeStruct((n_odd, B, GLA_HEADS, GLA_DK, GLA_DV), F32)),
        input_output_aliases=aliases,
        compiler_params=_params("parallel"),
        name="gla_mixer",
    )(*args)


def _fused_layer_kernel(even, final, chunk, rows, tiles_per_seq, n_tiles, slot, has_prev, *refs):
    it = iter(refs)
    xlead_ref, xlag_ref, gmix_ref, win_ref = (next(it) for _ in range(4))
    if even:
        cos_ref, sin_ref = next(it), next(it)
        mixer_weights = tuple(next(it) for _ in range(7))
    else:
        wg1_ref, wg2_ref, bg_ref, nw_ref = (next(it) for _ in range(4))
    wo_ref, gmlp_ref, wu_ref, wd_ref, gf_ref = (next(it) for _ in range(5))
    if has_prev:
        next(it)
    out_ref, snew_ref = next(it), next(it)
    if even:
        hnew_ref, cnew_ref = next(it), next(it)
    if not has_prev:
        zeros_ref = next(it)
    proj_scr = (next(it), next(it))
    mix_scr = (next(it), next(it))
    x1_scr, u_scr, s_scr = next(it), next(it), next(it)
    if even:
        h_scr, tail_scr = next(it), next(it)

    s = pl.program_id(0)
    tile_in_seq = lax.rem(jnp.clip(s - 1, 0, n_tiles - 1), tiles_per_seq)

    @pl.when(s == 0)
    def _():
        proj_scr[1][...] = jnp.zeros(proj_scr[1].shape, F32)
        mix_scr[0][...] = jnp.zeros(mix_scr[0].shape, BF16)

    @pl.when((s == 0) | (lax.rem(s + tiles_per_seq - 1, tiles_per_seq) == 0))
    def _():
        s_scr[...] = jnp.zeros(s_scr.shape, F32)
        if even:
            h_scr[...] = jnp.zeros(h_scr.shape, F32)
            tail_scr[...] = jnp.zeros(tail_scr.shape, F32)

    main_cols = EVEN_IN if even else ODD_MAIN
    macs_per_row = D_MODEL * (main_cols + D_MODEL + 2 * D_FF)

    def matrix_stream(proj_a, mix_c, pace):
        P = MXU_PIECE
        h_in = _rmsnorm(xlead_ref[...], gmix_ref[...]).astype(BF16)
        m = mix_c[...]
        for n0 in range(0, D_MODEL, P):
            x1 = xlag_ref[:, n0:n0 + P] + _dot(m, wo_ref[:, n0:n0 + P])
            x1_scr[:, n0:n0 + P] = x1
            pace.after(x1)
            yield D_MODEL * P / macs_per_row
        h_mlp = _rmsnorm(x1_scr[...], gmlp_ref[...]).astype(BF16)
        if not even:
            glr = _dot(h_in, wg1_ref[...]).astype(BF16)
        for n0 in range(0, main_cols, P):
            p = _dot(h_in, win_ref[:, n0:n0 + P])
            proj_a[:, n0:n0 + P] = p
            pace.after(p)
            if not even and n0 == P:
                z = _dot(glr, wg2_ref[...]) + bg_ref[...]
            yield D_MODEL * P / macs_per_row
        if not even:
            proj_a[:, ODD_MAIN:] = _scaled_log_sigmoid(z)
            yield 0.0
        for n0 in range(0, D_FF, P):
            u = jnp.square(jnp.maximum(_dot(h_mlp, wu_ref[:, n0:n0 + P]), 0.0))
            u_scr[:, n0:n0 + P] = u.astype(BF16)
            pace.after(u)
            yield D_MODEL * P / macs_per_row
        for n0 in range(0, D_MODEL, P):
            x2 = x1_scr[:, n0:n0 + P] + _dot(u_scr[...], wd_ref[:, n0:n0 + P])
            if final:
                x1_scr[:, n0:n0 + P] = x2
            else:
                out_ref[:, n0:n0 + P] = x2
            pace.after(x2)
            yield D_FF * P / macs_per_row
        if final:
            out_ref[...] = _rmsnorm(x1_scr[...], gf_ref[...])

    def step(par):
        proj_a, proj_b = proj_scr[par], proj_scr[1 - par]
        mix_b, mix_c = mix_scr[1 - par], mix_scr[par]

        def load(r, c):
            return proj_b[r, c]

        def store(r, c, val):
            mix_b[r, c] = val.astype(BF16)

        pace = _Pace()
        if even:
            vector_stream = _even_mixer_tile(chunk, rows // chunk, tile_in_seq * rows, load,
                                             store, cos_ref, sin_ref, s_scr, s_scr, h_scr, h_scr,
                                             tail_scr, mixer_weights, pace)
        else:
            vector_stream = _gla_mixer_tile(chunk, rows // chunk, load, store, s_scr, s_scr,
                                            nw_ref[...], pace)
        _interleave(matrix_stream(proj_a, mix_c, pace), vector_stream)

    parity = lax.rem(s, 2)
    pl.when(parity == 0)(functools.partial(step, 0))
    pl.when(parity == 1)(functools.partial(step, 1))

    if not has_prev:
        zeros_ref[...] = jnp.zeros(zeros_ref.shape, F32)

    @pl.when((s >= 1) & (s <= n_tiles) & (lax.rem(s, tiles_per_seq) == 0))
    def _():
        if has_prev:
            snew_ref[0] = s_scr[...]
        else:
            for other in range(snew_ref.shape[0]):
                snew_ref[other, 0] = s_scr[...] if other == slot else jnp.zeros(s_scr.shape, F32)
        if even:
            hnew_ref[0] = h_scr[...]
            cnew_ref[0] = tail_scr[SUBLANES - (CONV_W - 1):SUBLANES, :]


def _fused_layer(x, w, layer, cos, sin, prev_state, seq_len, n_short):
    n_rows = x.shape[0]
    B = n_rows // seq_len
    R = FUSED_ROWS
    even = layer % 2 == 0
    final = layer == DEPTH - 1
    i = layer // 2
    C = math.gcd(seq_len, RET_CHUNK if even else GLA_CHUNK)
    tps = seq_len // R
    NT = n_rows // R
    has_prev = prev_state is not None

    def lead(s):
        return (jnp.minimum(s, NT - 1), 0)

    def mid_tile(s):
        return jnp.clip(s - 1, 0, NT - 1)

    def lag(s):
        return (jnp.clip(s - 2, 0, NT - 1), 0)

    def seq3(s):
        return (mid_tile(s) // tps, 0, 0)

    in_specs = [pl.BlockSpec((R, D_MODEL), lead),
                pl.BlockSpec((R, D_MODEL), lag),
                _layer_spec((1, D_MODEL), layer)]
    args = [x, x, w["norm_mix"]]
    if even:
        in_specs += [_layer_spec((D_MODEL, EVEN_IN), i),
                     pl.BlockSpec((R, RET_KEY), lambda s: (mid_tile(s) % tps, 0)),
                     pl.BlockSpec((R, RET_KEY), lambda s: (mid_tile(s) % tps, 0)),
                     _layer_spec((CONV_W, LRU_WIDTH), i),
                     _layer_spec((1, LRU_WIDTH), i),
                     _layer_spec((LRU_WIDTH, LRU_WIDTH), i),
                     _layer_spec((1, LRU_WIDTH), i),
                     _layer_spec((LRU_WIDTH, LRU_WIDTH), i),
                     _layer_spec((1, LRU_WIDTH), i),
                     _layer_spec((1, LRU_WIDTH), i)]
        args += [w["w_in_even"], cos, sin, w["conv_w"], w["conv_b"], w["wa"], w["ba"],
                 w["wi"], w["bi"], w["lam"]]
        w_out = w["w_out_even"]
        state_shape = (RET_HEADS, RET_DK, RET_DV)
        n_stack = w["w_in_even"].shape[0]
        proj_cols = EVEN_IN
    else:
        in_specs += [_layer_spec((D_MODEL, ODD_MAIN), i),
                     _layer_spec((D_MODEL, LANES), i),
                     _layer_spec((LANES, GLA_KEY), i),
                     _layer_spec((1, GLA_KEY), i),
                     _layer_spec((1, GLA_DV), i)]
        args += [w["w_in_odd"], w["wg1"], w["wg2"], w["bg"], w["gla_norm"]]
        w_out = w["w_out_odd"]
        state_shape = (GLA_HEADS, GLA_DK, GLA_DV)
        n_stack = w["w_in_odd"].shape[0]
        proj_cols = ODD_OUT
    in_specs += [_layer_spec((D_MODEL, D_MODEL), i),
                 _layer_spec((1, D_MODEL), layer),
                 _layer_spec((D_MODEL, D_FF), layer),
                 _layer_spec((D_FF, D_MODEL), layer),
                 _const_spec((1, D_MODEL))]
    args += [w_out, w["norm_mlp"], w["w_up"], w["w_down"], w["norm_final"]]
    aliases = {}
    if has_prev:
        in_specs.append(pl.BlockSpec(memory_space=pl.ANY))
        args.append(prev_state)
        aliases = {len(args) - 1: 1}

    if has_prev:
        state_spec = pl.BlockSpec((None, 1) + state_shape,
                                  lambda s: (i, mid_tile(s) // tps, 0, 0, 0))
    else:
        state_spec = pl.BlockSpec((n_stack, 1) + state_shape,
                                  lambda s: (0, mid_tile(s) // tps, 0, 0, 0))
    out_specs = [pl.BlockSpec((R, D_MODEL), lag), state_spec]
    out_shapes = [jax.ShapeDtypeStruct((n_rows, D_MODEL), F32),
                  jax.ShapeDtypeStruct((n_stack, B) + state_shape, F32)]
    scratch = [pltpu.VMEM((R, proj_cols), F32), pltpu.VMEM((R, proj_cols), F32),
               pltpu.VMEM((R, D_MODEL), BF16), pltpu.VMEM((R, D_MODEL), BF16),
               pltpu.VMEM((R, D_MODEL), F32), pltpu.VMEM((R, D_FF), BF16),
               pltpu.VMEM(state_shape, F32)]
    if even:
        out_specs += [pl.BlockSpec((1, 1, LRU_WIDTH), seq3),
                      pl.BlockSpec((1, CONV_W - 1, LRU_WIDTH), seq3)]
        out_shapes += [jax.ShapeDtypeStruct((B, 1, LRU_WIDTH), F32),
                       jax.ShapeDtypeStruct((B, CONV_W - 1, LRU_WIDTH), F32)]
        scratch += [pltpu.VMEM((1, LRU_WIDTH), F32),
                    pltpu.VMEM((SUBLANES, LRU_WIDTH), F32)]
    if not has_prev:
        per_step = n_short // NT
        assert per_step * NT == n_short
        out_specs.append(pl.BlockSpec((n_stack, per_step) + state_shape,
                                      lambda s: (0, jnp.minimum(s, NT - 1), 0, 0, 0)))
        out_shapes.append(jax.ShapeDtypeStruct((n_stack, n_short) + state_shape, F32))
    return pl.pallas_call(
        functools.partial(_fused_layer_kernel, even, final, C, R, tps, NT, i, has_prev),
        grid=(NT + 2,),
        in_specs=in_specs,
        out_specs=tuple(out_specs),
        out_shape=tuple(out_shapes),
        scratch_shapes=scratch,
        input_output_aliases=aliases,
        compiler_params=_params("arbitrary"),
        name="layer_even" if even else "layer_odd",
    )(*args)


def _rope_tables(pos0, T):
    half = RET_DK // 2
    inv = ROPE_BASE ** (-jnp.arange(half, dtype=F32) / half)
    ang = (pos0 + jnp.arange(T, dtype=jnp.int32)).astype(F32)[:, None] * inv[None, :]
    cos = jnp.cos(ang)
    sin = jnp.sin(ang)
    cos_full = jnp.tile(jnp.concatenate([cos, cos], axis=-1), (1, RET_HEADS))
    sin_signed = jnp.tile(jnp.concatenate([-sin, sin], axis=-1), (1, RET_HEADS))
    return cos_full, sin_signed


def _block_diag(w):
    nl, nb, c, d = w.shape
    eye = jnp.eye(nb, dtype=w.dtype)
    return (eye[None, :, None, :, None] * w[:, :, :, None, :]).reshape(nl, nb * c, nb * d)


def _prepare_weights(norm_mix, norm_mlp, norm_final, w_in_even, w_out_even, conv_w, conv_b,
                     lru_w_a, lru_b_a, lru_w_i, lru_b_i, lru_lambda,
                     w_in_odd, gla_w_gate2, gla_b_gate, gla_norm, w_out_odd, w_up, w_down):
    n_even = w_in_even.shape[0]
    n_odd = w_in_odd.shape[0]
    pad1 = LANES - GLA_RANK
    return dict(
        norm_mix=norm_mix.reshape(DEPTH, 1, D_MODEL),
        norm_mlp=norm_mlp.reshape(DEPTH, 1, D_MODEL),
        norm_final=norm_final.reshape(1, D_MODEL),
        w_in_even=w_in_even.astype(BF16),
        w_out_even=w_out_even.astype(BF16),
        conv_w=conv_w,
        conv_b=conv_b.reshape(n_even, 1, LRU_WIDTH),
        wa=_block_diag(lru_w_a).astype(BF16),
        ba=lru_b_a.reshape(n_even, 1, LRU_WIDTH),
        wi=_block_diag(lru_w_i).astype(BF16),
        bi=lru_b_i.reshape(n_even, 1, LRU_WIDTH),
        lam=lru_lambda.reshape(n_even, 1, LRU_WIDTH),
        w_in_odd=w_in_odd.astype(BF16),
        wg1=jnp.pad(w_in_odd[:, :, ODD_MAIN:], ((0, 0), (0, 0), (0, pad1))).astype(BF16),
        wg2=jnp.pad(gla_w_gate2, ((0, 0), (0, pad1), (0, 0))).astype(BF16),
        bg=gla_b_gate.reshape(n_odd, 1, GLA_KEY),
        gla_norm=gla_norm.reshape(n_odd, 1, GLA_DV),
        w_out_odd=w_out_odd.astype(BF16),
        w_up=w_up.astype(BF16),
        w_down=w_down.astype(BF16),
    )


def _prompt_trunk(x, w, n_short):
    B, T, _ = x.shape
    xf = x.reshape(B * T, D_MODEL)
    cos, sin = _rope_tables(0, T)
    ret_new, gla_new = None, None
    lrus, convs, zeros = [], [], []
    for layer in range(DEPTH):
        prev = ret_new if layer % 2 == 0 else gla_new
        outs = _fused_layer(xf, w, layer, cos, sin, prev, T, n_short)
        if prev is None:
            zeros.append(outs[-1])
            outs = outs[:-1]
        if layer % 2 == 0:
            xf, ret_new, lh, cb = outs
            lrus.append(lh.reshape(B, LRU_WIDTH))
            convs.append(cb)
        else:
            xf, gla_new = outs
    states = (ret_new, jnp.stack(lrus), jnp.stack(convs), gla_new)
    return xf.reshape(B, T, D_MODEL), states, zeros


def _sample_trunk(x, pos0, ret_s, lru_h, conv_buf, gla_s, ret_new, gla_new, w):
    B, T, _ = x.shape
    xf = x.reshape(B * T, D_MODEL)
    cos, sin = _rope_tables(pos0, T)
    lru_h = lru_h.reshape(lru_h.shape[0], B, 1, LRU_WIDTH)
    lrus, convs = [], []
    for layer in range(DEPTH):
        if layer % 2 == 0:
            proj = _norm_proj_even(xf, w, layer)
            mix, ret_new, lh, cb = _even_mixer(proj.reshape(B, T, EVEN_IN), pos0, cos, sin,
                                               ret_s, lru_h, conv_buf, w, layer, ret_new)
            lrus.append(lh.reshape(B, LRU_WIDTH))
            convs.append(cb)
        else:
            proj = _norm_proj_odd(xf, w, layer)
            mix, gla_new = _gla_mixer(proj.reshape(B, T, ODD_OUT), gla_s, w, layer, gla_new)
        xf = _out_mlp(xf, mix.reshape(B * T, D_MODEL), w, layer)
    return xf.reshape(B, T, D_MODEL), ret_new, jnp.stack(lrus), jnp.stack(convs), gla_new


def kernel(x_prompt, x_sample, state_ret, state_lru, state_conv, state_gla, norm_mix, norm_mlp, norm_final, w_in_even, w_out_even, conv_w, conv_b, lru_w_a, lru_b_a, lru_w_i, lru_b_i, lru_lambda, w_in_odd, gla_w_gate2, gla_b_gate, gla_norm, w_out_odd, w_up, w_down):
    w = _prepare_weights(norm_mix, norm_mlp, norm_final, w_in_even, w_out_even, conv_w, conv_b,
                         lru_w_a, lru_b_a, lru_w_i, lru_b_i, lru_lambda,
                         w_in_odd, gla_w_gate2, gla_b_gate, gla_norm, w_out_odd, w_up, w_down)
    y_p, (ret_p, lru_p, conv_p, gla_p), (ret_zero, gla_zero) = _prompt_trunk(
        x_prompt, w, x_sample.shape[0])
    y_s, ret_s, lru_s, conv_s, gla_s = _sample_trunk(x_sample, PAST_LEN, state_ret, state_lru,
                                                     state_conv, state_gla, ret_zero, gla_zero, w)
    return (y_p, y_s, ret_p, ret_s, lru_p, lru_s, conv_p, conv_s, gla_p, gla_s)
```

```python
import functools
import math

import jax
import jax.numpy as jnp
from jax import lax
from jax.experimental import pallas as pl
from jax.experimental.pallas import tpu as pltpu

F32 = jnp.float32
BF16 = jnp.bfloat16

D_MODEL = 1024
DEPTH = 4
PAST_LEN = 16384
EPS = 1e-6
ROPE_BASE = 10000.0

RET_HEADS = 4
RET_DK = 64
RET_DV = 128
RET_KEY = RET_HEADS * RET_DK
RET_VALUE = RET_HEADS * RET_DV
RET_CHUNK = 128

LRU_WIDTH = 512
CONV_W = 4
LRU_C = 8.0

GLA_HEADS = 4
GLA_KEY = 512
GLA_VALUE = 1024
GLA_DK = GLA_KEY // GLA_HEADS
GLA_DV = GLA_VALUE // GLA_HEADS
GLA_RANK = 16
GLA_TAU = 16.0
GLA_CHUNK = 64

D_FF = 4 * D_MODEL
FF_CHUNK = 1024

EVEN_IN = 2 * RET_KEY + 2 * RET_VALUE + 2 * LRU_WIDTH
ODD_MAIN = 2 * GLA_KEY + 2 * GLA_VALUE
ODD_OUT = ODD_MAIN + GLA_KEY

LANES = 128
SUBLANES = 8
ROW_TILE = 512
FUSED_ROWS = 256
MXU_PIECE = 256
SAMPLE_SEQS = 8
VMEM_LIMIT = 56 * 1024 * 1024


def _dot(a, b):
    return jnp.dot(a.astype(BF16), b.astype(BF16), preferred_element_type=F32)


def _dot_nt(a, b):
    return lax.dot_general(a.astype(BF16), b.astype(BF16), (((1,), (1,)), ((), ())),
                           preferred_element_type=F32)


def _dot_tn(a, b):
    return lax.dot_general(a.astype(BF16), b.astype(BF16), (((0,), (0,)), ((), ())),
                           preferred_element_type=F32)


def _rmsnorm(x, g):
    return x * lax.rsqrt(jnp.mean(x * x, axis=-1, keepdims=True) + EPS) * g


def _head_rms(x, eps=EPS):
    return x * lax.rsqrt(jnp.mean(x * x, axis=-1, keepdims=True) + eps)


def _shift_rows(x, d, fill):
    row = lax.broadcasted_iota(jnp.int32, x.shape, 0)
    return jnp.where(row >= d, pltpu.roll(x, d, axis=0), fill)


def _layer_spec(shape, layer):
    n = len(shape)
    return pl.BlockSpec((None,) + tuple(shape), lambda *_: (layer,) + (0,) * n,
                        pipeline_mode=pl.Buffered(1))


def _const_spec(shape):
    n = len(shape)
    return pl.BlockSpec(shape, lambda *_: (0,) * n, pipeline_mode=pl.Buffered(1))


def _params(*semantics):
    return pltpu.CompilerParams(dimension_semantics=semantics, vmem_limit_bytes=VMEM_LIMIT)


def _scaled_log_sigmoid(z):
    return (jnp.minimum(z, 0.0) - jnp.log(1.0 + jnp.exp(-jnp.abs(z)))) / GLA_TAU


def _log_gate(h, wg1_ref, wg2_ref, bg_ref):
    glr = _dot(h, wg1_ref[...])
    return _scaled_log_sigmoid(_dot(glr, wg2_ref[...]) + bg_ref[...])


def _norm_proj_even_kernel(x_ref, g_ref, w_ref, o_ref):
    h = _rmsnorm(x_ref[...], g_ref[...])
    o_ref[...] = _dot(h, w_ref[...])


def _norm_proj_odd_kernel(x_ref, g_ref, w_ref, wg1_ref, wg2_ref, bg_ref, o_ref):
    h = _rmsnorm(x_ref[...], g_ref[...]).astype(BF16)
    o_ref[:, :ODD_MAIN] = _dot(h, w_ref[...])
    o_ref[:, ODD_MAIN:] = _log_gate(h, wg1_ref, wg2_ref, bg_ref)


def _out_mlp_kernel(final, x_ref, m_ref, wo_ref, g_ref, wu_ref, wd_ref, gf_ref, o_ref,
                    h_scr, acc_scr):
    c = pl.program_id(0)

    @pl.when(c == 0)
    def _():
        x1 = x_ref[...] + _dot(m_ref[...], wo_ref[...])
        acc_scr[...] = x1
        h_scr[...] = _rmsnorm(x1, g_ref[...]).astype(BF16)

    u = jnp.square(jnp.maximum(_dot(h_scr[...], wu_ref[...]), 0.0))
    acc_scr[...] += _dot(u, wd_ref[...])

    @pl.when(c == pl.num_programs(0) - 1)
    def _():
        x2 = acc_scr[...]
        o_ref[...] = _rmsnorm(x2, gf_ref[...]) if final else x2


def _norm_proj_even(x, w, layer):
    n = x.shape[0]
    tm = min(ROW_TILE, n)
    return pl.pallas_call(
        _norm_proj_even_kernel,
        grid=(n // tm,),
        in_specs=[pl.BlockSpec((tm, D_MODEL), lambda i: (i, 0)),
                  _layer_spec((1, D_MODEL), layer),
                  _layer_spec((D_MODEL, EVEN_IN), layer // 2)],
        out_specs=pl.BlockSpec((tm, EVEN_IN), lambda i: (i, 0)),
        out_shape=jax.ShapeDtypeStruct((n, EVEN_IN), F32),
        compiler_params=_params("parallel"),
        name="norm_proj_even",
    )(x, w["norm_mix"], w["w_in_even"])


def _norm_proj_odd(x, w, layer):
    n = x.shape[0]
    tm = min(ROW_TILE, n)
    i = layer // 2
    return pl.pallas_call(
        _norm_proj_odd_kernel,
        grid=(n // tm,),
        in_specs=[pl.BlockSpec((tm, D_MODEL), lambda i: (i, 0)),
                  _layer_spec((1, D_MODEL), layer),
                  _layer_spec((D_MODEL, ODD_MAIN), i),
                  _layer_spec((D_MODEL, LANES), i),
                  _layer_spec((LANES, GLA_KEY), i),
                  _layer_spec((1, GLA_KEY), i)],
        out_specs=pl.BlockSpec((tm, ODD_OUT), lambda i: (i, 0)),
        out_shape=jax.ShapeDtypeStruct((n, ODD_OUT), F32),
        compiler_params=_params("parallel"),
        name="norm_proj_odd",
    )(x, w["norm_mix"], w["w_in_odd"], w["wg1"], w["wg2"], w["bg"])


def _out_mlp(x, mix, w, layer):
    n = x.shape[0]
    final = layer == DEPTH - 1
    w_out = w["w_out_even"] if layer % 2 == 0 else w["w_out_odd"]
    return pl.pallas_call(
        functools.partial(_out_mlp_kernel, final),
        grid=(D_FF // FF_CHUNK,),
        in_specs=[_const_spec((n, D_MODEL)),
                  _const_spec((n, D_MODEL)),
                  _layer_spec((D_MODEL, D_MODEL), layer // 2),
                  _layer_spec((1, D_MODEL), layer),
                  pl.BlockSpec((None, D_MODEL, FF_CHUNK), lambda c: (layer, 0, c)),
                  pl.BlockSpec((None, FF_CHUNK, D_MODEL), lambda c: (layer, c, 0)),
                  _const_spec((1, D_MODEL))],
        out_specs=pl.BlockSpec((n, D_MODEL), lambda c: (0, 0)),
        out_shape=jax.ShapeDtypeStruct((n, D_MODEL), F32),
        scratch_shapes=[pltpu.VMEM((n, D_MODEL), BF16), pltpu.VMEM((n, D_MODEL), F32)],
        compiler_params=_params("arbitrary"),
        name="out_mlp_final" if final else "out_mlp",
    )(x, mix, w_out, w["norm_mlp"], w["w_up"], w["w_down"], w["norm_final"])


def _run(stream):
    for _ in stream:
        pass


def _interleave(*streams):
    done = [0.0] * len(streams)
    alive = list(range(len(streams)))
    while alive:
        i = min(alive, key=lambda j: done[j])
        try:
            done[i] += next(streams[i])
        except StopIteration:
            alive.remove(i)


class _Pace:
    def __init__(self):
        self.zero = None

    def after(self, value):
        t = value[0:1, 0:1]
        self.zero = jnp.where(t == t, 0.0, t)

    def __call__(self, x):
        return x if self.zero is None else x + self.zero


def _no_pace(x):
    return x


def _even_mixer_tile(C, chunks, pos_base, load, store, cos_ref, sin_ref,
                     s_src, s_dst, h_src, h_dst, tail_ref, weights, pace=_no_pace):
    cw_ref, cbias_ref, wa_ref, ba_ref, wi_ref, bi_ref, lam_ref = weights
    lane = lax.broadcasted_iota(jnp.int32, (C, RET_KEY), 1)
    first_half = (lane % RET_DK) < (RET_DK // 2)
    ri = lax.broadcasted_iota(jnp.int32, (C, C), 0)
    ci = lax.broadcasted_iota(jnp.int32, (C, C), 1)
    rel = (ri - ci).astype(F32)
    row = lax.broadcasted_iota(jnp.int32, (C, 1), 0).astype(F32)
    trow = lax.broadcasted_iota(jnp.int32, (C, LRU_WIDTH), 0)
    row8 = lax.broadcasted_iota(jnp.int32, (SUBLANES, LRU_WIDTH), 0)
    lam = lam_ref[...]
    softplus_neg_lam = jnp.maximum(-lam, 0.0) + jnp.log(1.0 + jnp.exp(-jnp.abs(lam)))
    log_decay = [math.log1p(-(2.0 ** (-5.0 - h))) for h in range(RET_HEADS)]
    decay_mask = [jnp.where(rel >= 0, jnp.exp(jnp.maximum(rel, 0.0) * lg), 0.0)
                  for lg in log_decay]
    q_decay = [jnp.exp((row + 1.0) * lg) for lg in log_decay]
    k_decay = [jnp.exp((C - 1.0 - row) * lg) for lg in log_decay]
    n_scan = max(1, int(math.log2(C)))

    v0 = 2 * RET_KEY
    g0 = v0 + RET_VALUE
    x0 = g0 + RET_VALUE
    xg0 = x0 + LRU_WIDTH
    heads = range(RET_HEADS)
    J = range(chunks)
    rows = [slice(j * C, (j + 1) * C) for j in J]

    def delayed(x, t, n):
        rolled = pltpu.roll(x, n, axis=0)
        head = jnp.where(row8 < n, pltpu.roll(t, n, axis=0), rolled[0:SUBLANES, :])
        if C == SUBLANES:
            return head
        return jnp.concatenate([head, rolled[SUBLANES:, :]], axis=0)

    qh, kh, q_dec, k_dec, vh, xc = ([None] * chunks for _ in range(6))
    tail = tail_ref[...]
    for j in J:
        cos = cos_ref[rows[j], :]
        sin = sin_ref[rows[j], :]

        def rope(x):
            partner = jnp.where(first_half,
                                pltpu.roll(x, RET_KEY - RET_DK // 2, axis=1),
                                pltpu.roll(x, RET_DK // 2, axis=1))
            return x * cos + partner * sin

        q = rope(load(rows[j], slice(0, RET_KEY)))
        k = rope(load(rows[j], slice(RET_KEY, 2 * RET_KEY))) * (RET_DK ** -0.5)
        q = [q[:, h * RET_DK:(h + 1) * RET_DK] for h in heads]
        k = [k[:, h * RET_DK:(h + 1) * RET_DK] for h in heads]
        q_dec[j] = [(q[h] * q_decay[h]).astype(BF16) for h in heads]
        k_dec[j] = [(k[h] * k_decay[h]).astype(BF16) for h in heads]
        qh[j] = [q[h].astype(BF16) for h in heads]
        kh[j] = [k[h].astype(BF16) for h in heads]
        vh[j] = [load(rows[j], slice(v0 + h * RET_DV, v0 + (h + 1) * RET_DV)).astype(BF16)
                 for h in heads]
        yield 0.08 / chunks

        x_new = load(rows[j], slice(x0, x0 + LRU_WIDTH))
        acc = pace(cbias_ref[...])
        for i in range(CONV_W - 1):
            acc = acc + delayed(x_new, tail, CONV_W - 1 - i) * cw_ref[i:i + 1, :]
        xc[j] = acc + x_new * cw_ref[CONV_W - 1:CONV_W, :]
        tail = x_new[C - SUBLANES:C, :]
        yield 0.10 / chunks
    tail_ref[...] = tail

    scores = [[_dot_nt(qh[j][h], kh[j][h]) for h in heads] for j in J]
    kv = [[_dot_tn(k_dec[j][h], vh[j][h]) for h in heads] for j in J]
    xc_all = xc[0] if chunks == 1 else jnp.concatenate(xc, axis=0)
    xc_all = xc_all.astype(BF16)
    diag = [slice(c, c + MXU_PIECE) for c in range(0, LRU_WIDTH, MXU_PIECE)]
    r_all = jnp.concatenate([_dot(xc_all[:, d], wa_ref[d, d]) for d in diag], axis=1)
    i_all = jnp.concatenate([_dot(xc_all[:, d], wi_ref[d, d]) for d in diag], axis=1)
    r_pre = [r_all[rows[j], :] for j in J]
    i_pre = [i_all[rows[j], :] for j in J]
    yield 0.0

    probs, s_in = [None] * chunks, [None] * chunks
    s_cur = [s_src[h] for h in heads]
    for j in J:
        probs[j] = [(scores[j][h] * decay_mask[h]).astype(BF16) for h in heads]
        s_in[j] = [s_cur[h].astype(BF16) for h in heads]
        s_cur = [s_cur[h] * math.exp(C * log_decay[h]) + kv[j][h] for h in heads]
        yield 0.06 / chunks
    for h in heads:
        s_dst[h] = s_cur[h]

    a_cum, b_loc = [None] * chunks, [None] * chunks
    for j in J:
        r = jax.nn.sigmoid(r_pre[j] + pace(ba_ref[...]))
        gate_i = jax.nn.sigmoid(i_pre[j] + bi_ref[...])
        log_a = -LRU_C * r * softplus_neg_lam
        a = jnp.exp(log_a)
        y = -jnp.tanh(log_a) * (a * a + 1.0)
        mult = jnp.where(y > 0.0, y * lax.rsqrt(y), 0.0)
        pos = trow + (pos_base + j * C)
        mult = jnp.where(pos == 0, 1.0, mult)
        b = mult * (gate_i * xc[j])
        yield 0.20 / chunks
        d = 1
        while d < C:
            b = a * _shift_rows(b, d, pace(0.0)) + b
            a = a * _shift_rows(a, d, pace(1.0))
            d *= 2
            yield 0.36 / (chunks * n_scan)
        a_cum[j], b_loc[j] = a, b

    out = [[_dot(probs[j][h], vh[j][h]) + _dot(q_dec[j][h], s_in[j][h]) for h in heads]
           for j in J]
    yield 0.0

    carry = h_src[...]
    for j in J:
        hidden = b_loc[j] + a_cum[j] * carry
        carry = hidden[C - 1:C, :]
        xg = load(rows[j], slice(xg0, xg0 + LRU_WIDTH))
        store(rows[j], slice(RET_VALUE, RET_VALUE + LRU_WIDTH), hidden * jax.nn.gelu(xg))
        yield 0.08 / chunks
        for h in heads:
            gh = load(rows[j], slice(g0 + h * RET_DV, g0 + (h + 1) * RET_DV))
            store(rows[j], slice(h * RET_DV, (h + 1) * RET_DV),
                  jax.nn.silu(gh) * _head_rms(out[j][h], pace(EPS)))
            yield 0.03 / chunks
    h_dst[...] = carry


def _gla_mixer_tile(C, chunks, load, store, s_src, s_dst, nw, pace=_no_pace):
    k0 = GLA_KEY
    v0 = 2 * GLA_KEY
    r0 = v0 + GLA_VALUE
    la0 = r0 + GLA_VALUE
    ri = lax.broadcasted_iota(jnp.int32, (C, C), 0)
    ci = lax.broadcasted_iota(jnp.int32, (C, C), 1)
    causal = ri >= ci
    di = lax.broadcasted_iota(jnp.int32, (GLA_DK, GLA_DK), 0)
    dj = lax.broadcasted_iota(jnp.int32, (GLA_DK, GLA_DK), 1)
    eye = di == dj
    heads = range(GLA_HEADS)
    key = [slice(h * GLA_DK, (h + 1) * GLA_DK) for h in heads]
    J = range(chunks)
    rows = [slice(j * C, (j + 1) * C) for j in J]

    q_in, k_in, q_st, k_st, decay_col, vh = ([None] * chunks for _ in range(6))
    for j in J:
        b = load(rows[j], slice(la0, la0 + GLA_KEY))
        d = 1
        while d < C:
            b = b + _shift_rows(b, d, pace(0.0))
            d *= 2
        yield 0.12 / chunks
        b_mid = pace(b[C // 2:C // 2 + 1, :])
        b_last = b[C - 1:C, :]
        q = load(rows[j], slice(0, GLA_KEY)) * (GLA_DK ** -0.5)
        k = load(rows[j], slice(k0, k0 + GLA_KEY))
        q_in[j] = (q * jnp.exp(b - b_mid)).astype(BF16)
        k_in[j] = (k * jnp.exp(b_mid - b)).astype(BF16)
        q_st[j] = (q * jnp.exp(b)).astype(BF16)
        k_st[j] = (k * jnp.exp(b_last - b)).astype(BF16)
        s_decay = jnp.exp(b_last)
        decay_col[j] = [jnp.sum(jnp.where(eye, s_decay[:, key[h]], 0.0), axis=1, keepdims=True)
                        for h in heads]
        vh[j] = [load(rows[j], slice(v0 + h * GLA_DV, v0 + (h + 1) * GLA_DV)).astype(BF16)
                 for h in heads]
        yield 0.28 / chunks

    scores = [[_dot_nt(q_in[j][:, key[h]], k_in[j][:, key[h]]) for h in heads] for j in J]
    kv = [[_dot_tn(k_st[j][:, key[h]], vh[j][h]) for h in heads] for j in J]
    yield 0.0

    probs, s_in = [None] * chunks, [None] * chunks
    s_cur = [s_src[h] for h in heads]
    for j in J:
        probs[j] = [jnp.where(causal, scores[j][h], 0.0).astype(BF16) for h in heads]
        s_in[j] = [s_cur[h].astype(BF16) for h in heads]
        s_cur = [s_cur[h] * decay_col[j][h] + kv[j][h] for h in heads]
        yield 0.16 / chunks
    for h in heads:
        s_dst[h] = s_cur[h]

    out = [[_dot(probs[j][h], vh[j][h]) + _dot(q_st[j][:, key[h]], s_in[j][h]) for h in heads]
           for j in J]
    yield 0.0

    for j in J:
        for h in heads:
            rh = load(rows[j], slice(r0 + h * GLA_DV, r0 + (h + 1) * GLA_DV))
            store(rows[j], slice(h * GLA_DV, (h + 1) * GLA_DV),
                  jax.nn.silu(rh) * (_head_rms(out[j][h], pace(EPS)) * nw))
            yield 0.11 / chunks


def _even_mixer_kernel(pos0, chunk, seqs, *refs):
    (proj_ref, cos_ref, sin_ref, s0_ref, h0_ref, cb_ref,
     cw_ref, cbias_ref, wa_ref, ba_ref, wi_ref, bi_ref, lam_ref) = refs[:13]
    refs = refs[14:]
    mix_ref, snew_ref, hnew_ref, cnew_ref, tail_scr = refs
    weights = (cw_ref, cbias_ref, wa_ref, ba_ref, wi_ref, bi_ref, lam_ref)
    for g in range(seqs):
        tail = tail_scr.at[g]
        tail[...] = jnp.zeros((SUBLANES, LRU_WIDTH), F32)
        tail[SUBLANES - (CONV_W - 1):SUBLANES, :] = cb_ref[g]

        def load(rows, cols, g=g):
            return proj_ref[g, rows, cols]

        def store(rows, cols, val, g=g):
            mix_ref[g, rows, cols] = val

        _run(_even_mixer_tile(chunk, 1, pos0, load, store, cos_ref, sin_ref,
                              s0_ref.at[g], snew_ref.at[g], h0_ref.at[g], hnew_ref.at[g],
                              tail, weights))
        cnew_ref[g] = tail[SUBLANES - (CONV_W - 1):SUBLANES, :]


def _even_mixer(proj, pos0, cos, sin, ret_s, lru_h, conv_buf, w, layer, prev_ret):
    B, T, _ = proj.shape
    i = layer // 2
    n_even = w["w_in_even"].shape[0]
    G = SAMPLE_SEQS
    ret_block = (None, G, RET_HEADS, RET_DK, RET_DV)
    in_specs = [pl.BlockSpec((G, T, EVEN_IN), lambda b: (b, 0, 0)),
                _const_spec((T, RET_KEY)),
                _const_spec((T, RET_KEY)),
                pl.BlockSpec(ret_block, lambda b: (i, b, 0, 0, 0)),
                pl.BlockSpec((None, G, 1, LRU_WIDTH), lambda b: (i, b, 0, 0)),
                pl.BlockSpec((None, G, CONV_W - 1, LRU_WIDTH), lambda b: (i, b, 0, 0)),
                _layer_spec((CONV_W, LRU_WIDTH), i),
                _layer_spec((1, LRU_WIDTH), i),
                _layer_spec((LRU_WIDTH, LRU_WIDTH), i),
                _layer_spec((1, LRU_WIDTH), i),
                _layer_spec((LRU_WIDTH, LRU_WIDTH), i),
                _layer_spec((1, LRU_WIDTH), i),
                _layer_spec((1, LRU_WIDTH), i)]
    args = [proj, cos, sin, ret_s, lru_h, conv_buf, w["conv_w"], w["conv_b"],
            w["wa"], w["ba"], w["wi"], w["bi"], w["lam"]]
    in_specs.append(pl.BlockSpec(memory_space=pl.ANY))
    args.append(prev_ret)
    aliases = {len(args) - 1: 1}
    out_shapes = (jax.ShapeDtypeStruct((B, T, D_MODEL), F32),
                  jax.ShapeDtypeStruct((n_even, B, RET_HEADS, RET_DK, RET_DV), F32),
                  jax.ShapeDtypeStruct((B, 1, LRU_WIDTH), F32),
                  jax.ShapeDtypeStruct((B, CONV_W - 1, LRU_WIDTH), F32))
    return pl.pallas_call(
        functools.partial(_even_mixer_kernel, pos0, T, G),
        grid=(B // G,),
        in_specs=in_specs,
        out_specs=(pl.BlockSpec((G, T, D_MODEL), lambda b: (b, 0, 0)),
                   pl.BlockSpec(ret_block, lambda b: (i, b, 0, 0, 0)),
                   pl.BlockSpec((G, 1, LRU_WIDTH), lambda b: (b, 0, 0)),
                   pl.BlockSpec((G, CONV_W - 1, LRU_WIDTH), lambda b: (b, 0, 0))),
        out_shape=out_shapes,
        scratch_shapes=[pltpu.VMEM((G, SUBLANES, LRU_WIDTH), F32)],
        input_output_aliases=aliases,
        compiler_params=_params("parallel"),
        name="even_mixer",
    )(*args)


def _gla_mixer_kernel(chunk, seqs, *refs):
    proj_ref, s0_ref, nw_ref = refs[:3]
    refs = refs[4:]
    mix_ref, snew_ref = refs
    nw = nw_ref[...]
    for g in range(seqs):
        def load(rows, cols, g=g):
            return proj_ref[g, rows, cols]

        def store(rows, cols, val, g=g):
            mix_ref[g, rows, cols] = val

        _run(_gla_mixer_tile(chunk, 1, load, store, s0_ref.at[g], snew_ref.at[g], nw))


def _gla_mixer(proj, gla_s, w, layer, prev_gla):
    B, T, _ = proj.shape
    i = layer // 2
    n_odd = w["w_in_odd"].shape[0]
    G = SAMPLE_SEQS
    state = (None, G, GLA_HEADS, GLA_DK, GLA_DV)
    in_specs = [pl.BlockSpec((G, T, ODD_OUT), lambda b: (b, 0, 0)),
                pl.BlockSpec(state, lambda b: (i, b, 0, 0, 0)),
                _layer_spec((1, GLA_DV), i)]
    args = [proj, gla_s, w["gla_norm"]]
    in_specs.append(pl.BlockSpec(memory_space=pl.ANY))
    args.append(prev_gla)
    aliases = {len(args) - 1: 1}
    return pl.pallas_call(
        functools.partial(_gla_mixer_kernel, T, G),
        grid=(B // G,),
        in_specs=in_specs,
        out_specs=(pl.BlockSpec((G, T, D_MODEL), lambda b: (b, 0, 0)),
                   pl.BlockSpec(state, lambda b: (i, b, 0, 0, 0))),
        out_shape=(jax.ShapeDtypeStruct((B, T, D_MODEL), F32),
                   jax.ShapeDtypeStruct((n_odd, B, GLA_HEADS, GLA_DK, GLA_DV), F32)),
        input_output_aliases=aliases,
        compiler_params=_params("parallel"),
        name="gla_mixer",
    )(*args)


def _fused_layer_kernel(even, final, chunk, rows, tiles_per_seq, n_tiles, slot, has_prev, *refs):
    it = iter(refs)
    xlead_ref, xlag_ref, gmix_ref, win_ref = (next(it) for _ in range(4))
    if even:
        cos_ref, sin_ref = next(it), next(it)
        mixer_weights = tuple(next(it) for _ in range(7))
    else:
        wg1_ref, wg2_ref, bg_ref, nw_ref = (next(it) for _ in range(4))
    wo_ref, gmlp_ref, wu_ref, wd_ref, gf_ref = (next(it) for _ in range(5))
    if has_prev:
        next(it)
    out_ref, snew_ref = next(it), next(it)
    if even:
        hnew_ref, cnew_ref = next(it), next(it)
    if not has_prev:
        zeros_ref = next(it)
    proj_scr = (next(it), next(it))
    mix_scr = (next(it), next(it))
    x1_scr, u_scr, s_scr = next(it), next(it), next(it)
    if even:
        h_scr, tail_scr = next(it), next(it)

    s = pl.program_id(0)
    tile_in_seq = lax.rem(jnp.clip(s - 1, 0, n_tiles - 1), tiles_per_seq)

    @pl.when(lax.rem(s + tiles_per_seq - 1, tiles_per_seq) == 0)
    def _():
        s_scr[...] = jnp.zeros(s_scr.shape, F32)
        if even:
            h_scr[...] = jnp.zeros(h_scr.shape, F32)
            tail_scr[...] = jnp.zeros(tail_scr.shape, F32)

    main_cols = EVEN_IN if even else ODD_MAIN
    macs_per_row = D_MODEL * (main_cols + D_MODEL + 2 * D_FF)

    def matrix_stream(proj_a, mix_c, pace, do_a, do_c):
        P = MXU_PIECE
        if do_a:
            h_in = _rmsnorm(xlead_ref[...], gmix_ref[...]).astype(BF16)
        if do_c:
            m = mix_c[...]
            for n0 in range(0, D_MODEL, P):
                x1 = xlag_ref[:, n0:n0 + P] + _dot(m, wo_ref[:, n0:n0 + P])
                x1_scr[:, n0:n0 + P] = x1
                pace.after(x1)
                yield D_MODEL * P / macs_per_row
            h_mlp = _rmsnorm(x1_scr[...], gmlp_ref[...]).astype(BF16)
        if do_a:
            if not even:
                glr = _dot(h_in, wg1_ref[...]).astype(BF16)
            for n0 in range(0, main_cols, P):
                p = _dot(h_in, win_ref[:, n0:n0 + P])
                proj_a[:, n0:n0 + P] = p
                pace.after(p)
                if not even and n0 == P:
                    z = _dot(glr, wg2_ref[...]) + bg_ref[...]
                yield D_MODEL * P / macs_per_row
            if not even:
                proj_a[:, ODD_MAIN:] = _scaled_log_sigmoid(z)
                yield 0.0
        if do_c:
            for n0 in range(0, D_FF, P):
                u = jnp.square(jnp.maximum(_dot(h_mlp, wu_ref[:, n0:n0 + P]), 0.0))
                u_scr[:, n0:n0 + P] = u.astype(BF16)
                pace.after(u)
                yield D_MODEL * P / macs_per_row
            for n0 in range(0, D_MODEL, P):
                x2 = x1_scr[:, n0:n0 + P] + _dot(u_scr[...], wd_ref[:, n0:n0 + P])
                if final:
                    x1_scr[:, n0:n0 + P] = x2
                else:
                    out_ref[:, n0:n0 + P] = x2
                pace.after(x2)
                yield D_FF * P / macs_per_row
            if final:
                out_ref[...] = _rmsnorm(x1_scr[...], gf_ref[...])

    def step(par, do_a=True, do_b=True, do_c=True):
        proj_a, proj_b = proj_scr[par], proj_scr[1 - par]
        mix_b, mix_c = mix_scr[1 - par], mix_scr[par]

        def load(r, c):
            return proj_b[r, c]

        def store(r, c, val):
            mix_b[r, c] = val.astype(BF16)

        pace = _Pace()
        streams = [matrix_stream(proj_a, mix_c, pace, do_a, do_c)]
        if do_b and even:
            streams.append(_even_mixer_tile(chunk, rows // chunk, tile_in_seq * rows, load, store,
                                            cos_ref, sin_ref, s_scr, s_scr, h_scr, h_scr,
                                            tail_scr, mixer_weights, pace))
        elif do_b:
            streams.append(_gla_mixer_tile(chunk, rows // chunk, load, store, s_scr, s_scr,
                                           nw_ref[...], pace))
        _interleave(*streams)

    parity = lax.rem(s, 2)
    steady = (s >= 2) & (s < n_tiles)
    pl.when(s == 0)(functools.partial(step, 0, do_b=False, do_c=False))
    pl.when(s == 1)(functools.partial(step, 1, do_c=False))
    pl.when(steady & (parity == 0))(functools.partial(step, 0))
    pl.when(steady & (parity == 1))(functools.partial(step, 1))
    pl.when(s == n_tiles)(functools.partial(step, 0, do_a=False))
    pl.when(s == n_tiles + 1)(functools.partial(step, 1, do_a=False, do_b=False))

    if not has_prev:
        zeros_ref[...] = jnp.zeros(zeros_ref.shape, F32)

    @pl.when((s >= 1) & (s <= n_tiles) & (lax.rem(s, tiles_per_seq) == 0))
    def _():
        if has_prev:
            snew_ref[0] = s_scr[...]
        else:
            for other in range(snew_ref.shape[0]):
                snew_ref[other, 0] = s_scr[...] if other == slot else jnp.zeros(s_scr.shape, F32)
        if even:
            hnew_ref[0] = h_scr[...]
            cnew_ref[0] = tail_scr[SUBLANES - (CONV_W - 1):SUBLANES, :]


def _fused_layer(x, w, layer, cos, sin, prev_state, seq_len, n_short):
    n_rows = x.shape[0]
    B = n_rows // seq_len
    R = FUSED_ROWS
    even = layer % 2 == 0
    final = layer == DEPTH - 1
    i = layer // 2
    C = math.gcd(seq_len, RET_CHUNK if even else GLA_CHUNK)
    tps = seq_len // R
    NT = n_rows // R
    assert NT % 2 == 0 and NT > 2
    has_prev = prev_state is not None

    def lead(s):
        return (jnp.minimum(s, NT - 1), 0)

    def mid_tile(s):
        return jnp.clip(s - 1, 0, NT - 1)

    def lag(s):
        return (jnp.clip(s - 2, 0, NT - 1), 0)

    def seq3(s):
        return (mid_tile(s) // tps, 0, 0)

    in_specs = [pl.BlockSpec((R, D_MODEL), lead),
                pl.BlockSpec((R, D_MODEL), lag),
                _layer_spec((1, D_MODEL), layer)]
    args = [x, x, w["norm_mix"]]
    if even:
        in_specs += [_layer_spec((D_MODEL, EVEN_IN), i),
                     pl.BlockSpec((R, RET_KEY), lambda s: (mid_tile(s) % tps, 0)),
                     pl.BlockSpec((R, RET_KEY), lambda s: (mid_tile(s) % tps, 0)),
                     _layer_spec((CONV_W, LRU_WIDTH), i),
                     _layer_spec((1, LRU_WIDTH), i),
                     _layer_spec((LRU_WIDTH, LRU_WIDTH), i),
                     _layer_spec((1, LRU_WIDTH), i),
                     _layer_spec((LRU_WIDTH, LRU_WIDTH), i),
                     _layer_spec((1, LRU_WIDTH), i),
                     _layer_spec((1, LRU_WIDTH), i)]
        args += [w["w_in_even"], cos, sin, w["conv_w"], w["conv_b"], w["wa"], w["ba"],
                 w["wi"], w["bi"], w["lam"]]
        w_out = w["w_out_even"]
        state_shape = (RET_HEADS, RET_DK, RET_DV)
        n_stack = w["w_in_even"].shape[0]
        proj_cols = EVEN_IN
    else:
        in_specs += [_layer_spec((D_MODEL, ODD_MAIN), i),
                     _layer_spec((D_MODEL, LANES), i),
                     _layer_spec((LANES, GLA_KEY), i),
                     _layer_spec((1, GLA_KEY), i),
                     _layer_spec((1, GLA_DV), i)]
        args += [w["w_in_odd"], w["wg1"], w["wg2"], w["bg"], w["gla_norm"]]
        w_out = w["w_out_odd"]
        state_shape = (GLA_HEADS, GLA_DK, GLA_DV)
        n_stack = w["w_in_odd"].shape[0]
        proj_cols = ODD_OUT
    in_specs += [_layer_spec((D_MODEL, D_MODEL), i),
                 _layer_spec((1, D_MODEL), layer),
                 _layer_spec((D_MODEL, D_FF), layer),
                 _layer_spec((D_FF, D_MODEL), layer),
                 _const_spec((1, D_MODEL))]
    args += [w_out, w["norm_mlp"], w["w_up"], w["w_down"], w["norm_final"]]
    aliases = {}
    if has_prev:
        in_specs.append(pl.BlockSpec(memory_space=pl.ANY))
        args.append(prev_state)
        aliases = {len(args) - 1: 1}

    if has_prev:
        state_spec = pl.BlockSpec((None, 1) + state_shape,
                                  lambda s: (i, mid_tile(s) // tps, 0, 0, 0))
    else:
        state_spec = pl.BlockSpec((n_stack, 1) + state_shape,
                                  lambda s: (0, mid_tile(s) // tps, 0, 0, 0))
    out_specs = [pl.BlockSpec((R, D_MODEL), lag), state_spec]
    out_shapes = [jax.ShapeDtypeStruct((n_rows, D_MODEL), F32),
                  jax.ShapeDtypeStruct((n_stack, B) + state_shape, F32)]
    scratch = [pltpu.VMEM((R, proj_cols), F32), pltpu.VMEM((R, proj_cols), F32),
               pltpu.VMEM((R, D_MODEL), BF16), pltpu.VMEM((R, D_MODEL), BF16),
               pltpu.VMEM((R, D_MODEL), F32), pltpu.VMEM((R, D_FF), BF16),
               pltpu.VMEM(state_shape, F32)]
    if even:
        out_specs += [pl.BlockSpec((1, 1, LRU_WIDTH), seq3),
                      pl.BlockSpec((1, CONV_W - 1, LRU_WIDTH), seq3)]
        out_shapes += [jax.ShapeDtypeStruct((B, 1, LRU_WIDTH), F32),
                       jax.ShapeDtypeStruct((B, CONV_W - 1, LRU_WIDTH), F32)]
        scratch += [pltpu.VMEM((1, LRU_WIDTH), F32),
                    pltpu.VMEM((SUBLANES, LRU_WIDTH), F32)]
    if not has_prev:
        per_step = n_short // NT
        assert per_step * NT == n_short
        out_specs.append(pl.BlockSpec((n_stack, per_step) + state_shape,
                                      lambda s: (0, jnp.minimum(s, NT - 1), 0, 0, 0)))
        out_shapes.append(jax.ShapeDtypeStruct((n_stack, n_short) + state_shape, F32))
    return pl.pallas_call(
        functools.partial(_fused_layer_kernel, even, final, C, R, tps, NT, i, has_prev),
        grid=(NT + 2,),
        in_specs=in_specs,
        out_specs=tuple(out_specs),
        out_shape=tuple(out_shapes),
        scratch_shapes=scratch,
        input_output_aliases=aliases,
        compiler_params=_params("arbitrary"),
        name="layer_even" if even else "layer_odd",
    )(*args)


def _rope_tables(pos0, T):
    half = RET_DK // 2
    inv = ROPE_BASE ** (-jnp.arange(half, dtype=F32) / half)
    ang = (pos0 + jnp.arange(T, dtype=jnp.int32)).astype(F32)[:, None] * inv[None, :]
    cos = jnp.cos(ang)
    sin = jnp.sin(ang)
    cos_full = jnp.tile(jnp.concatenate([cos, cos], axis=-1), (1, RET_HEADS))
    sin_signed = jnp.tile(jnp.concatenate([-sin, sin], axis=-1), (1, RET_HEADS))
    return cos_full, sin_signed


def _block_diag(w):
    nl, nb, c, d = w.shape
    eye = jnp.eye(nb, dtype=w.dtype)
    return (eye[None, :, None, :, None] * w[:, :, :, None, :]).reshape(nl, nb * c, nb * d)


def _prepare_weights(norm_mix, norm_mlp, norm_final, w_in_even, w_out_even, conv_w, conv_b,
                     lru_w_a, lru_b_a, lru_w_i, lru_b_i, lru_lambda,
                     w_in_odd, gla_w_gate2, gla_b_gate, gla_norm, w_out_odd, w_up, w_down):
    n_even = w_in_even.shape[0]
    n_odd = w_in_odd.shape[0]
    pad1 = LANES - GLA_RANK
    return dict(
        norm_mix=norm_mix.reshape(DEPTH, 1, D_MODEL),
        norm_mlp=norm_mlp.reshape(DEPTH, 1, D_MODEL),
        norm_final=norm_final.reshape(1, D_MODEL),
        w_in_even=w_in_even.astype(BF16),
        w_out_even=w_out_even.astype(BF16),
        conv_w=conv_w,
        conv_b=conv_b.reshape(n_even, 1, LRU_WIDTH),
        wa=_block_diag(lru_w_a).astype(BF16),
        ba=lru_b_a.reshape(n_even, 1, LRU_WIDTH),
        wi=_block_diag(lru_w_i).astype(BF16),
        bi=lru_b_i.reshape(n_even, 1, LRU_WIDTH),
        lam=lru_lambda.reshape(n_even, 1, LRU_WIDTH),
        w_in_odd=w_in_odd.astype(BF16),
        wg1=jnp.pad(w_in_odd[:, :, ODD_MAIN:], ((0, 0), (0, 0), (0, pad1))).astype(BF16),
        wg2=jnp.pad(gla_w_gate2, ((0, 0), (0, pad1), (0, 0))).astype(BF16),
        bg=gla_b_gate.reshape(n_odd, 1, GLA_KEY),
        gla_norm=gla_norm.reshape(n_odd, 1, GLA_DV),
        w_out_odd=w_out_odd.astype(BF16),
        w_up=w_up.astype(BF16),
        w_down=w_down.astype(BF16),
    )


def _prompt_trunk(x, w, n_short):
    B, T, _ = x.shape
    xf = x.reshape(B * T, D_MODEL)
    cos, sin = _rope_tables(0, T)
    ret_new, gla_new = None, None
    lrus, convs, zeros = [], [], []
    for layer in range(DEPTH):
        prev = ret_new if layer % 2 == 0 else gla_new
        outs = _fused_layer(xf, w, layer, cos, sin, prev, T, n_short)
        if prev is None:
            zeros.append(outs[-1])
            outs = outs[:-1]
        if layer % 2 == 0:
            xf, ret_new, lh, cb = outs
            lrus.append(lh.reshape(B, LRU_WIDTH))
            convs.append(cb)
        else:
            xf, gla_new = outs
    states = (ret_new, jnp.stack(lrus), jnp.stack(convs), gla_new)
    return xf.reshape(B, T, D_MODEL), states, zeros


def _sample_trunk(x, pos0, ret_s, lru_h, conv_buf, gla_s, ret_new, gla_new, w):
    B, T, _ = x.shape
    xf = x.reshape(B * T, D_MODEL)
    cos, sin = _rope_tables(pos0, T)
    lru_h = lru_h.reshape(lru_h.shape[0], B, 1, LRU_WIDTH)
    lrus, convs = [], []
    for layer in range(DEPTH):
        if layer % 2 == 0:
            proj = _norm_proj_even(xf, w, layer)
            mix, ret_new, lh, cb = _even_mixer(proj.reshape(B, T, EVEN_IN), pos0, cos, sin,
                                               ret_s, lru_h, conv_buf, w, layer, ret_new)
            lrus.append(lh.reshape(B, LRU_WIDTH))
            convs.append(cb)
        else:
            proj = _norm_proj_odd(xf, w, layer)
            mix, gla_new = _gla_mixer(proj.reshape(B, T, ODD_OUT), gla_s, w, layer, gla_new)
        xf = _out_mlp(xf, mix.reshape(B * T, D_MODEL), w, layer)
    return xf.reshape(B, T, D_MODEL), ret_new, jnp.stack(lrus), jnp.stack(convs), gla_new


def kernel(x_prompt, x_sample, state_ret, state_lru, state_conv, state_gla, norm_mix, norm_mlp, norm_final, w_in_even, w_out_even, conv_w, conv_b, lru_w_a, lru_b_a, lru_w_i, lru_b_i, lru_lambda, w_in_odd, gla_w_gate2, gla_b_gate, gla_norm, w_out_odd, w_up, w_down):
    w = _prepare_weights(norm_mix, norm_mlp, norm_final, w_in_even, w_out_even, conv_w, conv_b,
                         lru_w_a, lru_b_a, lru_w_i, lru_b_i, lru_lambda,
                         w_in_odd, gla_w_gate2, gla_b_gate, gla_norm, w_out_odd, w_up, w_down)
    y_p, (ret_p, lru_p, conv_p, gla_p), (ret_zero, gla_zero) = _prompt_trunk(
        x_prompt, w, x_sample.shape[0])
    y_s, ret_s, lru_s, conv_s, gla_s = _sample_trunk(x_sample, PAST_LEN, state_ret, state_lru,
                                                     state_conv, state_gla, ret_zero, gla_zero, w)
    return (y_p, y_s, ret_p, ret_s, lru_p, lru_s, conv_p, conv_s, gla_p, gla_s)
```

```python
import functools
import math

import jax
import jax.numpy as jnp
from jax import lax
from jax.experimental import pallas as pl
from jax.experimental.pallas import tpu as pltpu

F32 = jnp.float32
BF16 = jnp.bfloat16

D_MODEL = 1024
DEPTH = 4
PAST_LEN = 16384
EPS = 1e-6
ROPE_BASE = 10000.0

RET_HEADS = 4
RET_DK = 64
RET_DV = 128
RET_KEY = RET_HEADS * RET_DK
RET_VALUE = RET_HEADS * RET_DV
RET_CHUNK = 128

LRU_WIDTH = 512
CONV_W = 4
LRU_C = 8.0

GLA_HEADS = 4
GLA_KEY = 512
GLA_VALUE = 1024
GLA_DK = GLA_KEY // GLA_HEADS
GLA_DV = GLA_VALUE // GLA_HEADS
GLA_RANK = 16
GLA_TAU = 16.0
GLA_CHUNK = 64

D_FF = 4 * D_MODEL
FF_CHUNK = 1024

EVEN_IN = 2 * RET_KEY + 2 * RET_VALUE + 2 * LRU_WIDTH
ODD_MAIN = 2 * GLA_KEY + 2 * GLA_VALUE
ODD_OUT = ODD_MAIN + GLA_KEY

LANES = 128
SUBLANES = 8
ROW_TILE = 512
FUSED_ROWS = 256
MXU_PIECE = 256
SAMPLE_SEQS = 16
VMEM_LIMIT = 56 * 1024 * 1024


def _dot(a, b):
    return jnp.dot(a.astype(BF16), b.astype(BF16), preferred_element_type=F32)


def _dot_nt(a, b):
    return lax.dot_general(a.astype(BF16), b.astype(BF16), (((1,), (1,)), ((), ())),
                           preferred_element_type=F32)


def _dot_tn(a, b):
    return lax.dot_general(a.astype(BF16), b.astype(BF16), (((0,), (0,)), ((), ())),
                           preferred_element_type=F32)


def _rmsnorm(x, g):
    return x * lax.rsqrt(jnp.mean(x * x, axis=-1, keepdims=True) + EPS) * g


def _head_rms(x, eps=EPS):
    return x * lax.rsqrt(jnp.mean(x * x, axis=-1, keepdims=True) + eps)


def _shift_rows(x, d, fill):
    row = lax.broadcasted_iota(jnp.int32, x.shape, 0)
    return jnp.where(row >= d, pltpu.roll(x, d, axis=0), fill)


def _layer_spec(shape, layer):
    n = len(shape)
    return pl.BlockSpec((None,) + tuple(shape), lambda *_: (layer,) + (0,) * n,
                        pipeline_mode=pl.Buffered(1))


def _const_spec(shape):
    n = len(shape)
    return pl.BlockSpec(shape, lambda *_: (0,) * n, pipeline_mode=pl.Buffered(1))


def _params(*semantics):
    return pltpu.CompilerParams(dimension_semantics=semantics, vmem_limit_bytes=VMEM_LIMIT)


def _scaled_log_sigmoid(z):
    return (jnp.minimum(z, 0.0) - jnp.log(1.0 + jnp.exp(-jnp.abs(z)))) / GLA_TAU


def _log_gate(h, wg1_ref, wg2_ref, bg_ref):
    glr = _dot(h, wg1_ref[...])
    return _scaled_log_sigmoid(_dot(glr, wg2_ref[...]) + bg_ref[...])


def _norm_proj_even_kernel(x_ref, g_ref, w_ref, o_ref):
    h = _rmsnorm(x_ref[...], g_ref[...])
    o_ref[...] = _dot(h, w_ref[...])


def _norm_proj_odd_kernel(x_ref, g_ref, w_ref, wg1_ref, wg2_ref, bg_ref, o_ref):
    h = _rmsnorm(x_ref[...], g_ref[...]).astype(BF16)
    o_ref[:, :ODD_MAIN] = _dot(h, w_ref[...])
    o_ref[:, ODD_MAIN:] = _log_gate(h, wg1_ref, wg2_ref, bg_ref)


def _out_mlp_kernel(final, x_ref, m_ref, wo_ref, g_ref, wu_ref, wd_ref, gf_ref, o_ref,
                    h_scr, acc_scr):
    c = pl.program_id(0)

    @pl.when(c == 0)
    def _():
        x1 = x_ref[...] + _dot(m_ref[...], wo_ref[...])
        acc_scr[...] = x1
        h_scr[...] = _rmsnorm(x1, g_ref[...]).astype(BF16)

    u = jnp.square(jnp.maximum(_dot(h_scr[...], wu_ref[...]), 0.0))
    acc_scr[...] += _dot(u, wd_ref[...])

    @pl.when(c == pl.num_programs(0) - 1)
    def _():
        x2 = acc_scr[...]
        o_ref[...] = _rmsnorm(x2, gf_ref[...]) if final else x2


def _norm_proj_even(x, w, layer):
    n = x.shape[0]
    tm = min(ROW_TILE, n)
    return pl.pallas_call(
        _norm_proj_even_kernel,
        grid=(n // tm,),
        in_specs=[pl.BlockSpec((tm, D_MODEL), lambda i: (i, 0)),
                  _layer_spec((1, D_MODEL), layer),
                  _layer_spec((D_MODEL, EVEN_IN), layer // 2)],
        out_specs=pl.BlockSpec((tm, EVEN_IN), lambda i: (i, 0)),
        out_shape=jax.ShapeDtypeStruct((n, EVEN_IN), F32),
        compiler_params=_params("parallel"),
        name="norm_proj_even",
    )(x, w["norm_mix"], w["w_in_even"])


def _norm_proj_odd(x, w, layer):
    n = x.shape[0]
    tm = min(ROW_TILE, n)
    i = layer // 2
    return pl.pallas_call(
        _norm_proj_odd_kernel,
        grid=(n // tm,),
        in_specs=[pl.BlockSpec((tm, D_MODEL), lambda i: (i, 0)),
                  _layer_spec((1, D_MODEL), layer),
                  _layer_spec((D_MODEL, ODD_MAIN), i),
                  _layer_spec((D_MODEL, LANES), i),
                  _layer_spec((LANES, GLA_KEY), i),
                  _layer_spec((1, GLA_KEY), i)],
        out_specs=pl.BlockSpec((tm, ODD_OUT), lambda i: (i, 0)),
        out_shape=jax.ShapeDtypeStruct((n, ODD_OUT), F32),
        compiler_params=_params("parallel"),
        name="norm_proj_odd",
    )(x, w["norm_mix"], w["w_in_odd"], w["wg1"], w["wg2"], w["bg"])


def _out_mlp(x, mix, w, layer):
    n = x.shape[0]
    final = layer == DEPTH - 1
    w_out = w["w_out_even"] if layer % 2 == 0 else w["w_out_odd"]
    return pl.pallas_call(
        functools.partial(_out_mlp_kernel, final),
        grid=(D_FF // FF_CHUNK,),
        in_specs=[_const_spec((n, D_MODEL)),
                  _const_spec((n, D_MODEL)),
                  _layer_spec((D_MODEL, D_MODEL), layer // 2),
                  _layer_spec((1, D_MODEL), layer),
                  pl.BlockSpec((None, D_MODEL, FF_CHUNK), lambda c: (layer, 0, c)),
                  pl.BlockSpec((None, FF_CHUNK, D_MODEL), lambda c: (layer, c, 0)),
                  _const_spec((1, D_MODEL))],
        out_specs=pl.BlockSpec((n, D_MODEL), lambda c: (0, 0)),
        out_shape=jax.ShapeDtypeStruct((n, D_MODEL), F32),
        scratch_shapes=[pltpu.VMEM((n, D_MODEL), BF16), pltpu.VMEM((n, D_MODEL), F32)],
        compiler_params=_params("arbitrary"),
        name="out_mlp_final" if final else "out_mlp",
    )(x, mix, w_out, w["norm_mlp"], w["w_up"], w["w_down"], w["norm_final"])


def _run(stream):
    for _ in stream:
        pass


def _interleave(*streams):
    done = [0.0] * len(streams)
    alive = list(range(len(streams)))
    while alive:
        i = min(alive, key=lambda j: done[j])
        try:
            done[i] += next(streams[i])
        except StopIteration:
            alive.remove(i)


class _Pace:
    def __init__(self):
        self.zero = None

    def after(self, value):
        t = value[0:1, 0:1]
        self.zero = jnp.where(t == t, 0.0, t)

    def __call__(self, x):
        return x if self.zero is None else x + self.zero


def _no_pace(x):
    return x


def _even_mixer_tile(C, chunks, pos_base, load, store, cos_ref, sin_ref,
                     s_src, s_dst, h_src, h_dst, tail_ref, weights, pace=_no_pace):
    cw_ref, cbias_ref, wa_ref, ba_ref, wi_ref, bi_ref, lam_ref = weights
    lane = lax.broadcasted_iota(jnp.int32, (C, RET_KEY), 1)
    first_half = (lane % RET_DK) < (RET_DK // 2)
    ri = lax.broadcasted_iota(jnp.int32, (C, C), 0)
    ci = lax.broadcasted_iota(jnp.int32, (C, C), 1)
    rel = (ri - ci).astype(F32)
    row = lax.broadcasted_iota(jnp.int32, (C, 1), 0).astype(F32)
    trow = lax.broadcasted_iota(jnp.int32, (C, LRU_WIDTH), 0)
    row8 = lax.broadcasted_iota(jnp.int32, (SUBLANES, LRU_WIDTH), 0)
    lam = lam_ref[...]
    softplus_neg_lam = jnp.maximum(-lam, 0.0) + jnp.log(1.0 + jnp.exp(-jnp.abs(lam)))
    log_decay = [math.log1p(-(2.0 ** (-5.0 - h))) for h in range(RET_HEADS)]
    decay_mask = [jnp.where(rel >= 0, jnp.exp(jnp.maximum(rel, 0.0) * lg), 0.0)
                  for lg in log_decay]
    q_decay = [jnp.exp((row + 1.0) * lg) for lg in log_decay]
    k_decay = [jnp.exp((C - 1.0 - row) * lg) for lg in log_decay]
    n_scan = max(1, int(math.log2(C)))

    v0 = 2 * RET_KEY
    g0 = v0 + RET_VALUE
    x0 = g0 + RET_VALUE
    xg0 = x0 + LRU_WIDTH
    heads = range(RET_HEADS)
    J = range(chunks)
    rows = [slice(j * C, (j + 1) * C) for j in J]

    def delayed(x, t, n):
        rolled = pltpu.roll(x, n, axis=0)
        head = jnp.where(row8 < n, pltpu.roll(t, n, axis=0), rolled[0:SUBLANES, :])
        if C == SUBLANES:
            return head
        return jnp.concatenate([head, rolled[SUBLANES:, :]], axis=0)

    qh, kh, q_dec, k_dec, vh, xc = ([None] * chunks for _ in range(6))
    tail = tail_ref[...]
    for j in J:
        cos = cos_ref[rows[j], :]
        sin = sin_ref[rows[j], :]

        def rope(x):
            partner = jnp.where(first_half,
                                pltpu.roll(x, RET_KEY - RET_DK // 2, axis=1),
                                pltpu.roll(x, RET_DK // 2, axis=1))
            return x * cos + partner * sin

        q = rope(load(rows[j], slice(0, RET_KEY)))
        k = rope(load(rows[j], slice(RET_KEY, 2 * RET_KEY))) * (RET_DK ** -0.5)
        q = [q[:, h * RET_DK:(h + 1) * RET_DK] for h in heads]
        k = [k[:, h * RET_DK:(h + 1) * RET_DK] for h in heads]
        q_dec[j] = [(q[h] * q_decay[h]).astype(BF16) for h in heads]
        k_dec[j] = [(k[h] * k_decay[h]).astype(BF16) for h in heads]
        qh[j] = [q[h].astype(BF16) for h in heads]
        kh[j] = [k[h].astype(BF16) for h in heads]
        vh[j] = [load(rows[j], slice(v0 + h * RET_DV, v0 + (h + 1) * RET_DV)).astype(BF16)
                 for h in heads]
        yield 0.08 / chunks

        x_new = load(rows[j], slice(x0, x0 + LRU_WIDTH))
        acc = pace(cbias_ref[...])
        for i in range(CONV_W - 1):
            acc = acc + delayed(x_new, tail, CONV_W - 1 - i) * cw_ref[i:i + 1, :]
        xc[j] = acc + x_new * cw_ref[CONV_W - 1:CONV_W, :]
        tail = x_new[C - SUBLANES:C, :]
        yield 0.10 / chunks
    tail_ref[...] = tail

    scores = [[_dot_nt(qh[j][h], kh[j][h]) for h in heads] for j in J]
    kv = [[_dot_tn(k_dec[j][h], vh[j][h]) for h in heads] for j in J]
    xc_all = xc[0] if chunks == 1 else jnp.concatenate(xc, axis=0)
    xc_all = xc_all.astype(BF16)
    diag = [slice(c, c + MXU_PIECE) for c in range(0, LRU_WIDTH, MXU_PIECE)]
    r_all = jnp.concatenate([_dot(xc_all[:, d], wa_ref[d, d]) for d in diag], axis=1)
    i_all = jnp.concatenate([_dot(xc_all[:, d], wi_ref[d, d]) for d in diag], axis=1)
    r_pre = [r_all[rows[j], :] for j in J]
    i_pre = [i_all[rows[j], :] for j in J]
    yield 0.0

    probs, s_in = [None] * chunks, [None] * chunks
    s_cur = [s_src[h] for h in heads]
    for j in J:
        probs[j] = [(scores[j][h] * decay_mask[h]).astype(BF16) for h in heads]
        s_in[j] = [s_cur[h].astype(BF16) for h in heads]
        s_cur = [s_cur[h] * math.exp(C * log_decay[h]) + kv[j][h] for h in heads]
        yield 0.06 / chunks
    for h in heads:
        s_dst[h] = s_cur[h]

    a_cum, b_loc = [None] * chunks, [None] * chunks
    for j in J:
        r = jax.nn.sigmoid(r_pre[j] + pace(ba_ref[...]))
        gate_i = jax.nn.sigmoid(i_pre[j] + bi_ref[...])
        log_a = -LRU_C * r * softplus_neg_lam
        a = jnp.exp(log_a)
        y = -jnp.tanh(log_a) * (a * a + 1.0)
        mult = jnp.where(y > 0.0, y * lax.rsqrt(y), 0.0)
        pos = trow + (pos_base + j * C)
        mult = jnp.where(pos == 0, 1.0, mult)
        b = mult * (gate_i * xc[j])
        yield 0.20 / chunks
        d = 1
        while d < C:
            b = a * _shift_rows(b, d, pace(0.0)) + b
            a = a * _shift_rows(a, d, pace(1.0))
            d *= 2
            yield 0.36 / (chunks * n_scan)
        a_cum[j], b_loc[j] = a, b

    out = [[_dot(probs[j][h], vh[j][h]) + _dot(q_dec[j][h], s_in[j][h]) for h in heads]
           for j in J]
    yield 0.0

    carry = h_src[...]
    for j in J:
        hidden = b_loc[j] + a_cum[j] * carry
        carry = hidden[C - 1:C, :]
        xg = load(rows[j], slice(xg0, xg0 + LRU_WIDTH))
        store(rows[j], slice(RET_VALUE, RET_VALUE + LRU_WIDTH), hidden * jax.nn.gelu(xg))
        yield 0.08 / chunks
        for h in heads:
            gh = load(rows[j], slice(g0 + h * RET_DV, g0 + (h + 1) * RET_DV))
            store(rows[j], slice(h * RET_DV, (h + 1) * RET_DV),
                  jax.nn.silu(gh) * _head_rms(out[j][h], pace(EPS)))
            yield 0.03 / chunks
    h_dst[...] = carry


def _gla_mixer_tile(C, chunks, load, store, s_src, s_dst, nw, pace=_no_pace):
    k0 = GLA_KEY
    v0 = 2 * GLA_KEY
    r0 = v0 + GLA_VALUE
    la0 = r0 + GLA_VALUE
    ri = lax.broadcasted_iota(jnp.int32, (C, C), 0)
    ci = lax.broadcasted_iota(jnp.int32, (C, C), 1)
    causal = ri >= ci
    di = lax.broadcasted_iota(jnp.int32, (GLA_DK, GLA_DK), 0)
    dj = lax.broadcasted_iota(jnp.int32, (GLA_DK, GLA_DK), 1)
    eye = di == dj
    heads = range(GLA_HEADS)
    key = [slice(h * GLA_DK, (h + 1) * GLA_DK) for h in heads]
    J = range(chunks)
    rows = [slice(j * C, (j + 1) * C) for j in J]

    q_in, k_in, q_st, k_st, decay_col, vh = ([None] * chunks for _ in range(6))
    for j in J:
        b = load(rows[j], slice(la0, la0 + GLA_KEY))
        d = 1
        while d < C:
            b = b + _shift_rows(b, d, pace(0.0))
            d *= 2
        yield 0.12 / chunks
        b_mid = pace(b[C // 2:C // 2 + 1, :])
        b_last = b[C - 1:C, :]
        q = load(rows[j], slice(0, GLA_KEY)) * (GLA_DK ** -0.5)
        k = load(rows[j], slice(k0, k0 + GLA_KEY))
        q_in[j] = (q * jnp.exp(b - b_mid)).astype(BF16)
        k_in[j] = (k * jnp.exp(b_mid - b)).astype(BF16)
        q_st[j] = (q * jnp.exp(b)).astype(BF16)
        k_st[j] = (k * jnp.exp(b_last - b)).astype(BF16)
        s_decay = jnp.exp(b_last)
        decay_col[j] = [jnp.sum(jnp.where(eye, s_decay[:, key[h]], 0.0), axis=1, keepdims=True)
                        for h in heads]
        vh[j] = [load(rows[j], slice(v0 + h * GLA_DV, v0 + (h + 1) * GLA_DV)).astype(BF16)
                 for h in heads]
        yield 0.28 / chunks

    scores = [[_dot_nt(q_in[j][:, key[h]], k_in[j][:, key[h]]) for h in heads] for j in J]
    kv = [[_dot_tn(k_st[j][:, key[h]], vh[j][h]) for h in heads] for j in J]
    yield 0.0

    probs, s_in = [None] * chunks, [None] * chunks
    s_cur = [s_src[h] for h in heads]
    for j in J:
        probs[j] = [jnp.where(causal, scores[j][h], 0.0).astype(BF16) for h in heads]
        s_in[j] = [s_cur[h].astype(BF16) for h in heads]
        s_cur = [s_cur[h] * decay_col[j][h] + kv[j][h] for h in heads]
        yield 0.16 / chunks
    for h in heads:
        s_dst[h] = s_cur[h]

    out = [[_dot(probs[j][h], vh[j][h]) + _dot(q_st[j][:, key[h]], s_in[j][h]) for h in heads]
           for j in J]
    yield 0.0

    for j in J:
        for h in heads:
            rh = load(rows[j], slice(r0 + h * GLA_DV, r0 + (h + 1) * GLA_DV))
            store(rows[j], slice(h * GLA_DV, (h + 1) * GLA_DV),
                  jax.nn.silu(rh) * (_head_rms(out[j][h], pace(EPS)) * nw))
            yield 0.11 / chunks


def _even_mixer_kernel(pos0, chunk, seqs, *refs):
    (proj_ref, cos_ref, sin_ref, s0_ref, h0_ref, cb_ref,
     cw_ref, cbias_ref, wa_ref, ba_ref, wi_ref, bi_ref, lam_ref) = refs[:13]
    refs = refs[14:]
    mix_ref, snew_ref, hnew_ref, cnew_ref, tail_scr = refs
    weights = (cw_ref, cbias_ref, wa_ref, ba_ref, wi_ref, bi_ref, lam_ref)
    for g in range(seqs):
        tail = tail_scr.at[g]
        tail[...] = jnp.zeros((SUBLANES, LRU_WIDTH), F32)
        tail[SUBLANES - (CONV_W - 1):SUBLANES, :] = cb_ref[g]

        def load(rows, cols, g=g):
            return proj_ref[g, rows, cols]

        def store(rows, cols, val, g=g):
            mix_ref[g, rows, cols] = val

        _run(_even_mixer_tile(chunk, 1, pos0, load, store, cos_ref, sin_ref,
                              s0_ref.at[g], snew_ref.at[g], h0_ref.at[g], hnew_ref.at[g],
                              tail, weights))
        cnew_ref[g] = tail[SUBLANES - (CONV_W - 1):SUBLANES, :]


def _even_mixer(proj, pos0, cos, sin, ret_s, lru_h, conv_buf, w, layer, prev_ret):
    B, T, _ = proj.shape
    i = layer // 2
    n_even = w["w_in_even"].shape[0]
    G = SAMPLE_SEQS
    ret_block = (None, G, RET_HEADS, RET_DK, RET_DV)
    in_specs = [pl.BlockSpec((G, T, EVEN_IN), lambda b: (b, 0, 0)),
                _const_spec((T, RET_KEY)),
                _const_spec((T, RET_KEY)),
                pl.BlockSpec(ret_block, lambda b: (i, b, 0, 0, 0)),
                pl.BlockSpec((None, G, 1, LRU_WIDTH), lambda b: (i, b, 0, 0)),
                pl.BlockSpec((None, G, CONV_W - 1, LRU_WIDTH), lambda b: (i, b, 0, 0)),
                _layer_spec((CONV_W, LRU_WIDTH), i),
                _layer_spec((1, LRU_WIDTH), i),
                _layer_spec((LRU_WIDTH, LRU_WIDTH), i),
                _layer_spec((1, LRU_WIDTH), i),
                _layer_spec((LRU_WIDTH, LRU_WIDTH), i),
                _layer_spec((1, LRU_WIDTH), i),
                _layer_spec((1, LRU_WIDTH), i)]
    args = [proj, cos, sin, ret_s, lru_h, conv_buf, w["conv_w"], w["conv_b"],
            w["wa"], w["ba"], w["wi"], w["bi"], w["lam"]]
    in_specs.append(pl.BlockSpec(memory_space=pl.ANY))
    args.append(prev_ret)
    aliases = {len(args) - 1: 1}
    out_shapes = (jax.ShapeDtypeStruct((B, T, D_MODEL), F32),
                  jax.ShapeDtypeStruct((n_even, B, RET_HEADS, RET_DK, RET_DV), F32),
                  jax.ShapeDtypeStruct((B, 1, LRU_WIDTH), F32),
                  jax.ShapeDtypeStruct((B, CONV_W - 1, LRU_WIDTH), F32))
    return pl.pallas_call(
        functools.partial(_even_mixer_kernel, pos0, T, G),
        grid=(B // G,),
        in_specs=in_specs,
        out_specs=(pl.BlockSpec((G, T, D_MODEL), lambda b: (b, 0, 0)),
                   pl.BlockSpec(ret_block, lambda b: (i, b, 0, 0, 0)),
                   pl.BlockSpec((G, 1, LRU_WIDTH), lambda b: (b, 0, 0)),
                   pl.BlockSpec((G, CONV_W - 1, LRU_WIDTH), lambda b: (b, 0, 0))),
        out_shape=out_shapes,
        scratch_shapes=[pltpu.VMEM((G, SUBLANES, LRU_WIDTH), F32)],
        input_output_aliases=aliases,
        compiler_params=_params("parallel"),
        name="even_mixer",
    )(*args)


def _gla_mixer_kernel(chunk, seqs, *refs):
    proj_ref, s0_ref, nw_ref = refs[:3]
    refs = refs[4:]
    mix_ref, snew_ref = refs
    nw = nw_ref[...]
    for g in range(seqs):
        def load(rows, cols, g=g):
            return proj_ref[g, rows, cols]

        def store(rows, cols, val, g=g):
            mix_ref[g, rows, cols] = val

        _run(_gla_mixer_tile(chunk, 1, load, store, s0_ref.at[g], snew_ref.at[g], nw))


def _gla_mixer(proj, gla_s, w, layer, prev_gla):
    B, T, _ = proj.shape
    i = layer // 2
    n_odd = w["w_in_odd"].shape[0]
    G = SAMPLE_SEQS
    state = (None, G, GLA_HEADS, GLA_DK, GLA_DV)
    in_specs = [pl.BlockSpec((G, T, ODD_OUT), lambda b: (b, 0, 0)),
                pl.BlockSpec(state, lambda b: (i, b, 0, 0, 0)),
                _layer_spec((1, GLA_DV), i)]
    args = [proj, gla_s, w["gla_norm"]]
    in_specs.append(pl.BlockSpec(memory_space=pl.ANY))
    args.append(prev_gla)
    aliases = {len(args) - 1: 1}
    return pl.pallas_call(
        functools.partial(_gla_mixer_kernel, T, G),
        grid=(B // G,),
        in_specs=in_specs,
        out_specs=(pl.BlockSpec((G, T, D_MODEL), lambda b: (b, 0, 0)),
                   pl.BlockSpec(state, lambda b: (i, b, 0, 0, 0))),
        out_shape=(jax.ShapeDtypeStruct((B, T, D_MODEL), F32),
                   jax.ShapeDtypeStruct((n_odd, B, GLA_HEADS, GLA_DK, GLA_DV), F32)),
        input_output_aliases=aliases,
        compiler_params=_params("parallel"),
        name="gla_mixer",
    )(*args)


def _fused_layer_kernel(even, final, chunk, rows, tiles_per_seq, n_tiles, slot, has_prev, *refs):
    it = iter(refs)
    xlead_ref, xlag_ref, gmix_ref, win_ref = (next(it) for _ in range(4))
    if even:
        cos_ref, sin_ref = next(it), next(it)
        mixer_weights = tuple(next(it) for _ in range(7))
    else:
        wg1_ref, wg2_ref, bg_ref, nw_ref = (next(it) for _ in range(4))
    wo_ref, gmlp_ref, wu_ref, wd_ref, gf_ref = (next(it) for _ in range(5))
    if has_prev:
        next(it)
    out_ref, snew_ref = next(it), next(it)
    if even:
        hnew_ref, cnew_ref = next(it), next(it)
    if not has_prev:
        zeros_ref = next(it)
    proj_scr = (next(it), next(it))
    mix_scr = (next(it), next(it))
    x1_scr, u_scr, s_scr = next(it), next(it), next(it)
    if even:
        h_scr, tail_scr = next(it), next(it)

    s = pl.program_id(0)
    tile_in_seq = lax.rem(jnp.clip(s - 1, 0, n_tiles - 1), tiles_per_seq)

    @pl.when(s == 0)
    def _():
        proj_scr[1][...] = jnp.zeros(proj_scr[1].shape, F32)
        mix_scr[0][...] = jnp.zeros(mix_scr[0].shape, BF16)

    @pl.when((s == 0) | (lax.rem(s + tiles_per_seq - 1, tiles_per_seq) == 0))
    def _():
        s_scr[...] = jnp.zeros(s_scr.shape, F32)
        if even:
            h_scr[...] = jnp.zeros(h_scr.shape, F32)
            tail_scr[...] = jnp.zeros(tail_scr.shape, F32)

    main_cols = EVEN_IN if even else ODD_MAIN
    macs_per_row = D_MODEL * (main_cols + D_MODEL + 2 * D_FF)

    def matrix_stream(proj_a, mix_c, pace):
        P = MXU_PIECE
        h_in = _rmsnorm(xlead_ref[...], gmix_ref[...]).astype(BF16)
        m = mix_c[...]
        for n0 in range(0, D_MODEL, P):
            x1 = xlag_ref[:, n0:n0 + P] + _dot(m, wo_ref[:, n0:n0 + P])
            x1_scr[:, n0:n0 + P] = x1
            pace.after(x1)
            yield D_MODEL * P / macs_per_row
        h_mlp = _rmsnorm(x1_scr[...], gmlp_ref[...]).astype(BF16)
        if not even:
            glr = _dot(h_in, wg1_ref[...]).astype(BF16)
        for n0 in range(0, main_cols, P):
            p = _dot(h_in, win_ref[:, n0:n0 + P])
            proj_a[:, n0:n0 + P] = p
            pace.after(p)
            if not even and n0 == P:
                z = _dot(glr, wg2_ref[...]) + bg_ref[...]
            yield D_MODEL * P / macs_per_row
        if not even:
            proj_a[:, ODD_MAIN:] = _scaled_log_sigmoid(z)
            yield 0.0
        for n0 in range(0, D_FF, P):
            u = jnp.square(jnp.maximum(_dot(h_mlp, wu_ref[:, n0:n0 + P]), 0.0))
            u_scr[:, n0:n0 + P] = u.astype(BF16)
            pace.after(u)
            yield D_MODEL * P / macs_per_row
        for n0 in range(0, D_MODEL, P):
            x2 = x1_scr[:, n0:n0 + P] + _dot(u_scr[...], wd_ref[:, n0:n0 + P])
            if final:
                x1_scr[:, n0:n0 + P] = x2
            else:
                out_ref[:, n0:n0 + P] = x2
            pace.after(x2)
            yield D_FF * P / macs_per_row
        if final:
            out_ref[...] = _rmsnorm(x1_scr[...], gf_ref[...])

    def step(par):
        proj_a, proj_b = proj_scr[par], proj_scr[1 - par]
        mix_b, mix_c = mix_scr[1 - par], mix_scr[par]

        def load(r, c):
            return proj_b[r, c]

        def store(r, c, val):
            mix_b[r, c] = val.astype(BF16)

        pace = _Pace()
        if even:
            vector_stream = _even_mixer_tile(chunk, rows // chunk, tile_in_seq * rows, load,
                                             store, cos_ref, sin_ref, s_scr, s_scr, h_scr, h_scr,
                                             tail_scr, mixer_weights, pace)
        else:
            vector_stream = _gla_mixer_tile(chunk, rows // chunk, load, store, s_scr, s_scr,
                                            nw_ref[...], pace)
        _interleave(matrix_stream(proj_a, mix_c, pace), vector_stream)

    parity = lax.rem(s, 2)
    pl.when(parity == 0)(functools.partial(step, 0))
    pl.when(parity == 1)(functools.partial(step, 1))

    if not has_prev:
        zeros_ref[...] = jnp.zeros(zeros_ref.shape, F32)

    @pl.when((s >= 1) & (s <= n_tiles) & (lax.rem(s, tiles_per_seq) == 0))
    def _():
        if has_prev:
            snew_ref[0] = s_scr[...]
        else:
            for other in range(snew_ref.shape[0]):
                snew_ref[other, 0] = s_scr[...] if other == slot else jnp.zeros(s_scr.shape, F32)
        if even:
            hnew_ref[0] = h_scr[...]
            cnew_ref[0] = tail_scr[SUBLANES - (CONV_W - 1):SUBLANES, :]


def _fused_layer(x, w, layer, cos, sin, prev_state, seq_len, n_short):
    n_rows = x.shape[0]
    B = n_rows // seq_len
    R = FUSED_ROWS
    even = layer % 2 == 0
    final = layer == DEPTH - 1
    i = layer // 2
    C = math.gcd(seq_len, RET_CHUNK if even else GLA_CHUNK)
    tps = seq_len // R
    NT = n_rows // R
    has_prev = prev_state is not None

    def lead(s):
        return (jnp.minimum(s, NT - 1), 0)

    def mid_tile(s):
        return jnp.clip(s - 1, 0, NT - 1)

    def lag(s):
        return (jnp.clip(s - 2, 0, NT - 1), 0)

    def seq3(s):
        return (mid_tile(s) // tps, 0, 0)

    in_specs = [pl.BlockSpec((R, D_MODEL), lead),
                pl.BlockSpec((R, D_MODEL), lag),
                _layer_spec((1, D_MODEL), layer)]
    args = [x, x, w["norm_mix"]]
    if even:
        in_specs += [_layer_spec((D_MODEL, EVEN_IN), i),
                     pl.BlockSpec((R, RET_KEY), lambda s: (mid_tile(s) % tps, 0)),
                     pl.BlockSpec((R, RET_KEY), lambda s: (mid_tile(s) % tps, 0)),
                     _layer_spec((CONV_W, LRU_WIDTH), i),
                     _layer_spec((1, LRU_WIDTH), i),
                     _layer_spec((LRU_WIDTH, LRU_WIDTH), i),
                     _layer_spec((1, LRU_WIDTH), i),
                     _layer_spec((LRU_WIDTH, LRU_WIDTH), i),
                     _layer_spec((1, LRU_WIDTH), i),
                     _layer_spec((1, LRU_WIDTH), i)]
        args += [w["w_in_even"], cos, sin, w["conv_w"], w["conv_b"], w["wa"], w["ba"],
                 w["wi"], w["bi"], w["lam"]]
        w_out = w["w_out_even"]
        state_shape = (RET_HEADS, RET_DK, RET_DV)
        n_stack = w["w_in_even"].shape[0]
        proj_cols = EVEN_IN
    else:
        in_specs += [_layer_spec((D_MODEL, ODD_MAIN), i),
                     _layer_spec((D_MODEL, LANES), i),
                     _layer_spec((LANES, GLA_KEY), i),
                     _layer_spec((1, GLA_KEY), i),
                     _layer_spec((1, GLA_DV), i)]
        args += [w["w_in_odd"], w["wg1"], w["wg2"], w["bg"], w["gla_norm"]]
        w_out = w["w_out_odd"]
        state_shape = (GLA_HEADS, GLA_DK, GLA_DV)
        n_stack = w["w_in_odd"].shape[0]
        proj_cols = ODD_OUT
    in_specs += [_layer_spec((D_MODEL, D_MODEL), i),
                 _layer_spec((1, D_MODEL), layer),
                 _layer_spec((D_MODEL, D_FF), layer),
                 _layer_spec((D_FF, D_MODEL), layer),
                 _const_spec((1, D_MODEL))]
    args += [w_out, w["norm_mlp"], w["w_up"], w["w_down"], w["norm_final"]]
    aliases = {}
    if has_prev:
        in_specs.append(pl.BlockSpec(memory_space=pl.ANY))
        args.append(prev_state)
        aliases = {len(args) - 1: 1}

    if has_prev:
        state_spec = pl.BlockSpec((None, 1) + state_shape,
                                  lambda s: (i, mid_tile(s) // tps, 0, 0, 0))
    else:
        state_spec = pl.BlockSpec((n_stack, 1) + state_shape,
                                  lambda s: (0, mid_tile(s) // tps, 0, 0, 0))
    out_specs = [pl.BlockSpec((R, D_MODEL), lag), state_spec]
    out_shapes = [jax.ShapeDtypeStruct((n_rows, D_MODEL), F32),
                  jax.ShapeDtypeStruct((n_stack, B) + state_shape, F32)]
    scratch = [pltpu.VMEM((R, proj_cols), F32), pltpu.VMEM((R, proj_cols), F32),
               pltpu.VMEM((R, D_MODEL), BF16), pltpu.VMEM((R, D_MODEL), BF16),
               pltpu.VMEM((R, D_MODEL), F32), pltpu.VMEM((R, D_FF), BF16),
               pltpu.VMEM(state_shape, F32)]
    if even:
        out_specs += [pl.BlockSpec((1, 1, LRU_WIDTH), seq3),
                      pl.BlockSpec((1, CONV_W - 1, LRU_WIDTH), seq3)]
        out_shapes += [jax.ShapeDtypeStruct((B, 1, LRU_WIDTH), F32),
                       jax.ShapeDtypeStruct((B, CONV_W - 1, LRU_WIDTH), F32)]
        scratch += [pltpu.VMEM((1, LRU_WIDTH), F32),
                    pltpu.VMEM((SUBLANES, LRU_WIDTH), F32)]
    if not has_prev:
        per_step = n_short // NT
        assert per_step * NT == n_short
        out_specs.append(pl.BlockSpec((n_stack, per_step) + state_shape,
                                      lambda s: (0, jnp.minimum(s, NT - 1), 0, 0, 0)))
        out_shapes.append(jax.ShapeDtypeStruct((n_stack, n_short) + state_shape, F32))
    return pl.pallas_call(
        functools.partial(_fused_layer_kernel, even, final, C, R, tps, NT, i, has_prev),
        grid=(NT + 2,),
        in_specs=in_specs,
        out_specs=tuple(out_specs),
        out_shape=tuple(out_shapes),
        scratch_shapes=scratch,
        input_output_aliases=aliases,
        compiler_params=_params("arbitrary"),
        name="layer_even" if even else "layer_odd",
    )(*args)


def _rope_tables(pos0, T):
    half = RET_DK // 2
    inv = ROPE_BASE ** (-jnp.arange(half, dtype=F32) / half)
    ang = (pos0 + jnp.arange(T, dtype=jnp.int32)).astype(F32)[:, None] * inv[None, :]
    cos = jnp.cos(ang)
    sin = jnp.sin(ang)
    cos_full = jnp.tile(jnp.concatenate([cos, cos], axis=-1), (1, RET_HEADS))
    sin_signed = jnp.tile(jnp.concatenate([-sin, sin], axis=-1), (1, RET_HEADS))
    return cos_full, sin_signed


def _block_diag(w):
    nl, nb, c, d = w.shape
    eye = jnp.eye(nb, dtype=w.dtype)
    return (eye[None, :, None, :, None] * w[:, :, :, None, :]).reshape(nl, nb * c, nb * d)


def _prepare_weights(norm_mix, norm_mlp, norm_final, w_in_even, w_out_even, conv_w, conv_b,
                     lru_w_a, lru_b_a, lru_w_i, lru_b_i, lru_lambda,
                     w_in_odd, gla_w_gate2, gla_b_gate, gla_norm, w_out_odd, w_up, w_down):
    n_even = w_in_even.shape[0]
    n_odd = w_in_odd.shape[0]
    pad1 = LANES - GLA_RANK
    return dict(
        norm_mix=norm_mix.reshape(DEPTH, 1, D_MODEL),
        norm_mlp=norm_mlp.reshape(DEPTH, 1, D_MODEL),
        norm_final=norm_final.reshape(1, D_MODEL),
        w_in_even=w_in_even.astype(BF16),
        w_out_even=w_out_even.astype(BF16),
        conv_w=conv_w,
        conv_b=conv_b.reshape(n_even, 1, LRU_WIDTH),
        wa=_block_diag(lru_w_a).astype(BF16),
        ba=lru_b_a.reshape(n_even, 1, LRU_WIDTH),
        wi=_block_diag(lru_w_i).astype(BF16),
        bi=lru_b_i.reshape(n_even, 1, LRU_WIDTH),
        lam=lru_lambda.reshape(n_even, 1, LRU_WIDTH),
        w_in_odd=w_in_odd.astype(BF16),
        wg1=jnp.pad(w_in_odd[:, :, ODD_MAIN:], ((0, 0), (0, 0), (0, pad1))).astype(BF16),
        wg2=jnp.pad(gla_w_gate2, ((0, 0), (0, pad1), (0, 0))).astype(BF16),
        bg=gla_b_gate.reshape(n_odd, 1, GLA_KEY),
        gla_norm=gla_norm.reshape(n_odd, 1, GLA_DV),
        w_out_odd=w_out_odd.astype(BF16),
        w_up=w_up.astype(BF16),
        w_down=w_down.astype(BF16),
    )


def _prompt_trunk(x, w, n_short):
    B, T, _ = x.shape
    xf = x.reshape(B * T, D_MODEL)
    cos, sin = _rope_tables(0, T)
    ret_new, gla_new = None, None
    lrus, convs, zeros = [], [], []
    for layer in range(DEPTH):
        prev = ret_new if layer % 2 == 0 else gla_new
        outs = _fused_layer(xf, w, layer, cos, sin, prev, T, n_short)
        if prev is None:
            zeros.append(outs[-1])
            outs = outs[:-1]
        if layer % 2 == 0:
            xf, ret_new, lh, cb = outs
            lrus.append(lh.reshape(B, LRU_WIDTH))
            convs.append(cb)
        else:
            xf, gla_new = outs
    states = (ret_new, jnp.stack(lrus), jnp.stack(convs), gla_new)
    return xf.reshape(B, T, D_MODEL), states, zeros


def _sample_trunk(x, pos0, ret_s, lru_h, conv_buf, gla_s, ret_new, gla_new, w):
    B, T, _ = x.shape
    xf = x.reshape(B * T, D_MODEL)
    cos, sin = _rope_tables(pos0, T)
    lru_h = lru_h.reshape(lru_h.shape[0], B, 1, LRU_WIDTH)
    lrus, convs = [], []
    for layer in range(DEPTH):
        if layer % 2 == 0:
            proj = _norm_proj_even(xf, w, layer)
            mix, ret_new, lh, cb = _even_mixer(proj.reshape(B, T, EVEN_IN), pos0, cos, sin,
                                               ret_s, lru_h, conv_buf, w, layer, ret_new)
            lrus.append(lh.reshape(B, LRU_WIDTH))
            convs.append(cb)
        else:
            proj = _norm_proj_odd(xf, w, layer)
            mix, gla_new = _gla_mixer(proj.reshape(B, T, ODD_OUT), gla_s, w, layer, gla_new)
        xf = _out_mlp(xf, mix.reshape(B * T, D_MODEL), w, layer)
    return xf.reshape(B, T, D_MODEL), ret_new, jnp.stack(lrus), jnp.stack(convs), gla_new


def kernel(x_prompt, x_sample, state_ret, state_lru, state_conv, state_gla, norm_mix, norm_mlp, norm_final, w_in_even, w_out_even, conv_w, conv_b, lru_w_a, lru_b_a, lru_w_i, lru_b_i, lru_lambda, w_in_odd, gla_w_gate2, gla_b_gate, gla_norm, w_out_odd, w_up, w_down):
    w = _prepare_weights(norm_mix, norm_mlp, norm_final, w_in_even, w_out_even, conv_w, conv_b,
                         lru_w_a, lru_b_a, lru_w_i, lru_b_i, lru_lambda,
                         w_in_odd, gla_w_gate2, gla_b_gate, gla_norm, w_out_odd, w_up, w_down)
    y_p, (ret_p, lru_p, conv_p, gla_p), (ret_zero, gla_zero) = _prompt_trunk(
        x_prompt, w, x_sample.shape[0])
    y_s, ret_s, lru_s, conv_s, gla_s = _sample_trunk(x_sample, PAST_LEN, state_ret, state_lru,
                                                     state_conv, state_gla, ret_zero, gla_zero, w)
    return (y_p, y_s, ret_p, ret_s, lru_p, lru_s, conv_p, conv_s, gla_p, gla_s)
```

```python
import functools
import math

import jax
import jax.numpy as jnp
from jax import lax
from jax.experimental import pallas as pl
from jax.experimental.pallas import tpu as pltpu

F32 = jnp.float32
BF16 = jnp.bfloat16

D_MODEL = 1024
DEPTH = 4
PAST_LEN = 16384
EPS = 1e-6
ROPE_BASE = 10000.0

RET_HEADS = 4
RET_DK = 64
RET_DV = 128
RET_KEY = RET_HEADS * RET_DK
RET_VALUE = RET_HEADS * RET_DV
RET_CHUNK = 128

LRU_WIDTH = 512
CONV_W = 4
LRU_C = 8.0

GLA_HEADS = 4
GLA_KEY = 512
GLA_VALUE = 1024
GLA_DK = GLA_KEY // GLA_HEADS
GLA_DV = GLA_VALUE // GLA_HEADS
GLA_RANK = 16
GLA_TAU = 16.0
GLA_CHUNK = 64

D_FF = 4 * D_MODEL
FF_CHUNK = 1024

EVEN_IN = 2 * RET_KEY + 2 * RET_VALUE + 2 * LRU_WIDTH
ODD_MAIN = 2 * GLA_KEY + 2 * GLA_VALUE
ODD_OUT = ODD_MAIN + GLA_KEY

LANES = 128
SUBLANES = 8
ROW_TILE = 512
FUSED_ROWS = 256
MXU_PIECE = 256
SAMPLE_SEQS = 8
VMEM_LIMIT = 56 * 1024 * 1024


def _dot(a, b):
    return jnp.dot(a.astype(BF16), b.astype(BF16), preferred_element_type=F32)


def _dot_nt(a, b):
    return lax.dot_general(a.astype(BF16), b.astype(BF16), (((1,), (1,)), ((), ())),
                           preferred_element_type=F32)


def _dot_tn(a, b):
    return lax.dot_general(a.astype(BF16), b.astype(BF16), (((0,), (0,)), ((), ())),
                           preferred_element_type=F32)


def _rmsnorm(x, g):
    return x * lax.rsqrt(jnp.mean(x * x, axis=-1, keepdims=True) + EPS) * g


def _head_rms(x, eps=EPS):
    return x * lax.rsqrt(jnp.mean(x * x, axis=-1, keepdims=True) + eps)


def _shift_rows(x, d, fill):
    row = lax.broadcasted_iota(jnp.int32, x.shape, 0)
    return jnp.where(row >= d, pltpu.roll(x, d, axis=0), fill)


def _layer_spec(shape, layer):
    n = len(shape)
    return pl.BlockSpec((None,) + tuple(shape), lambda *_: (layer,) + (0,) * n,
                        pipeline_mode=pl.Buffered(1))


def _const_spec(shape):
    n = len(shape)
    return pl.BlockSpec(shape, lambda *_: (0,) * n, pipeline_mode=pl.Buffered(1))


def _params(*semantics):
    return pltpu.CompilerParams(dimension_semantics=semantics, vmem_limit_bytes=VMEM_LIMIT)


def _scaled_log_sigmoid(z):
    return (jnp.minimum(z, 0.0) - jnp.log(1.0 + jnp.exp(-jnp.abs(z)))) / GLA_TAU


def _log_gate(h, wg1_ref, wg2_ref, bg_ref):
    glr = _dot(h, wg1_ref[...])
    return _scaled_log_sigmoid(_dot(glr, wg2_ref[...]) + bg_ref[...])


def _norm_proj_even_kernel(x_ref, g_ref, w_ref, o_ref):
    h = _rmsnorm(x_ref[...], g_ref[...])
    o_ref[...] = _dot(h, w_ref[...])


def _norm_proj_odd_kernel(x_ref, g_ref, w_ref, wg1_ref, wg2_ref, bg_ref, o_ref):
    h = _rmsnorm(x_ref[...], g_ref[...]).astype(BF16)
    o_ref[:, :ODD_MAIN] = _dot(h, w_ref[...])
    o_ref[:, ODD_MAIN:] = _log_gate(h, wg1_ref, wg2_ref, bg_ref)


def _out_mlp_kernel(final, x_ref, m_ref, wo_ref, g_ref, wu_ref, wd_ref, gf_ref, o_ref,
                    h_scr, acc_scr):
    c = pl.program_id(0)

    @pl.when(c == 0)
    def _():
        x1 = x_ref[...] + _dot(m_ref[...], wo_ref[...])
        acc_scr[...] = x1
        h_scr[...] = _rmsnorm(x1, g_ref[...]).astype(BF16)

    u = jnp.square(jnp.maximum(_dot(h_scr[...], wu_ref[...]), 0.0))
    acc_scr[...] += _dot(u, wd_ref[...])

    @pl.when(c == pl.num_programs(0) - 1)
    def _():
        x2 = acc_scr[...]
        o_ref[...] = _rmsnorm(x2, gf_ref[...]) if final else x2


def _norm_proj_even(x, w, layer):
    n = x.shape[0]
    tm = min(ROW_TILE, n)
    return pl.pallas_call(
        _norm_proj_even_kernel,
        grid=(n // tm,),
        in_specs=[pl.BlockSpec((tm, D_MODEL), lambda i: (i, 0)),
                  _layer_spec((1, D_MODEL), layer),
                  _layer_spec((D_MODEL, EVEN_IN), layer // 2)],
        out_specs=pl.BlockSpec((tm, EVEN_IN), lambda i: (i, 0)),
        out_shape=jax.ShapeDtypeStruct((n, EVEN_IN), F32),
        compiler_params=_params("parallel"),
        name="norm_proj_even",
    )(x, w["norm_mix"], w["w_in_even"])


def _norm_proj_odd(x, w, layer):
    n = x.shape[0]
    tm = min(ROW_TILE, n)
    i = layer // 2
    return pl.pallas_call(
        _norm_proj_odd_kernel,
        grid=(n // tm,),
        in_specs=[pl.BlockSpec((tm, D_MODEL), lambda i: (i, 0)),
                  _layer_spec((1, D_MODEL), layer),
                  _layer_spec((D_MODEL, ODD_MAIN), i),
                  _layer_spec((D_MODEL, LANES), i),
                  _layer_spec((LANES, GLA_KEY), i),
                  _layer_spec((1, GLA_KEY), i)],
        out_specs=pl.BlockSpec((tm, ODD_OUT), lambda i: (i, 0)),
        out_shape=jax.ShapeDtypeStruct((n, ODD_OUT), F32),
        compiler_params=_params("parallel"),
        name="norm_proj_odd",
    )(x, w["norm_mix"], w["w_in_odd"], w["wg1"], w["wg2"], w["bg"])


def _out_mlp(x, mix, w, layer):
    n = x.shape[0]
    final = layer == DEPTH - 1
    w_out = w["w_out_even"] if layer % 2 == 0 else w["w_out_odd"]
    return pl.pallas_call(
        functools.partial(_out_mlp_kernel, final),
        grid=(D_FF // FF_CHUNK,),
        in_specs=[_const_spec((n, D_MODEL)),
                  _const_spec((n, D_MODEL)),
                  _layer_spec((D_MODEL, D_MODEL), layer // 2),
                  _layer_spec((1, D_MODEL), layer),
                  pl.BlockSpec((None, D_MODEL, FF_CHUNK), lambda c: (layer, 0, c)),
                  pl.BlockSpec((None, FF_CHUNK, D_MODEL), lambda c: (layer, c, 0)),
                  _const_spec((1, D_MODEL))],
        out_specs=pl.BlockSpec((n, D_MODEL), lambda c: (0, 0)),
        out_shape=jax.ShapeDtypeStruct((n, D_MODEL), F32),
        scratch_shapes=[pltpu.VMEM((n, D_MODEL), BF16), pltpu.VMEM((n, D_MODEL), F32)],
        compiler_params=_params("arbitrary"),
        name="out_mlp_final" if final else "out_mlp",
    )(x, mix, w_out, w["norm_mlp"], w["w_up"], w["w_down"], w["norm_final"])


def _run(stream):
    for _ in stream:
        pass


def _interleave(*streams):
    done = [0.0] * len(streams)
    alive = list(range(len(streams)))
    while alive:
        i = min(alive, key=lambda j: done[j])
        try:
            done[i] += next(streams[i])
        except StopIteration:
            alive.remove(i)


class _Pace:
    def __init__(self):
        self.zero = None

    def after(self, value):
        t = value[0:1, 0:1]
        self.zero = jnp.where(t == t, 0.0, t)

    def __call__(self, x):
        return x if self.zero is None else x + self.zero


def _no_pace(x):
    return x


def _even_mixer_tile(C, chunks, pos_base, load, store, cos_ref, sin_ref,
                     s_src, s_dst, h_src, h_dst, tail_ref, weights, pace=_no_pace):
    cw_ref, cbias_ref, wa_ref, ba_ref, wi_ref, bi_ref, lam_ref = weights
    lane = lax.broadcasted_iota(jnp.int32, (C, RET_KEY), 1)
    first_half = (lane % RET_DK) < (RET_DK // 2)
    ri = lax.broadcasted_iota(jnp.int32, (C, C), 0)
    ci = lax.broadcasted_iota(jnp.int32, (C, C), 1)
    rel = (ri - ci).astype(F32)
    row = lax.broadcasted_iota(jnp.int32, (C, 1), 0).astype(F32)
    trow = lax.broadcasted_iota(jnp.int32, (C, LRU_WIDTH), 0)
    row8 = lax.broadcasted_iota(jnp.int32, (SUBLANES, LRU_WIDTH), 0)
    lam = lam_ref[...]
    softplus_neg_lam = jnp.maximum(-lam, 0.0) + jnp.log(1.0 + jnp.exp(-jnp.abs(lam)))
    log_decay = [math.log1p(-(2.0 ** (-5.0 - h))) for h in range(RET_HEADS)]
    decay_mask = [jnp.where(rel >= 0, jnp.exp(jnp.maximum(rel, 0.0) * lg), 0.0)
                  for lg in log_decay]
    q_decay = [jnp.exp((row + 1.0) * lg) for lg in log_decay]
    k_decay = [jnp.exp((C - 1.0 - row) * lg) for lg in log_decay]
    groups = C // SUBLANES
    in_group = {d: (trow % SUBLANES) >= d for d in (1, 2, 4)}
    n_scan = 3 + (1 if groups > 1 else 0)

    v0 = 2 * RET_KEY
    g0 = v0 + RET_VALUE
    x0 = g0 + RET_VALUE
    xg0 = x0 + LRU_WIDTH
    heads = range(RET_HEADS)
    J = range(chunks)
    rows = [slice(j * C, (j + 1) * C) for j in J]

    def delayed(x, t, n):
        rolled = pltpu.roll(x, n, axis=0)
        head = jnp.where(row8 < n, pltpu.roll(t, n, axis=0), rolled[0:SUBLANES, :])
        if C == SUBLANES:
            return head
        return jnp.concatenate([head, rolled[SUBLANES:, :]], axis=0)

    qh, kh, q_dec, k_dec, vh, xc = ([None] * chunks for _ in range(6))
    tail = tail_ref[...]
    for j in J:
        cos = cos_ref[rows[j], :]
        sin = sin_ref[rows[j], :]

        def rope(x):
            partner = jnp.where(first_half,
                                pltpu.roll(x, RET_KEY - RET_DK // 2, axis=1),
                                pltpu.roll(x, RET_DK // 2, axis=1))
            return x * cos + partner * sin

        q = rope(load(rows[j], slice(0, RET_KEY)))
        k = rope(load(rows[j], slice(RET_KEY, 2 * RET_KEY))) * (RET_DK ** -0.5)
        q = [q[:, h * RET_DK:(h + 1) * RET_DK] for h in heads]
        k = [k[:, h * RET_DK:(h + 1) * RET_DK] for h in heads]
        q_dec[j] = [(q[h] * q_decay[h]).astype(BF16) for h in heads]
        k_dec[j] = [(k[h] * k_decay[h]).astype(BF16) for h in heads]
        qh[j] = [q[h].astype(BF16) for h in heads]
        kh[j] = [k[h].astype(BF16) for h in heads]
        vh[j] = [load(rows[j], slice(v0 + h * RET_DV, v0 + (h + 1) * RET_DV)).astype(BF16)
                 for h in heads]
        yield 0.08 / chunks

        x_new = load(rows[j], slice(x0, x0 + LRU_WIDTH))
        acc = pace(cbias_ref[...])
        for i in range(CONV_W - 1):
            acc = acc + delayed(x_new, tail, CONV_W - 1 - i) * cw_ref[i:i + 1, :]
        xc[j] = acc + x_new * cw_ref[CONV_W - 1:CONV_W, :]
        tail = x_new[C - SUBLANES:C, :]
        yield 0.10 / chunks
    tail_ref[...] = tail

    scores = [[_dot_nt(qh[j][h], kh[j][h]) for h in heads] for j in J]
    kv = [[_dot_tn(k_dec[j][h], vh[j][h]) for h in heads] for j in J]
    xc_all = xc[0] if chunks == 1 else jnp.concatenate(xc, axis=0)
    xc_all = xc_all.astype(BF16)
    diag = [slice(c, c + MXU_PIECE) for c in range(0, LRU_WIDTH, MXU_PIECE)]
    r_all = jnp.concatenate([_dot(xc_all[:, d], wa_ref[d, d]) for d in diag], axis=1)
    i_all = jnp.concatenate([_dot(xc_all[:, d], wi_ref[d, d]) for d in diag], axis=1)
    r_pre = [r_all[rows[j], :] for j in J]
    i_pre = [i_all[rows[j], :] for j in J]
    yield 0.0

    probs, s_in = [None] * chunks, [None] * chunks
    s_cur = [s_src[h] for h in heads]
    for j in J:
        probs[j] = [(scores[j][h] * decay_mask[h]).astype(BF16) for h in heads]
        s_in[j] = [s_cur[h].astype(BF16) for h in heads]
        s_cur = [s_cur[h] * math.exp(C * log_decay[h]) + kv[j][h] for h in heads]
        yield 0.06 / chunks
    for h in heads:
        s_dst[h] = s_cur[h]

    a_cum, b_loc = [None] * chunks, [None] * chunks
    for j in J:
        r = jax.nn.sigmoid(r_pre[j] + pace(ba_ref[...]))
        gate_i = jax.nn.sigmoid(i_pre[j] + bi_ref[...])
        log_a = -LRU_C * r * softplus_neg_lam
        a = jnp.exp(log_a)
        y = -jnp.tanh(log_a) * (a * a + 1.0)
        mult = jnp.where(y > 0.0, y * lax.rsqrt(y), 0.0)
        pos = trow + (pos_base + j * C)
        mult = jnp.where(pos == 0, 1.0, mult)
        b = mult * (gate_i * xc[j])
        yield 0.20 / chunks
        d = 1
        while d < SUBLANES:
            b = a * jnp.where(in_group[d], pltpu.roll(b, d, axis=0), pace(0.0)) + b
            a = a * jnp.where(in_group[d], pltpu.roll(a, d, axis=0), pace(1.0))
            d *= 2
            yield 0.36 / (chunks * n_scan)
        if groups > 1:
            h_in, p_in = [], []
            h = jnp.zeros((1, LRU_WIDTH), F32)
            p = jnp.ones((1, LRU_WIDTH), F32)
            for g in range(groups):
                h_in.append(jnp.broadcast_to(h, (SUBLANES, LRU_WIDTH)))
                p_in.append(jnp.broadcast_to(p, (SUBLANES, LRU_WIDTH)))
                last = slice((g + 1) * SUBLANES - 1, (g + 1) * SUBLANES)
                h = a[last, :] * h + b[last, :]
                p = p * a[last, :]
            b = b + a * jnp.concatenate(h_in, axis=0)
            a = a * jnp.concatenate(p_in, axis=0)
            yield 0.36 / (chunks * n_scan)
        a_cum[j], b_loc[j] = a, b

    out = [[_dot(probs[j][h], vh[j][h]) + _dot(q_dec[j][h], s_in[j][h]) for h in heads]
           for j in J]
    yield 0.0

    carry = h_src[...]
    for j in J:
        hidden = b_loc[j] + a_cum[j] * carry
        carry = hidden[C - 1:C, :]
        xg = load(rows[j], slice(xg0, xg0 + LRU_WIDTH))
        store(rows[j], slice(RET_VALUE, RET_VALUE + LRU_WIDTH), hidden * jax.nn.gelu(xg))
        yield 0.08 / chunks
        for h in heads:
            gh = load(rows[j], slice(g0 + h * RET_DV, g0 + (h + 1) * RET_DV))
            store(rows[j], slice(h * RET_DV, (h + 1) * RET_DV),
                  jax.nn.silu(gh) * _head_rms(out[j][h], pace(EPS)))
            yield 0.03 / chunks
    h_dst[...] = carry


def _gla_mixer_tile(C, chunks, load, store, s_src, s_dst, nw, pace=_no_pace):
    k0 = GLA_KEY
    v0 = 2 * GLA_KEY
    r0 = v0 + GLA_VALUE
    la0 = r0 + GLA_VALUE
    ri = lax.broadcasted_iota(jnp.int32, (C, C), 0)
    ci = lax.broadcasted_iota(jnp.int32, (C, C), 1)
    causal = ri >= ci
    di = lax.broadcasted_iota(jnp.int32, (GLA_DK, GLA_DK), 0)
    dj = lax.broadcasted_iota(jnp.int32, (GLA_DK, GLA_DK), 1)
    eye = di == dj
    heads = range(GLA_HEADS)
    key = [slice(h * GLA_DK, (h + 1) * GLA_DK) for h in heads]
    J = range(chunks)
    rows = [slice(j * C, (j + 1) * C) for j in J]

    q_in, k_in, q_st, k_st, decay_col, vh = ([None] * chunks for _ in range(6))
    for j in J:
        b = load(rows[j], slice(la0, la0 + GLA_KEY))
        d = 1
        while d < C:
            b = b + _shift_rows(b, d, pace(0.0))
            d *= 2
        yield 0.12 / chunks
        b_mid = pace(b[C // 2:C // 2 + 1, :])
        b_last = b[C - 1:C, :]
        q = load(rows[j], slice(0, GLA_KEY)) * (GLA_DK ** -0.5)
        k = load(rows[j], slice(k0, k0 + GLA_KEY))
        q_in[j] = (q * jnp.exp(b - b_mid)).astype(BF16)
        k_in[j] = (k * jnp.exp(b_mid - b)).astype(BF16)
        q_st[j] = (q * jnp.exp(b)).astype(BF16)
        k_st[j] = (k * jnp.exp(b_last - b)).astype(BF16)
        s_decay = jnp.exp(b_last)
        decay_col[j] = [jnp.sum(jnp.where(eye, s_decay[:, key[h]], 0.0), axis=1, keepdims=True)
                        for h in heads]
        vh[j] = [load(rows[j], slice(v0 + h * GLA_DV, v0 + (h + 1) * GLA_DV)).astype(BF16)
                 for h in heads]
        yield 0.28 / chunks

    scores = [[_dot_nt(q_in[j][:, key[h]], k_in[j][:, key[h]]) for h in heads] for j in J]
    kv = [[_dot_tn(k_st[j][:, key[h]], vh[j][h]) for h in heads] for j in J]
    yield 0.0

    probs, s_in = [None] * chunks, [None] * chunks
    s_cur = [s_src[h] for h in heads]
    for j in J:
        probs[j] = [jnp.where(causal, scores[j][h], 0.0).astype(BF16) for h in heads]
        s_in[j] = [s_cur[h].astype(BF16) for h in heads]
        s_cur = [s_cur[h] * decay_col[j][h] + kv[j][h] for h in heads]
        yield 0.16 / chunks
    for h in heads:
        s_dst[h] = s_cur[h]

    out = [[_dot(probs[j][h], vh[j][h]) + _dot(q_st[j][:, key[h]], s_in[j][h]) for h in heads]
           for j in J]
    yield 0.0

    for j in J:
        for h in heads:
            rh = load(rows[j], slice(r0 + h * GLA_DV, r0 + (h + 1) * GLA_DV))
            store(rows[j], slice(h * GLA_DV, (h + 1) * GLA_DV),
                  jax.nn.silu(rh) * (_head_rms(out[j][h], pace(EPS)) * nw))
            yield 0.11 / chunks


def _even_mixer_kernel(pos0, chunk, seqs, *refs):
    (proj_ref, cos_ref, sin_ref, s0_ref, h0_ref, cb_ref,
     cw_ref, cbias_ref, wa_ref, ba_ref, wi_ref, bi_ref, lam_ref) = refs[:13]
    refs = refs[14:]
    mix_ref, snew_ref, hnew_ref, cnew_ref, tail_scr = refs
    weights = (cw_ref, cbias_ref, wa_ref, ba_ref, wi_ref, bi_ref, lam_ref)
    for g in range(seqs):
        tail = tail_scr.at[g]
        tail[...] = jnp.zeros((SUBLANES, LRU_WIDTH), F32)
        tail[SUBLANES - (CONV_W - 1):SUBLANES, :] = cb_ref[g]

        def load(rows, cols, g=g):
            return proj_ref[g, rows, cols]

        def store(rows, cols, val, g=g):
            mix_ref[g, rows, cols] = val

        _run(_even_mixer_tile(chunk, 1, pos0, load, store, cos_ref, sin_ref,
                              s0_ref.at[g], snew_ref.at[g], h0_ref.at[g], hnew_ref.at[g],
                              tail, weights))
        cnew_ref[g] = tail[SUBLANES - (CONV_W - 1):SUBLANES, :]


def _even_mixer(proj, pos0, cos, sin, ret_s, lru_h, conv_buf, w, layer, prev_ret):
    B, T, _ = proj.shape
    i = layer // 2
    n_even = w["w_in_even"].shape[0]
    G = SAMPLE_SEQS
    ret_block = (None, G, RET_HEADS, RET_DK, RET_DV)
    in_specs = [pl.BlockSpec((G, T, EVEN_IN), lambda b: (b, 0, 0)),
                _const_spec((T, RET_KEY)),
                _const_spec((T, RET_KEY)),
                pl.BlockSpec(ret_block, lambda b: (i, b, 0, 0, 0)),
                pl.BlockSpec((None, G, 1, LRU_WIDTH), lambda b: (i, b, 0, 0)),
                pl.BlockSpec((None, G, CONV_W - 1, LRU_WIDTH), lambda b: (i, b, 0, 0)),
                _layer_spec((CONV_W, LRU_WIDTH), i),
                _layer_spec((1, LRU_WIDTH), i),
                _layer_spec((LRU_WIDTH, LRU_WIDTH), i),
                _layer_spec((1, LRU_WIDTH), i),
                _layer_spec((LRU_WIDTH, LRU_WIDTH), i),
                _layer_spec((1, LRU_WIDTH), i),
                _layer_spec((1, LRU_WIDTH), i)]
    args = [proj, cos, sin, ret_s, lru_h, conv_buf, w["conv_w"], w["conv_b"],
            w["wa"], w["ba"], w["wi"], w["bi"], w["lam"]]
    in_specs.append(pl.BlockSpec(memory_space=pl.ANY))
    args.append(prev_ret)
    aliases = {len(args) - 1: 1}
    out_shapes = (jax.ShapeDtypeStruct((B, T, D_MODEL), F32),
                  jax.ShapeDtypeStruct((n_even, B, RET_HEADS, RET_DK, RET_DV), F32),
                  jax.ShapeDtypeStruct((B, 1, LRU_WIDTH), F32),
                  jax.ShapeDtypeStruct((B, CONV_W - 1, LRU_WIDTH), F32))
    return pl.pallas_call(
        functools.partial(_even_mixer_kernel, pos0, T, G),
        grid=(B // G,),
        in_specs=in_specs,
        out_specs=(pl.BlockSpec((G, T, D_MODEL), lambda b: (b, 0, 0)),
                   pl.BlockSpec(ret_block, lambda b: (i, b, 0, 0, 0)),
                   pl.BlockSpec((G, 1, LRU_WIDTH), lambda b: (b, 0, 0)),
                   pl.BlockSpec((G, CONV_W - 1, LRU_WIDTH), lambda b: (b, 0, 0))),
        out_shape=out_shapes,
        scratch_shapes=[pltpu.VMEM((G, SUBLANES, LRU_WIDTH), F32)],
        input_output_aliases=aliases,
        compiler_params=_params("parallel"),
        name="even_mixer",
    )(*args)


def _gla_mixer_kernel(chunk, seqs, *refs):
    proj_ref, s0_ref, nw_ref = refs[:3]
    refs = refs[4:]
    mix_ref, snew_ref = refs
    nw = nw_ref[...]
    for g in range(seqs):
        def load(rows, cols, g=g):
            return proj_ref[g, rows, cols]

        def store(rows, cols, val, g=g):
            mix_ref[g, rows, cols] = val

        _run(_gla_mixer_tile(chunk, 1, load, store, s0_ref.at[g], snew_ref.at[g], nw))


def _gla_mixer(proj, gla_s, w, layer, prev_gla):
    B, T, _ = proj.shape
    i = layer // 2
    n_odd = w["w_in_odd"].shape[0]
    G = SAMPLE_SEQS
    state = (None, G, GLA_HEADS, GLA_DK, GLA_DV)
    in_specs = [pl.BlockSpec((G, T, ODD_OUT), lambda b: (b, 0, 0)),
                pl.BlockSpec(state, lambda b: (i, b, 0, 0, 0)),
                _layer_spec((1, GLA_DV), i)]
    args = [proj, gla_s, w["gla_norm"]]
    in_specs.append(pl.BlockSpec(memory_space=pl.ANY))
    args.append(prev_gla)
    aliases = {len(args) - 1: 1}
    return pl.pallas_call(
        functools.partial(_gla_mixer_kernel, T, G),
        grid=(B // G,),
        in_specs=in_specs,
        out_specs=(pl.BlockSpec((G, T, D_MODEL), lambda b: (b, 0, 0)),
                   pl.BlockSpec(state, lambda b: (i, b, 0, 0, 0))),
        out_shape=(jax.ShapeDtypeStruct((B, T, D_MODEL), F32),
                   jax.ShapeDtypeStruct((n_odd, B, GLA_HEADS, GLA_DK, GLA_DV), F32)),
        input_output_aliases=aliases,
        compiler_params=_params("parallel"),
        name="gla_mixer",
    )(*args)


def _fused_layer_kernel(even, final, chunk, rows, tiles_per_seq, n_tiles, slot, has_prev, *refs):
    it = iter(refs)
    xlead_ref, xlag_ref, gmix_ref, win_ref = (next(it) for _ in range(4))
    if even:
        cos_ref, sin_ref = next(it), next(it)
        mixer_weights = tuple(next(it) for _ in range(7))
    else:
        wg1_ref, wg2_ref, bg_ref, nw_ref = (next(it) for _ in range(4))
    wo_ref, gmlp_ref, wu_ref, wd_ref, gf_ref = (next(it) for _ in range(5))
    if has_prev:
        next(it)
    out_ref, snew_ref = next(it), next(it)
    if even:
        hnew_ref, cnew_ref = next(it), next(it)
    if not has_prev:
        zeros_ref = next(it)
    proj_scr = (next(it), next(it))
    mix_scr = (next(it), next(it))
    x1_scr, u_scr, s_scr = next(it), next(it), next(it)
    if even:
        h_scr, tail_scr = next(it), next(it)

    s = pl.program_id(0)
    tile_in_seq = lax.rem(jnp.clip(s - 1, 0, n_tiles - 1), tiles_per_seq)

    @pl.when(s == 0)
    def _():
        proj_scr[1][...] = jnp.zeros(proj_scr[1].shape, F32)
        mix_scr[0][...] = jnp.zeros(mix_scr[0].shape, BF16)

    @pl.when((s == 0) | (lax.rem(s + tiles_per_seq - 1, tiles_per_seq) == 0))
    def _():
        s_scr[...] = jnp.zeros(s_scr.shape, F32)
        if even:
            h_scr[...] = jnp.zeros(h_scr.shape, F32)
            tail_scr[...] = jnp.zeros(tail_scr.shape, F32)

    main_cols = EVEN_IN if even else ODD_MAIN
    macs_per_row = D_MODEL * (main_cols + D_MODEL + 2 * D_FF)

    def matrix_stream(proj_a, mix_c, pace):
        P = MXU_PIECE
        h_in = _rmsnorm(xlead_ref[...], gmix_ref[...]).astype(BF16)
        m = mix_c[...]
        for n0 in range(0, D_MODEL, P):
            x1 = xlag_ref[:, n0:n0 + P] + _dot(m, wo_ref[:, n0:n0 + P])
            x1_scr[:, n0:n0 + P] = x1
            pace.after(x1)
            yield D_MODEL * P / macs_per_row
        h_mlp = _rmsnorm(x1_scr[...], gmlp_ref[...]).astype(BF16)
        if not even:
            glr = _dot(h_in, wg1_ref[...]).astype(BF16)
        for n0 in range(0, main_cols, P):
            p = _dot(h_in, win_ref[:, n0:n0 + P])
            proj_a[:, n0:n0 + P] = p
            pace.after(p)
            if not even and n0 == P:
                z = _dot(glr, wg2_ref[...]) + bg_ref[...]
            yield D_MODEL * P / macs_per_row
        if not even:
            proj_a[:, ODD_MAIN:] = _scaled_log_sigmoid(z)
            yield 0.0
        for n0 in range(0, D_FF, P):
            u = jnp.square(jnp.maximum(_dot(h_mlp, wu_ref[:, n0:n0 + P]), 0.0))
            u_scr[:, n0:n0 + P] = u.astype(BF16)
            pace.after(u)
            yield D_MODEL * P / macs_per_row
        for n0 in range(0, D_MODEL, P):
            x2 = x1_scr[:, n0:n0 + P] + _dot(u_scr[...], wd_ref[:, n0:n0 + P])
            if final:
                x1_scr[:, n0:n0 + P] = x2
            else:
                out_ref[:, n0:n0 + P] = x2
            pace.after(x2)
            yield D_FF * P / macs_per_row
        if final:
            out_ref[...] = _rmsnorm(x1_scr[...], gf_ref[...])

    def step(par):
        proj_a, proj_b = proj_scr[par], proj_scr[1 - par]
        mix_b, mix_c = mix_scr[1 - par], mix_scr[par]

        def load(r, c):
            return proj_b[r, c]

        def store(r, c, val):
            mix_b[r, c] = val.astype(BF16)

        pace = _Pace()
        if even:
            vector_stream = _even_mixer_tile(chunk, rows // chunk, tile_in_seq * rows, load,
                                             store, cos_ref, sin_ref, s_scr, s_scr, h_scr, h_scr,
                                             tail_scr, mixer_weights, pace)
        else:
            vector_stream = _gla_mixer_tile(chunk, rows // chunk, load, store, s_scr, s_scr,
                                            nw_ref[...], pace)
        _interleave(matrix_stream(proj_a, mix_c, pace), vector_stream)

    parity = lax.rem(s, 2)
    pl.when(parity == 0)(functools.partial(step, 0))
    pl.when(parity == 1)(functools.partial(step, 1))

    if not has_prev:
        zeros_ref[...] = jnp.zeros(zeros_ref.shape, F32)

    @pl.when((s >= 1) & (s <= n_tiles) & (lax.rem(s, tiles_per_seq) == 0))
    def _():
        if has_prev:
            snew_ref[0] = s_scr[...]
        else:
            for other in range(snew_ref.shape[0]):
                snew_ref[other, 0] = s_scr[...] if other == slot else jnp.zeros(s_scr.shape, F32)
        if even:
            hnew_ref[0] = h_scr[...]
            cnew_ref[0] = tail_scr[SUBLANES - (CONV_W - 1):SUBLANES, :]


def _fused_layer(x, w, layer, cos, sin, prev_state, seq_len, n_short):
    n_rows = x.shape[0]
    B = n_rows // seq_len
    R = FUSED_ROWS
    even = layer % 2 == 0
    final = layer == DEPTH - 1
    i = layer // 2
    C = math.gcd(seq_len, RET_CHUNK if even else GLA_CHUNK)
    tps = seq_len // R
    NT = n_rows // R
    has_prev = prev_state is not None

    def lead(s):
        return (jnp.minimum(s, NT - 1), 0)

    def mid_tile(s):
        return jnp.clip(s - 1, 0, NT - 1)

    def lag(s):
        return (jnp.clip(s - 2, 0, NT - 1), 0)

    def seq3(s):
        return (mid_tile(s) // tps, 0, 0)

    in_specs = [pl.BlockSpec((R, D_MODEL), lead),
                pl.BlockSpec((R, D_MODEL), lag),
                _layer_spec((1, D_MODEL), layer)]
    args = [x, x, w["norm_mix"]]
    if even:
        in_specs += [_layer_spec((D_MODEL, EVEN_IN), i),
                     pl.BlockSpec((R, RET_KEY), lambda s: (mid_tile(s) % tps, 0)),
                     pl.BlockSpec((R, RET_KEY), lambda s: (mid_tile(s) % tps, 0)),
                     _layer_spec((CONV_W, LRU_WIDTH), i),
                     _layer_spec((1, LRU_WIDTH), i),
                     _layer_spec((LRU_WIDTH, LRU_WIDTH), i),
                     _layer_spec((1, LRU_WIDTH), i),
                     _layer_spec((LRU_WIDTH, LRU_WIDTH), i),
                     _layer_spec((1, LRU_WIDTH), i),
                     _layer_spec((1, LRU_WIDTH), i)]
        args += [w["w_in_even"], cos, sin, w["conv_w"], w["conv_b"], w["wa"], w["ba"],
                 w["wi"], w["bi"], w["lam"]]
        w_out = w["w_out_even"]
        state_shape = (RET_HEADS, RET_DK, RET_DV)
        n_stack = w["w_in_even"].shape[0]
        proj_cols = EVEN_IN
    else:
        in_specs += [_layer_spec((D_MODEL, ODD_MAIN), i),
                     _layer_spec((D_MODEL, LANES), i),
                     _layer_spec((LANES, GLA_KEY), i),
                     _layer_spec((1, GLA_KEY), i),
                     _layer_spec((1, GLA_DV), i)]
        args += [w["w_in_odd"], w["wg1"], w["wg2"], w["bg"], w["gla_norm"]]
        w_out = w["w_out_odd"]
        state_shape = (GLA_HEADS, GLA_DK, GLA_DV)
        n_stack = w["w_in_odd"].shape[0]
        proj_cols = ODD_OUT
    in_specs += [_layer_spec((D_MODEL, D_MODEL), i),
                 _layer_spec((1, D_MODEL), layer),
                 _layer_spec((D_MODEL, D_FF), layer),
                 _layer_spec((D_FF, D_MODEL), layer),
                 _const_spec((1, D_MODEL))]
    args += [w_out, w["norm_mlp"], w["w_up"], w["w_down"], w["norm_final"]]
    aliases = {}
    if has_prev:
        in_specs.append(pl.BlockSpec(memory_space=pl.ANY))
        args.append(prev_state)
        aliases = {len(args) - 1: 1}

    if has_prev:
        state_spec = pl.BlockSpec((None, 1) + state_shape,
                                  lambda s: (i, mid_tile(s) // tps, 0, 0, 0))
    else:
        state_spec = pl.BlockSpec((n_stack, 1) + state_shape,
                                  lambda s: (0, mid_tile(s) // tps, 0, 0, 0))
    out_specs = [pl.BlockSpec((R, D_MODEL), lag), state_spec]
    out_shapes = [jax.ShapeDtypeStruct((n_rows, D_MODEL), F32),
                  jax.ShapeDtypeStruct((n_stack, B) + state_shape, F32)]
    scratch = [pltpu.VMEM((R, proj_cols), F32), pltpu.VMEM((R, proj_cols), F32),
               pltpu.VMEM((R, D_MODEL), BF16), pltpu.VMEM((R, D_MODEL), BF16),
               pltpu.VMEM((R, D_MODEL), F32), pltpu.VMEM((R, D_FF), BF16),
               pltpu.VMEM(state_shape, F32)]
    if even:
        out_specs += [pl.BlockSpec((1, 1, LRU_WIDTH), seq3),
                      pl.BlockSpec((1, CONV_W - 1, LRU_WIDTH), seq3)]
        out_shapes += [jax.ShapeDtypeStruct((B, 1, LRU_WIDTH), F32),
                       jax.ShapeDtypeStruct((B, CONV_W - 1, LRU_WIDTH), F32)]
        scratch += [pltpu.VMEM((1, LRU_WIDTH), F32),
                    pltpu.VMEM((SUBLANES, LRU_WIDTH), F32)]
    if not has_prev:
        per_step = n_short // NT
        assert per_step * NT == n_short
        out_specs.append(pl.BlockSpec((n_stack, per_step) + state_shape,
                                      lambda s: (0, jnp.minimum(s, NT - 1), 0, 0, 0)))
        out_shapes.append(jax.ShapeDtypeStruct((n_stack, n_short) + state_shape, F32))
    return pl.pallas_call(
        functools.partial(_fused_layer_kernel, even, final, C, R, tps, NT, i, has_prev),
        grid=(NT + 2,),
        in_specs=in_specs,
        out_specs=tuple(out_specs),
        out_shape=tuple(out_shapes),
        scratch_shapes=scratch,
        input_output_aliases=aliases,
        compiler_params=_params("arbitrary"),
        name="layer_even" if even else "layer_odd",
    )(*args)


def _rope_tables(pos0, T):
    half = RET_DK // 2
    inv = ROPE_BASE ** (-jnp.arange(half, dtype=F32) / half)
    ang = (pos0 + jnp.arange(T, dtype=jnp.int32)).astype(F32)[:, None] * inv[None, :]
    cos = jnp.cos(ang)
    sin = jnp.sin(ang)
    cos_full = jnp.tile(jnp.concatenate([cos, cos], axis=-1), (1, RET_HEADS))
    sin_signed = jnp.tile(jnp.concatenate([-sin, sin], axis=-1), (1, RET_HEADS))
    return cos_full, sin_signed


def _block_diag(w):
    nl, nb, c, d = w.shape
    eye = jnp.eye(nb, dtype=w.dtype)
    return (eye[None, :, None, :, None] * w[:, :, :, None, :]).reshape(nl, nb * c, nb * d)


def _prepare_weights(norm_mix, norm_mlp, norm_final, w_in_even, w_out_even, conv_w, conv_b,
                     lru_w_a, lru_b_a, lru_w_i, lru_b_i, lru_lambda,
                     w_in_odd, gla_w_gate2, gla_b_gate, gla_norm, w_out_odd, w_up, w_down):
    n_even = w_in_even.shape[0]
    n_odd = w_in_odd.shape[0]
    pad1 = LANES - GLA_RANK
    return dict(
        norm_mix=norm_mix.reshape(DEPTH, 1, D_MODEL),
        norm_mlp=norm_mlp.reshape(DEPTH, 1, D_MODEL),
        norm_final=norm_final.reshape(1, D_MODEL),
        w_in_even=w_in_even.astype(BF16),
        w_out_even=w_out_even.astype(BF16),
        conv_w=conv_w,
        conv_b=conv_b.reshape(n_even, 1, LRU_WIDTH),
        wa=_block_diag(lru_w_a).astype(BF16),
        ba=lru_b_a.reshape(n_even, 1, LRU_WIDTH),
        wi=_block_diag(lru_w_i).astype(BF16),
        bi=lru_b_i.reshape(n_even, 1, LRU_WIDTH),
        lam=lru_lambda.reshape(n_even, 1, LRU_WIDTH),
        w_in_odd=w_in_odd.astype(BF16),
        wg1=jnp.pad(w_in_odd[:, :, ODD_MAIN:], ((0, 0), (0, 0), (0, pad1))).astype(BF16),
        wg2=jnp.pad(gla_w_gate2, ((0, 0), (0, pad1), (0, 0))).astype(BF16),
        bg=gla_b_gate.reshape(n_odd, 1, GLA_KEY),
        gla_norm=gla_norm.reshape(n_odd, 1, GLA_DV),
        w_out_odd=w_out_odd.astype(BF16),
        w_up=w_up.astype(BF16),
        w_down=w_down.astype(BF16),
    )


def _prompt_trunk(x, w, n_short):
    B, T, _ = x.shape
    xf = x.reshape(B * T, D_MODEL)
    cos, sin = _rope_tables(0, T)
    ret_new, gla_new = None, None
    lrus, convs, zeros = [], [], []
    for layer in range(DEPTH):
        prev = ret_new if layer % 2 == 0 else gla_new
        outs = _fused_layer(xf, w, layer, cos, sin, prev, T, n_short)
        if prev is None:
            zeros.append(outs[-1])
            outs = outs[:-1]
        if layer % 2 == 0:
            xf, ret_new, lh, cb = outs
            lrus.append(lh.reshape(B, LRU_WIDTH))
            convs.append(cb)
        else:
            xf, gla_new = outs
    states = (ret_new, jnp.stack(lrus), jnp.stack(convs), gla_new)
    return xf.reshape(B, T, D_MODEL), states, zeros


def _sample_trunk(x, pos0, ret_s, lru_h, conv_buf, gla_s, ret_new, gla_new, w):
    B, T, _ = x.shape
    xf = x.reshape(B * T, D_MODEL)
    cos, sin = _rope_tables(pos0, T)
    lru_h = lru_h.reshape(lru_h.shape[0], B, 1, LRU_WIDTH)
    lrus, convs = [], []
    for layer in range(DEPTH):
        if layer % 2 == 0:
            proj = _norm_proj_even(xf, w, layer)
            mix, ret_new, lh, cb = _even_mixer(proj.reshape(B, T, EVEN_IN), pos0, cos, sin,
                                               ret_s, lru_h, conv_buf, w, layer, ret_new)
            lrus.append(lh.reshape(B, LRU_WIDTH))
            convs.append(cb)
        else:
            proj = _norm_proj_odd(xf, w, layer)
            mix, gla_new = _gla_mixer(proj.reshape(B, T, ODD_OUT), gla_s, w, layer, gla_new)
        xf = _out_mlp(xf, mix.reshape(B * T, D_MODEL), w, layer)
    return xf.reshape(B, T, D_MODEL), ret_new, jnp.stack(lrus), jnp.stack(convs), gla_new


def kernel(x_prompt, x_sample, state_ret, state_lru, state_conv, state_gla, norm_mix, norm_mlp, norm_final, w_in_even, w_out_even, conv_w, conv_b, lru_w_a, lru_b_a, lru_w_i, lru_b_i, lru_lambda, w_in_odd, gla_w_gate2, gla_b_gate, gla_norm, w_out_odd, w_up, w_down):
    w = _prepare_weights(norm_mix, norm_mlp, norm_final, w_in_even, w_out_even, conv_w, conv_b,
                         lru_w_a, lru_b_a, lru_w_i, lru_b_i, lru_lambda,
                         w_in_odd, gla_w_gate2, gla_b_gate, gla_norm, w_out_odd, w_up, w_down)
    y_p, (ret_p, lru_p, conv_p, gla_p), (ret_zero, gla_zero) = _prompt_trunk(
        x_prompt, w, x_sample.shape[0])
    y_s, ret_s, lru_s, conv_s, gla_s = _sample_trunk(x_sample, PAST_LEN, state_ret, state_lru,
                                                     state_conv, state_gla, ret_zero, gla_zero, w)
    return (y_p, y_s, ret_p, ret_s, lru_p, lru_s, conv_p, conv_s, gla_p, gla_s)
```

```python
import functools
import math

import jax
import jax.numpy as jnp
from jax import lax
from jax.experimental import pallas as pl
from jax.experimental.pallas import tpu as pltpu

F32 = jnp.float32
BF16 = jnp.bfloat16

D_MODEL = 1024
DEPTH = 4
PAST_LEN = 16384
EPS = 1e-6
ROPE_BASE = 10000.0

RET_HEADS = 4
RET_DK = 64
RET_DV = 128
RET_KEY = RET_HEADS * RET_DK
RET_VALUE = RET_HEADS * RET_DV
RET_CHUNK = 128

LRU_WIDTH = 512
CONV_W = 4
LRU_C = 8.0

GLA_HEADS = 4
GLA_KEY = 512
GLA_VALUE = 1024
GLA_DK = GLA_KEY // GLA_HEADS
GLA_DV = GLA_VALUE // GLA_HEADS
GLA_RANK = 16
GLA_TAU = 16.0
GLA_CHUNK = 64

D_FF = 4 * D_MODEL
FF_CHUNK = 1024

EVEN_IN = 2 * RET_KEY + 2 * RET_VALUE + 2 * LRU_WIDTH
ODD_MAIN = 2 * GLA_KEY + 2 * GLA_VALUE
ODD_OUT = ODD_MAIN + GLA_KEY

LANES = 128
SUBLANES = 8
ROW_TILE = 512
FUSED_ROWS = 256
MXU_PIECE = 256
SAMPLE_SEQS = 8
VMEM_LIMIT = 56 * 1024 * 1024


def _dot(a, b):
    return jnp.dot(a.astype(BF16), b.astype(BF16), preferred_element_type=F32)


def _dot_nt(a, b):
    return lax.dot_general(a.astype(BF16), b.astype(BF16), (((1,), (1,)), ((), ())),
                           preferred_element_type=F32)


def _dot_tn(a, b):
    return lax.dot_general(a.astype(BF16), b.astype(BF16), (((0,), (0,)), ((), ())),
                           preferred_element_type=F32)


def _rmsnorm(x, g):
    return x * lax.rsqrt(jnp.mean(x * x, axis=-1, keepdims=True) + EPS) * g


def _head_rms(x, eps=EPS):
    return x * lax.rsqrt(jnp.mean(x * x, axis=-1, keepdims=True) + eps)


def _shift_rows(x, d, fill):
    row = lax.broadcasted_iota(jnp.int32, x.shape, 0)
    return jnp.where(row >= d, pltpu.roll(x, d, axis=0), fill)


def _layer_spec(shape, layer):
    n = len(shape)
    return pl.BlockSpec((None,) + tuple(shape), lambda *_: (layer,) + (0,) * n,
                        pipeline_mode=pl.Buffered(1))


def _const_spec(shape):
    n = len(shape)
    return pl.BlockSpec(shape, lambda *_: (0,) * n, pipeline_mode=pl.Buffered(1))


def _params(*semantics):
    return pltpu.CompilerParams(dimension_semantics=semantics, vmem_limit_bytes=VMEM_LIMIT)


def _scaled_log_sigmoid(z):
    return (jnp.minimum(z, 0.0) - jnp.log(1.0 + jnp.exp(-jnp.abs(z)))) / GLA_TAU


def _log_gate(h, wg1_ref, wg2_ref, bg_ref):
    glr = _dot(h, wg1_ref[...])
    return _scaled_log_sigmoid(_dot(glr, wg2_ref[...]) + bg_ref[...])


def _norm_proj_even_kernel(x_ref, g_ref, w_ref, o_ref):
    h = _rmsnorm(x_ref[...], g_ref[...])
    o_ref[...] = _dot(h, w_ref[...])


def _norm_proj_odd_kernel(x_ref, g_ref, w_ref, wg1_ref, wg2_ref, bg_ref, o_ref):
    h = _rmsnorm(x_ref[...], g_ref[...]).astype(BF16)
    o_ref[:, :ODD_MAIN] = _dot(h, w_ref[...])
    o_ref[:, ODD_MAIN:] = _log_gate(h, wg1_ref, wg2_ref, bg_ref)


def _out_mlp_kernel(final, x_ref, m_ref, wo_ref, g_ref, wu_ref, wd_ref, gf_ref, o_ref,
                    h_scr, acc_scr):
    c = pl.program_id(0)

    @pl.when(c == 0)
    def _():
        x1 = x_ref[...] + _dot(m_ref[...], wo_ref[...])
        acc_scr[...] = x1
        h_scr[...] = _rmsnorm(x1, g_ref[...]).astype(BF16)

    u = jnp.square(jnp.maximum(_dot(h_scr[...], wu_ref[...]), 0.0))
    acc_scr[...] += _dot(u, wd_ref[...])

    @pl.when(c == pl.num_programs(0) - 1)
    def _():
        x2 = acc_scr[...]
        o_ref[...] = _rmsnorm(x2, gf_ref[...]) if final else x2


def _norm_proj_even(x, w, layer):
    n = x.shape[0]
    tm = min(ROW_TILE, n)
    return pl.pallas_call(
        _norm_proj_even_kernel,
        grid=(n // tm,),
        in_specs=[pl.BlockSpec((tm, D_MODEL), lambda i: (i, 0)),
                  _layer_spec((1, D_MODEL), layer),
                  _layer_spec((D_MODEL, EVEN_IN), layer // 2)],
        out_specs=pl.BlockSpec((tm, EVEN_IN), lambda i: (i, 0)),
        out_shape=jax.ShapeDtypeStruct((n, EVEN_IN), F32),
        compiler_params=_params("parallel"),
        name="norm_proj_even",
    )(x, w["norm_mix"], w["w_in_even"])


def _norm_proj_odd(x, w, layer):
    n = x.shape[0]
    tm = min(ROW_TILE, n)
    i = layer // 2
    return pl.pallas_call(
        _norm_proj_odd_kernel,
        grid=(n // tm,),
        in_specs=[pl.BlockSpec((tm, D_MODEL), lambda i: (i, 0)),
                  _layer_spec((1, D_MODEL), layer),
                  _layer_spec((D_MODEL, ODD_MAIN), i),
                  _layer_spec((D_MODEL, LANES), i),
                  _layer_spec((LANES, GLA_KEY), i),
                  _layer_spec((1, GLA_KEY), i)],
        out_specs=pl.BlockSpec((tm, ODD_OUT), lambda i: (i, 0)),
        out_shape=jax.ShapeDtypeStruct((n, ODD_OUT), F32),
        compiler_params=_params("parallel"),
        name="norm_proj_odd",
    )(x, w["norm_mix"], w["w_in_odd"], w["wg1"], w["wg2"], w["bg"])


def _out_mlp(x, mix, w, layer):
    n = x.shape[0]
    final = layer == DEPTH - 1
    w_out = w["w_out_even"] if layer % 2 == 0 else w["w_out_odd"]
    return pl.pallas_call(
        functools.partial(_out_mlp_kernel, final),
        grid=(D_FF // FF_CHUNK,),
        in_specs=[_const_spec((n, D_MODEL)),
                  _const_spec((n, D_MODEL)),
                  _layer_spec((D_MODEL, D_MODEL), layer // 2),
                  _layer_spec((1, D_MODEL), layer),
                  pl.BlockSpec((None, D_MODEL, FF_CHUNK), lambda c: (layer, 0, c)),
                  pl.BlockSpec((None, FF_CHUNK, D_MODEL), lambda c: (layer, c, 0)),
                  _const_spec((1, D_MODEL))],
        out_specs=pl.BlockSpec((n, D_MODEL), lambda c: (0, 0)),
        out_shape=jax.ShapeDtypeStruct((n, D_MODEL), F32),
        scratch_shapes=[pltpu.VMEM((n, D_MODEL), BF16), pltpu.VMEM((n, D_MODEL), F32)],
        compiler_params=_params("arbitrary"),
        name="out_mlp_final" if final else "out_mlp",
    )(x, mix, w_out, w["norm_mlp"], w["w_up"], w["w_down"], w["norm_final"])


def _run(stream):
    for _ in stream:
        pass


def _interleave(*streams):
    done = [0.0] * len(streams)
    alive = list(range(len(streams)))
    while alive:
        i = min(alive, key=lambda j: done[j])
        try:
            done[i] += next(streams[i])
        except StopIteration:
            alive.remove(i)


class _Pace:
    def __init__(self):
        self.zero = None

    def after(self, value):
        t = value[0:1, 0:1]
        self.zero = jnp.where(t == t, 0.0, t)

    def __call__(self, x):
        return x if self.zero is None else x + self.zero


def _no_pace(x):
    return x


def _even_mixer_tile(C, chunks, pos_base, load, store, cos_ref, sin_ref,
                     s_src, s_dst, h_src, h_dst, tail_ref, weights, pace=_no_pace):
    cw_ref, cbias_ref, wa_ref, ba_ref, wi_ref, bi_ref, lam_ref = weights
    lane = lax.broadcasted_iota(jnp.int32, (C, RET_KEY), 1)
    first_half = (lane % RET_DK) < (RET_DK // 2)
    ri = lax.broadcasted_iota(jnp.int32, (C, C), 0)
    ci = lax.broadcasted_iota(jnp.int32, (C, C), 1)
    rel = (ri - ci).astype(F32)
    row = lax.broadcasted_iota(jnp.int32, (C, 1), 0).astype(F32)
    trow = lax.broadcasted_iota(jnp.int32, (C, LRU_WIDTH), 0)
    row8 = lax.broadcasted_iota(jnp.int32, (SUBLANES, LRU_WIDTH), 0)
    lam = lam_ref[...]
    softplus_neg_lam = jnp.maximum(-lam, 0.0) + jnp.log(1.0 + jnp.exp(-jnp.abs(lam)))
    log_decay = [math.log1p(-(2.0 ** (-5.0 - h))) for h in range(RET_HEADS)]
    decay_mask = [jnp.where(rel >= 0, jnp.exp(jnp.maximum(rel, 0.0) * lg), 0.0)
                  for lg in log_decay]
    q_decay = [jnp.exp((row + 1.0) * lg) for lg in log_decay]
    k_decay = [jnp.exp((C - 1.0 - row) * lg) for lg in log_decay]
    n_scan = max(1, int(math.log2(C)))

    v0 = 2 * RET_KEY
    g0 = v0 + RET_VALUE
    x0 = g0 + RET_VALUE
    xg0 = x0 + LRU_WIDTH
    heads = range(RET_HEADS)
    J = range(chunks)
    rows = [slice(j * C, (j + 1) * C) for j in J]

    def delayed(x, t, n):
        rolled = pltpu.roll(x, n, axis=0)
        head = jnp.where(row8 < n, pltpu.roll(t, n, axis=0), rolled[0:SUBLANES, :])
        if C == SUBLANES:
            return head
        return jnp.concatenate([head, rolled[SUBLANES:, :]], axis=0)

    qh, kh, q_dec, k_dec, vh, xc = ([None] * chunks for _ in range(6))
    tail = tail_ref[...]
    for j in J:
        cos = cos_ref[rows[j], :]
        sin = sin_ref[rows[j], :]

        def rope(x):
            partner = jnp.where(first_half,
                                pltpu.roll(x, RET_KEY - RET_DK // 2, axis=1),
                                pltpu.roll(x, RET_DK // 2, axis=1))
            return x * cos + partner * sin

        q = rope(load(rows[j], slice(0, RET_KEY)))
        k = rope(load(rows[j], slice(RET_KEY, 2 * RET_KEY))) * (RET_DK ** -0.5)
        q = [q[:, h * RET_DK:(h + 1) * RET_DK] for h in heads]
        k = [k[:, h * RET_DK:(h + 1) * RET_DK] for h in heads]
        q_dec[j] = [(q[h] * q_decay[h]).astype(BF16) for h in heads]
        k_dec[j] = [(k[h] * k_decay[h]).astype(BF16) for h in heads]
        qh[j] = [q[h].astype(BF16) for h in heads]
        kh[j] = [k[h].astype(BF16) for h in heads]
        vh[j] = [load(rows[j], slice(v0 + h * RET_DV, v0 + (h + 1) * RET_DV)).astype(BF16)
                 for h in heads]
        yield 0.08 / chunks

        x_new = load(rows[j], slice(x0, x0 + LRU_WIDTH))
        acc = pace(cbias_ref[...])
        for i in range(CONV_W - 1):
            acc = acc + delayed(x_new, tail, CONV_W - 1 - i) * cw_ref[i:i + 1, :]
        xc[j] = acc + x_new * cw_ref[CONV_W - 1:CONV_W, :]
        tail = x_new[C - SUBLANES:C, :]
        yield 0.10 / chunks
    tail_ref[...] = tail

    scores = [[_dot_nt(qh[j][h], kh[j][h]) for h in heads] for j in J]
    kv = [[_dot_tn(k_dec[j][h], vh[j][h]) for h in heads] for j in J]
    xc_all = xc[0] if chunks == 1 else jnp.concatenate(xc, axis=0)
    xc_all = xc_all.astype(BF16)
    diag = [slice(c, c + MXU_PIECE) for c in range(0, LRU_WIDTH, MXU_PIECE)]
    r_all = jnp.concatenate([_dot(xc_all[:, d], wa_ref[d, d]) for d in diag], axis=1)
    i_all = jnp.concatenate([_dot(xc_all[:, d], wi_ref[d, d]) for d in diag], axis=1)
    r_pre = [r_all[rows[j], :] for j in J]
    i_pre = [i_all[rows[j], :] for j in J]
    yield 0.0

    probs, s_in = [None] * chunks, [None] * chunks
    s_cur = [s_src[h] for h in heads]
    for j in J:
        probs[j] = [(scores[j][h] * decay_mask[h]).astype(BF16) for h in heads]
        s_in[j] = [s_cur[h].astype(BF16) for h in heads]
        s_cur = [s_cur[h] * math.exp(C * log_decay[h]) + kv[j][h] for h in heads]
        yield 0.06 / chunks
    for h in heads:
        s_dst[h] = s_cur[h]

    a_cum, b_loc = [None] * chunks, [None] * chunks
    for j in J:
        r = jax.nn.sigmoid(r_pre[j] + pace(ba_ref[...]))
        gate_i = jax.nn.sigmoid(i_pre[j] + bi_ref[...])
        log_a = -LRU_C * r * softplus_neg_lam
        a = jnp.exp(log_a)
        y = -jnp.tanh(log_a) * (a * a + 1.0)
        mult = jnp.where(y > 0.0, y * lax.rsqrt(y), 0.0)
        pos = trow + (pos_base + j * C)
        mult = jnp.where(pos == 0, 1.0, mult)
        b = mult * (gate_i * xc[j])
        yield 0.20 / chunks
        d = 1
        while d < C:
            b = a * _shift_rows(b, d, pace(0.0)) + b
            a = a * _shift_rows(a, d, pace(1.0))
            d *= 2
            yield 0.36 / (chunks * n_scan)
        a_cum[j], b_loc[j] = a, b

    out = [[_dot(probs[j][h], vh[j][h]) + _dot(q_dec[j][h], s_in[j][h]) for h in heads]
           for j in J]
    yield 0.0

    carry = h_src[...]
    for j in J:
        hidden = b_loc[j] + a_cum[j] * carry
        carry = hidden[C - 1:C, :]
        xg = load(rows[j], slice(xg0, xg0 + LRU_WIDTH))
        store(rows[j], slice(RET_VALUE, RET_VALUE + LRU_WIDTH),
              hidden.astype(BF16) * jax.nn.gelu(xg.astype(BF16)))
        yield 0.08 / chunks
        for h in heads:
            gh = load(rows[j], slice(g0 + h * RET_DV, g0 + (h + 1) * RET_DV))
            store(rows[j], slice(h * RET_DV, (h + 1) * RET_DV),
                  jax.nn.silu(gh.astype(BF16)) * _head_rms(out[j][h], pace(EPS)).astype(BF16))
            yield 0.03 / chunks
    h_dst[...] = carry


def _gla_mixer_tile(C, chunks, load, store, s_src, s_dst, nw, pace=_no_pace):
    k0 = GLA_KEY
    v0 = 2 * GLA_KEY
    r0 = v0 + GLA_VALUE
    la0 = r0 + GLA_VALUE
    ri = lax.broadcasted_iota(jnp.int32, (C, C), 0)
    ci = lax.broadcasted_iota(jnp.int32, (C, C), 1)
    causal = ri >= ci
    di = lax.broadcasted_iota(jnp.int32, (GLA_DK, GLA_DK), 0)
    dj = lax.broadcasted_iota(jnp.int32, (GLA_DK, GLA_DK), 1)
    eye = di == dj
    heads = range(GLA_HEADS)
    key = [slice(h * GLA_DK, (h + 1) * GLA_DK) for h in heads]
    J = range(chunks)
    rows = [slice(j * C, (j + 1) * C) for j in J]

    q_in, k_in, q_st, k_st, decay_col, vh = ([None] * chunks for _ in range(6))
    for j in J:
        b = load(rows[j], slice(la0, la0 + GLA_KEY))
        d = 1
        while d < C:
            b = b + _shift_rows(b, d, pace(0.0))
            d *= 2
        yield 0.12 / chunks
        b_mid = pace(b[C // 2:C // 2 + 1, :])
        b_last = b[C - 1:C, :]
        q = load(rows[j], slice(0, GLA_KEY)) * (GLA_DK ** -0.5)
        k = load(rows[j], slice(k0, k0 + GLA_KEY))
        q_in[j] = (q * jnp.exp(b - b_mid)).astype(BF16)
        k_in[j] = (k * jnp.exp(b_mid - b)).astype(BF16)
        q_st[j] = (q * jnp.exp(b)).astype(BF16)
        k_st[j] = (k * jnp.exp(b_last - b)).astype(BF16)
        s_decay = jnp.exp(b_last)
        decay_col[j] = [jnp.sum(jnp.where(eye, s_decay[:, key[h]], 0.0), axis=1, keepdims=True)
                        for h in heads]
        vh[j] = [load(rows[j], slice(v0 + h * GLA_DV, v0 + (h + 1) * GLA_DV)).astype(BF16)
                 for h in heads]
        yield 0.28 / chunks

    scores = [[_dot_nt(q_in[j][:, key[h]], k_in[j][:, key[h]]) for h in heads] for j in J]
    kv = [[_dot_tn(k_st[j][:, key[h]], vh[j][h]) for h in heads] for j in J]
    yield 0.0

    probs, s_in = [None] * chunks, [None] * chunks
    s_cur = [s_src[h] for h in heads]
    for j in J:
        probs[j] = [jnp.where(causal, scores[j][h], 0.0).astype(BF16) for h in heads]
        s_in[j] = [s_cur[h].astype(BF16) for h in heads]
        s_cur = [s_cur[h] * decay_col[j][h] + kv[j][h] for h in heads]
        yield 0.16 / chunks
    for h in heads:
        s_dst[h] = s_cur[h]

    out = [[_dot(probs[j][h], vh[j][h]) + _dot(q_st[j][:, key[h]], s_in[j][h]) for h in heads]
           for j in J]
    yield 0.0

    for j in J:
        for h in heads:
            rh = load(rows[j], slice(r0 + h * GLA_DV, r0 + (h + 1) * GLA_DV))
            store(rows[j], slice(h * GLA_DV, (h + 1) * GLA_DV),
                  jax.nn.silu(rh.astype(BF16)) * (_head_rms(out[j][h], pace(EPS)) * nw).astype(BF16))
            yield 0.11 / chunks


def _even_mixer_kernel(pos0, chunk, seqs, *refs):
    (proj_ref, cos_ref, sin_ref, s0_ref, h0_ref, cb_ref,
     cw_ref, cbias_ref, wa_ref, ba_ref, wi_ref, bi_ref, lam_ref) = refs[:13]
    refs = refs[14:]
    mix_ref, snew_ref, hnew_ref, cnew_ref, tail_scr = refs
    weights = (cw_ref, cbias_ref, wa_ref, ba_ref, wi_ref, bi_ref, lam_ref)
    for g in range(seqs):
        tail = tail_scr.at[g]
        tail[...] = jnp.zeros((SUBLANES, LRU_WIDTH), F32)
        tail[SUBLANES - (CONV_W - 1):SUBLANES, :] = cb_ref[g]

        def load(rows, cols, g=g):
            return proj_ref[g, rows, cols]

        def store(rows, cols, val, g=g):
            mix_ref[g, rows, cols] = val.astype(F32)

        _run(_even_mixer_tile(chunk, 1, pos0, load, store, cos_ref, sin_ref,
                              s0_ref.at[g], snew_ref.at[g], h0_ref.at[g], hnew_ref.at[g],
                              tail, weights))
        cnew_ref[g] = tail[SUBLANES - (CONV_W - 1):SUBLANES, :]


def _even_mixer(proj, pos0, cos, sin, ret_s, lru_h, conv_buf, w, layer, prev_ret):
    B, T, _ = proj.shape
    i = layer // 2
    n_even = w["w_in_even"].shape[0]
    G = SAMPLE_SEQS
    ret_block = (None, G, RET_HEADS, RET_DK, RET_DV)
    in_specs = [pl.BlockSpec((G, T, EVEN_IN), lambda b: (b, 0, 0)),
                _const_spec((T, RET_KEY)),
                _const_spec((T, RET_KEY)),
                pl.BlockSpec(ret_block, lambda b: (i, b, 0, 0, 0)),
                pl.BlockSpec((None, G, 1, LRU_WIDTH), lambda b: (i, b, 0, 0)),
                pl.BlockSpec((None, G, CONV_W - 1, LRU_WIDTH), lambda b: (i, b, 0, 0)),
                _layer_spec((CONV_W, LRU_WIDTH), i),
                _layer_spec((1, LRU_WIDTH), i),
                _layer_spec((LRU_WIDTH, LRU_WIDTH), i),
                _layer_spec((1, LRU_WIDTH), i),
                _layer_spec((LRU_WIDTH, LRU_WIDTH), i),
                _layer_spec((1, LRU_WIDTH), i),
                _layer_spec((1, LRU_WIDTH), i)]
    args = [proj, cos, sin, ret_s, lru_h, conv_buf, w["conv_w"], w["conv_b"],
            w["wa"], w["ba"], w["wi"], w["bi"], w["lam"]]
    in_specs.append(pl.BlockSpec(memory_space=pl.ANY))
    args.append(prev_ret)
    aliases = {len(args) - 1: 1}
    out_shapes = (jax.ShapeDtypeStruct((B, T, D_MODEL), F32),
                  jax.ShapeDtypeStruct((n_even, B, RET_HEADS, RET_DK, RET_DV), F32),
                  jax.ShapeDtypeStruct((B, 1, LRU_WIDTH), F32),
                  jax.ShapeDtypeStruct((B, CONV_W - 1, LRU_WIDTH), F32))
    return pl.pallas_call(
        functools.partial(_even_mixer_kernel, pos0, T, G),
        grid=(B // G,),
        in_specs=in_specs,
        out_specs=(pl.BlockSpec((G, T, D_MODEL), lambda b: (b, 0, 0)),
                   pl.BlockSpec(ret_block, lambda b: (i, b, 0, 0, 0)),
                   pl.BlockSpec((G, 1, LRU_WIDTH), lambda b: (b, 0, 0)),
                   pl.BlockSpec((G, CONV_W - 1, LRU_WIDTH), lambda b: (b, 0, 0))),
        out_shape=out_shapes,
        scratch_shapes=[pltpu.VMEM((G, SUBLANES, LRU_WIDTH), F32)],
        input_output_aliases=aliases,
        compiler_params=_params("parallel"),
        name="even_mixer",
    )(*args)


def _gla_mixer_kernel(chunk, seqs, *refs):
    proj_ref, s0_ref, nw_ref = refs[:3]
    refs = refs[4:]
    mix_ref, snew_ref = refs
    nw = nw_ref[...]
    for g in range(seqs):
        def load(rows, cols, g=g):
            return proj_ref[g, rows, cols]

        def store(rows, cols, val, g=g):
            mix_ref[g, rows, cols] = val.astype(F32)

        _run(_gla_mixer_tile(chunk, 1, load, store, s0_ref.at[g], snew_ref.at[g], nw))


def _gla_mixer(proj, gla_s, w, layer, prev_gla):
    B, T, _ = proj.shape
    i = layer // 2
    n_odd = w["w_in_odd"].shape[0]
    G = SAMPLE_SEQS
    state = (None, G, GLA_HEADS, GLA_DK, GLA_DV)
    in_specs = [pl.BlockSpec((G, T, ODD_OUT), lambda b: (b, 0, 0)),
                pl.BlockSpec(state, lambda b: (i, b, 0, 0, 0)),
                _layer_spec((1, GLA_DV), i)]
    args = [proj, gla_s, w["gla_norm"]]
    in_specs.append(pl.BlockSpec(memory_space=pl.ANY))
    args.append(prev_gla)
    aliases = {len(args) - 1: 1}
    return pl.pallas_call(
        functools.partial(_gla_mixer_kernel, T, G),
        grid=(B // G,),
        in_specs=in_specs,
        out_specs=(pl.BlockSpec((G, T, D_MODEL), lambda b: (b, 0, 0)),
                   pl.BlockSpec(state, lambda b: (i, b, 0, 0, 0))),
        out_shape=(jax.ShapeDtypeStruct((B, T, D_MODEL), F32),
                   jax.ShapeDtypeStruct((n_odd, B, GLA_HEADS, GLA_DK, GLA_DV), F32)),
        input_output_aliases=aliases,
        compiler_params=_params("parallel"),
        name="gla_mixer",
    )(*args)


def _fused_layer_kernel(even, final, chunk, rows, tiles_per_seq, n_tiles, slot, has_prev, *refs):
    it = iter(refs)
    xlead_ref, xlag_ref, gmix_ref, win_ref = (next(it) for _ in range(4))
    if even:
        cos_ref, sin_ref = next(it), next(it)
        mixer_weights = tuple(next(it) for _ in range(7))
    else:
        wg1_ref, wg2_ref, bg_ref, nw_ref = (next(it) for _ in range(4))
    wo_ref, gmlp_ref, wu_ref, wd_ref, gf_ref = (next(it) for _ in range(5))
    if has_prev:
        next(it)
    out_ref, snew_ref = next(it), next(it)
    if even:
        hnew_ref, cnew_ref = next(it), next(it)
    if not has_prev:
        zeros_ref = next(it)
    proj_scr = (next(it), next(it))
    mix_scr = (next(it), next(it))
    x1_scr, u_scr, s_scr = next(it), next(it), next(it)
    if even:
        h_scr, tail_scr = next(it), next(it)

    s = pl.program_id(0)
    tile_in_seq = lax.rem(jnp.clip(s - 1, 0, n_tiles - 1), tiles_per_seq)

    @pl.when(s == 0)
    def _():
        proj_scr[1][...] = jnp.zeros(proj_scr[1].shape, F32)
        mix_scr[0][...] = jnp.zeros(mix_scr[0].shape, BF16)

    @pl.when((s == 0) | (lax.rem(s + tiles_per_seq - 1, tiles_per_seq) == 0))
    def _():
        s_scr[...] = jnp.zeros(s_scr.shape, F32)
        if even:
            h_scr[...] = jnp.zeros(h_scr.shape, F32)
            tail_scr[...] = jnp.zeros(tail_scr.shape, F32)

    main_cols = EVEN_IN if even else ODD_MAIN
    macs_per_row = D_MODEL * (main_cols + D_MODEL + 2 * D_FF)

    def matrix_stream(proj_a, mix_c, pace):
        P = MXU_PIECE
        h_in = _rmsnorm(xlead_ref[...], gmix_ref[...]).astype(BF16)
        m = mix_c[...]
        for n0 in range(0, D_MODEL, P):
            x1 = xlag_ref[:, n0:n0 + P] + _dot(m, wo_ref[:, n0:n0 + P])
            x1_scr[:, n0:n0 + P] = x1
            pace.after(x1)
            yield D_MODEL * P / macs_per_row
        h_mlp = _rmsnorm(x1_scr[...], gmlp_ref[...]).astype(BF16)
        if not even:
            glr = _dot(h_in, wg1_ref[...]).astype(BF16)
        for n0 in range(0, main_cols, P):
            p = _dot(h_in, win_ref[:, n0:n0 + P])
            proj_a[:, n0:n0 + P] = p
            pace.after(p)
            if not even and n0 == P:
                z = _dot(glr, wg2_ref[...]) + bg_ref[...]
            yield D_MODEL * P / macs_per_row
        if not even:
            proj_a[:, ODD_MAIN:] = _scaled_log_sigmoid(z)
            yield 0.0
        for n0 in range(0, D_FF, P):
            u = jnp.square(jnp.maximum(_dot(h_mlp, wu_ref[:, n0:n0 + P]), 0.0))
            u_scr[:, n0:n0 + P] = u.astype(BF16)
            pace.after(u)
            yield D_MODEL * P / macs_per_row
        for n0 in range(0, D_MODEL, P):
            x2 = x1_scr[:, n0:n0 + P] + _dot(u_scr[...], wd_ref[:, n0:n0 + P])
            if final:
                x1_scr[:, n0:n0 + P] = x2
            else:
                out_ref[:, n0:n0 + P] = x2
            pace.after(x2)
            yield D_FF * P / macs_per_row
        if final:
            out_ref[...] = _rmsnorm(x1_scr[...], gf_ref[...])

    def step(par):
        proj_a, proj_b = proj_scr[par], proj_scr[1 - par]
        mix_b, mix_c = mix_scr[1 - par], mix_scr[par]

        def load(r, c):
            return proj_b[r, c]

        def store(r, c, val):
            mix_b[r, c] = val.astype(BF16)

        pace = _Pace()
        if even:
            vector_stream = _even_mixer_tile(chunk, rows // chunk, tile_in_seq * rows, load,
                                             store, cos_ref, sin_ref, s_scr, s_scr, h_scr, h_scr,
                                             tail_scr, mixer_weights, pace)
        else:
            vector_stream = _gla_mixer_tile(chunk, rows // chunk, load, store, s_scr, s_scr,
                                            nw_ref[...], pace)
        _interleave(matrix_stream(proj_a, mix_c, pace), vector_stream)

    parity = lax.rem(s, 2)
    pl.when(parity == 0)(functools.partial(step, 0))
    pl.when(parity == 1)(functools.partial(step, 1))

    if not has_prev:
        zeros_ref[...] = jnp.zeros(zeros_ref.shape, F32)

    @pl.when((s >= 1) & (s <= n_tiles) & (lax.rem(s, tiles_per_seq) == 0))
    def _():
        if has_prev:
            snew_ref[0] = s_scr[...]
        else:
            for other in range(snew_ref.shape[0]):
                snew_ref[other, 0] = s_scr[...] if other == slot else jnp.zeros(s_scr.shape, F32)
        if even:
            hnew_ref[0] = h_scr[...]
            cnew_ref[0] = tail_scr[SUBLANES - (CONV_W - 1):SUBLANES, :]


def _fused_layer(x, w, layer, cos, sin, prev_state, seq_len, n_short):
    n_rows = x.shape[0]
    B = n_rows // seq_len
    R = FUSED_ROWS
    even = layer % 2 == 0
    final = layer == DEPTH - 1
    i = layer // 2
    C = math.gcd(seq_len, RET_CHUNK if even else GLA_CHUNK)
    tps = seq_len // R
    NT = n_rows // R
    has_prev = prev_state is not None

    def lead(s):
        return (jnp.minimum(s, NT - 1), 0)

    def mid_tile(s):
        return jnp.clip(s - 1, 0, NT - 1)

    def lag(s):
        return (jnp.clip(s - 2, 0, NT - 1), 0)

    def seq3(s):
        return (mid_tile(s) // tps, 0, 0)

    in_specs = [pl.BlockSpec((R, D_MODEL), lead),
                pl.BlockSpec((R, D_MODEL), lag),
                _layer_spec((1, D_MODEL), layer)]
    args = [x, x, w["norm_mix"]]
    if even:
        in_specs += [_layer_spec((D_MODEL, EVEN_IN), i),
                     pl.BlockSpec((R, RET_KEY), lambda s: (mid_tile(s) % tps, 0)),
                     pl.BlockSpec((R, RET_KEY), lambda s: (mid_tile(s) % tps, 0)),
                     _layer_spec((CONV_W, LRU_WIDTH), i),
                     _layer_spec((1, LRU_WIDTH), i),
                     _layer_spec((LRU_WIDTH, LRU_WIDTH), i),
                     _layer_spec((1, LRU_WIDTH), i),
                     _layer_spec((LRU_WIDTH, LRU_WIDTH), i),
                     _layer_spec((1, LRU_WIDTH), i),
                     _layer_spec((1, LRU_WIDTH), i)]
        args += [w["w_in_even"], cos, sin, w["conv_w"], w["conv_b"], w["wa"], w["ba"],
                 w["wi"], w["bi"], w["lam"]]
        w_out = w["w_out_even"]
        state_shape = (RET_HEADS, RET_DK, RET_DV)
        n_stack = w["w_in_even"].shape[0]
        proj_cols = EVEN_IN
    else:
        in_specs += [_layer_spec((D_MODEL, ODD_MAIN), i),
                     _layer_spec((D_MODEL, LANES), i),
                     _layer_spec((LANES, GLA_KEY), i),
                     _layer_spec((1, GLA_KEY), i),
                     _layer_spec((1, GLA_DV), i)]
        args += [w["w_in_odd"], w["wg1"], w["wg2"], w["bg"], w["gla_norm"]]
        w_out = w["w_out_odd"]
        state_shape = (GLA_HEADS, GLA_DK, GLA_DV)
        n_stack = w["w_in_odd"].shape[0]
        proj_cols = ODD_OUT
    in_specs += [_layer_spec((D_MODEL, D_MODEL), i),
                 _layer_spec((1, D_MODEL), layer),
                 _layer_spec((D_MODEL, D_FF), layer),
                 _layer_spec((D_FF, D_MODEL), layer),
                 _const_spec((1, D_MODEL))]
    args += [w_out, w["norm_mlp"], w["w_up"], w["w_down"], w["norm_final"]]
    aliases = {}
    if has_prev:
        in_specs.append(pl.BlockSpec(memory_space=pl.ANY))
        args.append(prev_state)
        aliases = {len(args) - 1: 1}

    if has_prev:
        state_spec = pl.BlockSpec((None, 1) + state_shape,
                                  lambda s: (i, mid_tile(s) // tps, 0, 0, 0))
    else:
        state_spec = pl.BlockSpec((n_stack, 1) + state_shape,
                                  lambda s: (0, mid_tile(s) // tps, 0, 0, 0))
    out_specs = [pl.BlockSpec((R, D_MODEL), lag), state_spec]
    out_shapes = [jax.ShapeDtypeStruct((n_rows, D_MODEL), F32),
                  jax.ShapeDtypeStruct((n_stack, B) + state_shape, F32)]
    scratch = [pltpu.VMEM((R, proj_cols), F32), pltpu.VMEM((R, proj_cols), F32),
               pltpu.VMEM((R, D_MODEL), BF16), pltpu.VMEM((R, D_MODEL), BF16),
               pltpu.VMEM((R, D_MODEL), F32), pltpu.VMEM((R, D_FF), BF16),
               pltpu.VMEM(state_shape, F32)]
    if even:
        out_specs += [pl.BlockSpec((1, 1, LRU_WIDTH), seq3),
                      pl.BlockSpec((1, CONV_W - 1, LRU_WIDTH), seq3)]
        out_shapes += [jax.ShapeDtypeStruct((B, 1, LRU_WIDTH), F32),
                       jax.ShapeDtypeStruct((B, CONV_W - 1, LRU_WIDTH), F32)]
        scratch += [pltpu.VMEM((1, LRU_WIDTH), F32),
                    pltpu.VMEM((SUBLANES, LRU_WIDTH), F32)]
    if not has_prev:
        per_step = n_short // NT
        assert per_step * NT == n_short
        out_specs.append(pl.BlockSpec((n_stack, per_step) + state_shape,
                                      lambda s: (0, jnp.minimum(s, NT - 1), 0, 0, 0)))
        out_shapes.append(jax.ShapeDtypeStruct((n_stack, n_short) + state_shape, F32))
    return pl.pallas_call(
        functools.partial(_fused_layer_kernel, even, final, C, R, tps, NT, i, has_prev),
        grid=(NT + 2,),
        in_specs=in_specs,
        out_specs=tuple(out_specs),
        out_shape=tuple(out_shapes),
        scratch_shapes=scratch,
        input_output_aliases=aliases,
        compiler_params=_params("arbitrary"),
        name="layer_even" if even else "layer_odd",
    )(*args)


def _rope_tables(pos0, T):
    half = RET_DK // 2
    inv = ROPE_BASE ** (-jnp.arange(half, dtype=F32) / half)
    ang = (pos0 + jnp.arange(T, dtype=jnp.int32)).astype(F32)[:, None] * inv[None, :]
    cos = jnp.cos(ang)
    sin = jnp.sin(ang)
    cos_full = jnp.tile(jnp.concatenate([cos, cos], axis=-1), (1, RET_HEADS))
    sin_signed = jnp.tile(jnp.concatenate([-sin, sin], axis=-1), (1, RET_HEADS))
    return cos_full, sin_signed


def _block_diag(w):
    nl, nb, c, d = w.shape
    eye = jnp.eye(nb, dtype=w.dtype)
    return (eye[None, :, None, :, None] * w[:, :, :, None, :]).reshape(nl, nb * c, nb * d)


def _prepare_weights(norm_mix, norm_mlp, norm_final, w_in_even, w_out_even, conv_w, conv_b,
                     lru_w_a, lru_b_a, lru_w_i, lru_b_i, lru_lambda,
                     w_in_odd, gla_w_gate2, gla_b_gate, gla_norm, w_out_odd, w_up, w_down):
    n_even = w_in_even.shape[0]
    n_odd = w_in_odd.shape[0]
    pad1 = LANES - GLA_RANK
    return dict(
        norm_mix=norm_mix.reshape(DEPTH, 1, D_MODEL),
        norm_mlp=norm_mlp.reshape(DEPTH, 1, D_MODEL),
        norm_final=norm_final.reshape(1, D_MODEL),
        w_in_even=w_in_even.astype(BF16),
        w_out_even=w_out_even.astype(BF16),
        conv_w=conv_w,
        conv_b=conv_b.reshape(n_even, 1, LRU_WIDTH),
        wa=_block_diag(lru_w_a).astype(BF16),
        ba=lru_b_a.reshape(n_even, 1, LRU_WIDTH),
        wi=_block_diag(lru_w_i).astype(BF16),
        bi=lru_b_i.reshape(n_even, 1, LRU_WIDTH),
        lam=lru_lambda.reshape(n_even, 1, LRU_WIDTH),
        w_in_odd=w_in_odd.astype(BF16),
        wg1=jnp.pad(w_in_odd[:, :, ODD_MAIN:], ((0, 0), (0, 0), (0, pad1))).astype(BF16),
        wg2=jnp.pad(gla_w_gate2, ((0, 0), (0, pad1), (0, 0))).astype(BF16),
        bg=gla_b_gate.reshape(n_odd, 1, GLA_KEY),
        gla_norm=gla_norm.reshape(n_odd, 1, GLA_DV),
        w_out_odd=w_out_odd.astype(BF16),
        w_up=w_up.astype(BF16),
        w_down=w_down.astype(BF16),
    )


def _prompt_trunk(x, w, n_short):
    B, T, _ = x.shape
    xf = x.reshape(B * T, D_MODEL)
    cos, sin = _rope_tables(0, T)
    ret_new, gla_new = None, None
    lrus, convs, zeros = [], [], []
    for layer in range(DEPTH):
        prev = ret_new if layer % 2 == 0 else gla_new
        outs = _fused_layer(xf, w, layer, cos, sin, prev, T, n_short)
        if prev is None:
            zeros.append(outs[-1])
            outs = outs[:-1]
        if layer % 2 == 0:
            xf, ret_new, lh, cb = outs
            lrus.append(lh.reshape(B, LRU_WIDTH))
            convs.append(cb)
        else:
            xf, gla_new = outs
    states = (ret_new, jnp.stack(lrus), jnp.stack(convs), gla_new)
    return xf.reshape(B, T, D_MODEL), states, zeros


def _sample_trunk(x, pos0, ret_s, lru_h, conv_buf, gla_s, ret_new, gla_new, w):
    B, T, _ = x.shape
    xf = x.reshape(B * T, D_MODEL)
    cos, sin = _rope_tables(pos0, T)
    lru_h = lru_h.reshape(lru_h.shape[0], B, 1, LRU_WIDTH)
    lrus, convs = [], []
    for layer in range(DEPTH):
        if layer % 2 == 0:
            proj = _norm_proj_even(xf, w, layer)
            mix, ret_new, lh, cb = _even_mixer(proj.reshape(B, T, EVEN_IN), pos0, cos, sin,
                                               ret_s, lru_h, conv_buf, w, layer, ret_new)
            lrus.append(lh.reshape(B, LRU_WIDTH))
            convs.append(cb)
        else:
            proj = _norm_proj_odd(xf, w, layer)
            mix, gla_new = _gla_mixer(proj.reshape(B, T, ODD_OUT), gla_s, w, layer, gla_new)
        xf = _out_mlp(xf, mix.reshape(B * T, D_MODEL), w, layer)
    return xf.reshape(B, T, D_MODEL), ret_new, jnp.stack(lrus), jnp.stack(convs), gla_new


def kernel(x_prompt, x_sample, state_ret, state_lru, state_conv, state_gla, norm_mix, norm_mlp, norm_final, w_in_even, w_out_even, conv_w, conv_b, lru_w_a, lru_b_a, lru_w_i, lru_b_i, lru_lambda, w_in_odd, gla_w_gate2, gla_b_gate, gla_norm, w_out_odd, w_up, w_down):
    w = _prepare_weights(norm_mix, norm_mlp, norm_final, w_in_even, w_out_even, conv_w, conv_b,
                         lru_w_a, lru_b_a, lru_w_i, lru_b_i, lru_lambda,
                         w_in_odd, gla_w_gate2, gla_b_gate, gla_norm, w_out_odd, w_up, w_down)
    y_p, (ret_p, lru_p, conv_p, gla_p), (ret_zero, gla_zero) = _prompt_trunk(
        x_prompt, w, x_sample.shape[0])
    y_s, ret_s, lru_s, conv_s, gla_s = _sample_trunk(x_sample, PAST_LEN, state_ret, state_lru,
                                                     state_conv, state_gla, ret_zero, gla_zero, w)
    return (y_p, y_s, ret_p, ret_s, lru_p, lru_s, conv_p, conv_s, gla_p, gla_s)
```

```python
import functools
import math

import jax
import jax.numpy as jnp
from jax import lax
from jax.experimental import pallas as pl
from jax.experimental.pallas import tpu as pltpu

F32 = jnp.float32
BF16 = jnp.bfloat16

D_MODEL = 1024
DEPTH = 4
PAST_LEN = 16384
EPS = 1e-6
ROPE_BASE = 10000.0

RET_HEADS = 4
RET_DK = 64
RET_DV = 128
RET_KEY = RET_HEADS * RET_DK
RET_VALUE = RET_HEADS * RET_DV
RET_CHUNK = 128

LRU_WIDTH = 512
CONV_W = 4
LRU_C = 8.0

GLA_HEADS = 4
GLA_KEY = 512
GLA_VALUE = 1024
GLA_DK = GLA_KEY // GLA_HEADS
GLA_DV = GLA_VALUE // GLA_HEADS
GLA_RANK = 16
GLA_TAU = 16.0
GLA_CHUNK = 64

D_FF = 4 * D_MODEL
FF_CHUNK = 1024

EVEN_IN = 2 * RET_KEY + 2 * RET_VALUE + 2 * LRU_WIDTH
ODD_MAIN = 2 * GLA_KEY + 2 * GLA_VALUE
ODD_OUT = ODD_MAIN + GLA_KEY

LANES = 128
SUBLANES = 8
ROW_TILE = 512
FUSED_ROWS = 256
MXU_PIECE = 256
SAMPLE_SEQS = 8
VMEM_LIMIT = 56 * 1024 * 1024


def _dot(a, b):
    return jnp.dot(a.astype(BF16), b.astype(BF16), preferred_element_type=F32)


def _dot_nt(a, b):
    return lax.dot_general(a.astype(BF16), b.astype(BF16), (((1,), (1,)), ((), ())),
                           preferred_element_type=F32)


def _dot_tn(a, b):
    return lax.dot_general(a.astype(BF16), b.astype(BF16), (((0,), (0,)), ((), ())),
                           preferred_element_type=F32)


def _rmsnorm(x, g):
    return x * lax.rsqrt(jnp.mean(x * x, axis=-1, keepdims=True) + EPS) * g


def _head_rms(x, eps=EPS):
    return x * lax.rsqrt(jnp.mean(x * x, axis=-1, keepdims=True) + eps)


def _shift_rows(x, d, fill):
    row = lax.broadcasted_iota(jnp.int32, x.shape, 0)
    return jnp.where(row >= d, pltpu.roll(x, d, axis=0), fill)


def _layer_spec(shape, layer):
    n = len(shape)
    return pl.BlockSpec((None,) + tuple(shape), lambda *_: (layer,) + (0,) * n,
                        pipeline_mode=pl.Buffered(1))


def _const_spec(shape):
    n = len(shape)
    return pl.BlockSpec(shape, lambda *_: (0,) * n, pipeline_mode=pl.Buffered(1))


def _params(*semantics):
    return pltpu.CompilerParams(dimension_semantics=semantics, vmem_limit_bytes=VMEM_LIMIT)


def _scaled_log_sigmoid(z):
    return (jnp.minimum(z, 0.0) - jnp.log(1.0 + jnp.exp(-jnp.abs(z)))) / GLA_TAU


def _log_gate(h, wg1_ref, wg2_ref, bg_ref):
    glr = _dot(h, wg1_ref[...])
    return _scaled_log_sigmoid(_dot(glr, wg2_ref[...]) + bg_ref[...])


def _norm_proj_even_kernel(x_ref, g_ref, w_ref, o_ref):
    h = _rmsnorm(x_ref[...], g_ref[...])
    o_ref[...] = _dot(h, w_ref[...])


def _norm_proj_odd_kernel(x_ref, g_ref, w_ref, wg1_ref, wg2_ref, bg_ref, o_ref):
    h = _rmsnorm(x_ref[...], g_ref[...]).astype(BF16)
    o_ref[:, :ODD_MAIN] = _dot(h, w_ref[...])
    o_ref[:, ODD_MAIN:] = _log_gate(h, wg1_ref, wg2_ref, bg_ref)


def _out_mlp_kernel(final, x_ref, m_ref, wo_ref, g_ref, wu_ref, wd_ref, gf_ref, o_ref,
                    h_scr, acc_scr):
    c = pl.program_id(0)

    @pl.when(c == 0)
    def _():
        x1 = x_ref[...] + _dot(m_ref[...], wo_ref[...])
        acc_scr[...] = x1
        h_scr[...] = _rmsnorm(x1, g_ref[...]).astype(BF16)

    u = jnp.square(jnp.maximum(_dot(h_scr[...], wu_ref[...]), 0.0))
    acc_scr[...] += _dot(u, wd_ref[...])

    @pl.when(c == pl.num_programs(0) - 1)
    def _():
        x2 = acc_scr[...]
        o_ref[...] = _rmsnorm(x2, gf_ref[...]) if final else x2


def _norm_proj_even(x, w, layer):
    n = x.shape[0]
    tm = min(ROW_TILE, n)
    return pl.pallas_call(
        _norm_proj_even_kernel,
        grid=(n // tm,),
        in_specs=[pl.BlockSpec((tm, D_MODEL), lambda i: (i, 0)),
                  _layer_spec((1, D_MODEL), layer),
                  _layer_spec((D_MODEL, EVEN_IN), layer // 2)],
        out_specs=pl.BlockSpec((tm, EVEN_IN), lambda i: (i, 0)),
        out_shape=jax.ShapeDtypeStruct((n, EVEN_IN), F32),
        compiler_params=_params("parallel"),
        name="norm_proj_even",
    )(x, w["norm_mix"], w["w_in_even"])


def _norm_proj_odd(x, w, layer):
    n = x.shape[0]
    tm = min(ROW_TILE, n)
    i = layer // 2
    return pl.pallas_call(
        _norm_proj_odd_kernel,
        grid=(n // tm,),
        in_specs=[pl.BlockSpec((tm, D_MODEL), lambda i: (i, 0)),
                  _layer_spec((1, D_MODEL), layer),
                  _layer_spec((D_MODEL, ODD_MAIN), i),
                  _layer_spec((D_MODEL, LANES), i),
                  _layer_spec((LANES, GLA_KEY), i),
                  _layer_spec((1, GLA_KEY), i)],
        out_specs=pl.BlockSpec((tm, ODD_OUT), lambda i: (i, 0)),
        out_shape=jax.ShapeDtypeStruct((n, ODD_OUT), F32),
        compiler_params=_params("parallel"),
        name="norm_proj_odd",
    )(x, w["norm_mix"], w["w_in_odd"], w["wg1"], w["wg2"], w["bg"])


def _out_mlp(x, mix, w, layer):
    n = x.shape[0]
    final = layer == DEPTH - 1
    w_out = w["w_out_even"] if layer % 2 == 0 else w["w_out_odd"]
    return pl.pallas_call(
        functools.partial(_out_mlp_kernel, final),
        grid=(D_FF // FF_CHUNK,),
        in_specs=[_const_spec((n, D_MODEL)),
                  _const_spec((n, D_MODEL)),
                  _layer_spec((D_MODEL, D_MODEL), layer // 2),
                  _layer_spec((1, D_MODEL), layer),
                  pl.BlockSpec((None, D_MODEL, FF_CHUNK), lambda c: (layer, 0, c)),
                  pl.BlockSpec((None, FF_CHUNK, D_MODEL), lambda c: (layer, c, 0)),
                  _const_spec((1, D_MODEL))],
        out_specs=pl.BlockSpec((n, D_MODEL), lambda c: (0, 0)),
        out_shape=jax.ShapeDtypeStruct((n, D_MODEL), F32),
        scratch_shapes=[pltpu.VMEM((n, D_MODEL), BF16), pltpu.VMEM((n, D_MODEL), F32)],
        compiler_params=_params("arbitrary"),
        name="out_mlp_final" if final else "out_mlp",
    )(x, mix, w_out, w["norm_mlp"], w["w_up"], w["w_down"], w["norm_final"])


def _run(stream):
    for _ in stream:
        pass


def _interleave(*streams):
    done = [0.0] * len(streams)
    alive = list(range(len(streams)))
    while alive:
        i = min(alive, key=lambda j: done[j])
        try:
            done[i] += next(streams[i])
        except StopIteration:
            alive.remove(i)


class _Pace:
    def __init__(self):
        self.zero = None

    def after(self, value):
        t = value[0:1, 0:1]
        self.zero = jnp.where(t == t, 0.0, t)

    def __call__(self, x):
        return x if self.zero is None else x + self.zero


def _no_pace(x):
    return x


def _even_mixer_tile(C, chunks, pos_base, load, store, cos_ref, sin_ref,
                     s_src, s_dst, h_src, h_dst, tail_ref, weights, pace=_no_pace):
    cw_ref, cbias_ref, wa_ref, ba_ref, wi_ref, bi_ref, lam_ref = weights
    lane = lax.broadcasted_iota(jnp.int32, (C, RET_KEY), 1)
    first_half = (lane % RET_DK) < (RET_DK // 2)
    ri = lax.broadcasted_iota(jnp.int32, (C, C), 0)
    ci = lax.broadcasted_iota(jnp.int32, (C, C), 1)
    rel = (ri - ci).astype(F32)
    row = lax.broadcasted_iota(jnp.int32, (C, 1), 0).astype(F32)
    trow = lax.broadcasted_iota(jnp.int32, (C, LRU_WIDTH), 0)
    row8 = lax.broadcasted_iota(jnp.int32, (SUBLANES, LRU_WIDTH), 0)
    lam = lam_ref[...]
    softplus_neg_lam = jnp.maximum(-lam, 0.0) + jnp.log(1.0 + jnp.exp(-jnp.abs(lam)))
    log_decay = [math.log1p(-(2.0 ** (-5.0 - h))) for h in range(RET_HEADS)]
    decay_mask = [jnp.where(rel >= 0, jnp.exp(jnp.maximum(rel, 0.0) * lg), 0.0)
                  for lg in log_decay]
    q_decay = [jnp.exp((row + 1.0) * lg) for lg in log_decay]
    k_decay = [jnp.exp((C - 1.0 - row) * lg) for lg in log_decay]
    n_scan = max(1, int(math.log2(C)))

    v0 = 2 * RET_KEY
    g0 = v0 + RET_VALUE
    x0 = g0 + RET_VALUE
    xg0 = x0 + LRU_WIDTH
    heads = range(RET_HEADS)
    J = range(chunks)
    rows = [slice(j * C, (j + 1) * C) for j in J]

    def delayed(x, t, n):
        rolled = pltpu.roll(x, n, axis=0)
        head = jnp.where(row8 < n, pltpu.roll(t, n, axis=0), rolled[0:SUBLANES, :])
        if C == SUBLANES:
            return head
        return jnp.concatenate([head, rolled[SUBLANES:, :]], axis=0)

    qh, kh, q_dec, k_dec, vh, xc = ([None] * chunks for _ in range(6))
    tail = tail_ref[...]
    for j in J:
        cos = cos_ref[rows[j], :]
        sin = sin_ref[rows[j], :]

        def rope(x):
            partner = jnp.where(first_half,
                                pltpu.roll(x, RET_KEY - RET_DK // 2, axis=1),
                                pltpu.roll(x, RET_DK // 2, axis=1))
            return x * cos + partner * sin

        q = rope(load(rows[j], slice(0, RET_KEY)))
        k = rope(load(rows[j], slice(RET_KEY, 2 * RET_KEY))) * (RET_DK ** -0.5)
        q = [q[:, h * RET_DK:(h + 1) * RET_DK] for h in heads]
        k = [k[:, h * RET_DK:(h + 1) * RET_DK] for h in heads]
        q_dec[j] = [(q[h] * q_decay[h]).astype(BF16) for h in heads]
        k_dec[j] = [(k[h] * k_decay[h]).astype(BF16) for h in heads]
        qh[j] = [q[h].astype(BF16) for h in heads]
        kh[j] = [k[h].astype(BF16) for h in heads]
        vh[j] = [load(rows[j], slice(v0 + h * RET_DV, v0 + (h + 1) * RET_DV)).astype(BF16)
                 for h in heads]
        yield 0.08 / chunks

        x_new = load(rows[j], slice(x0, x0 + LRU_WIDTH))
        acc = pace(cbias_ref[...])
        for i in range(CONV_W - 1):
            acc = acc + delayed(x_new, tail, CONV_W - 1 - i) * cw_ref[i:i + 1, :]
        xc[j] = acc + x_new * cw_ref[CONV_W - 1:CONV_W, :]
        tail = x_new[C - SUBLANES:C, :]
        yield 0.10 / chunks
    tail_ref[...] = tail

    scores = [[_dot_nt(qh[j][h], kh[j][h]) for h in heads] for j in J]
    kv = [[_dot_tn(k_dec[j][h], vh[j][h]) for h in heads] for j in J]
    xc_all = xc[0] if chunks == 1 else jnp.concatenate(xc, axis=0)
    xc_all = xc_all.astype(BF16)
    diag = [slice(c, c + MXU_PIECE) for c in range(0, LRU_WIDTH, MXU_PIECE)]
    r_all = jnp.concatenate([_dot(xc_all[:, d], wa_ref[d, d]) for d in diag], axis=1)
    i_all = jnp.concatenate([_dot(xc_all[:, d], wi_ref[d, d]) for d in diag], axis=1)
    r_pre = [r_all[rows[j], :] for j in J]
    i_pre = [i_all[rows[j], :] for j in J]
    yield 0.0

    probs, s_in = [None] * chunks, [None] * chunks
    s_cur = [s_src[h] for h in heads]
    for j in J:
        probs[j] = [(scores[j][h] * decay_mask[h]).astype(BF16) for h in heads]
        s_in[j] = [s_cur[h].astype(BF16) for h in heads]
        s_cur = [s_cur[h] * math.exp(C * log_decay[h]) + kv[j][h] for h in heads]
        yield 0.06 / chunks
    for h in heads:
        s_dst[h] = s_cur[h]

    a_cum, b_loc = [None] * chunks, [None] * chunks
    for j in J:
        r = jax.nn.sigmoid(r_pre[j] + pace(ba_ref[...]))
        gate_i = jax.nn.sigmoid(i_pre[j] + bi_ref[...])
        log_a = -LRU_C * r * softplus_neg_lam
        a = jnp.exp(log_a)
        y = -jnp.tanh(log_a) * (a * a + 1.0)
        mult = jnp.where(y > 0.0, y * lax.rsqrt(y), 0.0)
        pos = trow + (pos_base + j * C)
        mult = jnp.where(pos == 0, 1.0, mult)
        b = mult * (gate_i * xc[j])
        yield 0.20 / chunks
        d = 1
        while d < C:
            b = a * _shift_rows(b, d, pace(0.0)) + b
            a = a * _shift_rows(a, d, pace(1.0))
            d *= 2
            yield 0.36 / (chunks * n_scan)
        a_cum[j], b_loc[j] = a, b

    out = [[_dot(probs[j][h], vh[j][h]) + _dot(q_dec[j][h], s_in[j][h]) for h in heads]
           for j in J]
    yield 0.0

    carry = h_src[...]
    for j in J:
        hidden = b_loc[j] + a_cum[j] * carry
        carry = hidden[C - 1:C, :]
        xg = load(rows[j], slice(xg0, xg0 + LRU_WIDTH))
        store(rows[j], slice(RET_VALUE, RET_VALUE + LRU_WIDTH), hidden * jax.nn.gelu(xg))
        yield 0.08 / chunks
        for h in heads:
            gh = load(rows[j], slice(g0 + h * RET_DV, g0 + (h + 1) * RET_DV))
            store(rows[j], slice(h * RET_DV, (h + 1) * RET_DV),
                  jax.nn.silu(gh.astype(BF16)) * _head_rms(out[j][h], pace(EPS)).astype(BF16))
            yield 0.03 / chunks
    h_dst[...] = carry


def _gla_mixer_tile(C, chunks, load, store, s_src, s_dst, nw, pace=_no_pace):
    k0 = GLA_KEY
    v0 = 2 * GLA_KEY
    r0 = v0 + GLA_VALUE
    la0 = r0 + GLA_VALUE
    ri = lax.broadcasted_iota(jnp.int32, (C, C), 0)
    ci = lax.broadcasted_iota(jnp.int32, (C, C), 1)
    causal = ri >= ci
    di = lax.broadcasted_iota(jnp.int32, (GLA_DK, GLA_DK), 0)
    dj = lax.broadcasted_iota(jnp.int32, (GLA_DK, GLA_DK), 1)
    eye = di == dj
    heads = range(GLA_HEADS)
    key = [slice(h * GLA_DK, (h + 1) * GLA_DK) for h in heads]
    J = range(chunks)
    rows = [slice(j * C, (j + 1) * C) for j in J]

    q_in, k_in, q_st, k_st, decay_col, vh = ([None] * chunks for _ in range(6))
    for j in J:
        b = load(rows[j], slice(la0, la0 + GLA_KEY))
        d = 1
        while d < C:
            b = b + _shift_rows(b, d, pace(0.0))
            d *= 2
        yield 0.12 / chunks
        b_mid = pace(b[C // 2:C // 2 + 1, :])
        b_last = b[C - 1:C, :]
        q = load(rows[j], slice(0, GLA_KEY)) * (GLA_DK ** -0.5)
        k = load(rows[j], slice(k0, k0 + GLA_KEY))
        q_in[j] = (q * jnp.exp(b - b_mid)).astype(BF16)
        k_in[j] = (k * jnp.exp(b_mid - b)).astype(BF16)
        q_st[j] = (q * jnp.exp(b)).astype(BF16)
        k_st[j] = (k * jnp.exp(b_last - b)).astype(BF16)
        s_decay = jnp.exp(b_last)
        decay_col[j] = [jnp.sum(jnp.where(eye, s_decay[:, key[h]], 0.0), axis=1, keepdims=True)
                        for h in heads]
        vh[j] = [load(rows[j], slice(v0 + h * GLA_DV, v0 + (h + 1) * GLA_DV)).astype(BF16)
                 for h in heads]
        yield 0.28 / chunks

    scores = [[_dot_nt(q_in[j][:, key[h]], k_in[j][:, key[h]]) for h in heads] for j in J]
    kv = [[_dot_tn(k_st[j][:, key[h]], vh[j][h]) for h in heads] for j in J]
    yield 0.0

    probs, s_in = [None] * chunks, [None] * chunks
    s_cur = [s_src[h] for h in heads]
    for j in J:
        probs[j] = [jnp.where(causal, scores[j][h], 0.0).astype(BF16) for h in heads]
        s_in[j] = [s_cur[h].astype(BF16) for h in heads]
        s_cur = [s_cur[h] * decay_col[j][h] + kv[j][h] for h in heads]
        yield 0.16 / chunks
    for h in heads:
        s_dst[h] = s_cur[h]

    out = [[_dot(probs[j][h], vh[j][h]) + _dot(q_st[j][:, key[h]], s_in[j][h]) for h in heads]
           for j in J]
    yield 0.0

    for j in J:
        for h in heads:
            rh = load(rows[j], slice(r0 + h * GLA_DV, r0 + (h + 1) * GLA_DV))
            store(rows[j], slice(h * GLA_DV, (h + 1) * GLA_DV),
                  jax.nn.silu(rh.astype(BF16)) * (_head_rms(out[j][h], pace(EPS)) * nw).astype(BF16))
            yield 0.11 / chunks


def _even_mixer_kernel(pos0, chunk, seqs, *refs):
    (proj_ref, cos_ref, sin_ref, s0_ref, h0_ref, cb_ref,
     cw_ref, cbias_ref, wa_ref, ba_ref, wi_ref, bi_ref, lam_ref) = refs[:13]
    refs = refs[14:]
    mix_ref, snew_ref, hnew_ref, cnew_ref, tail_scr = refs
    weights = (cw_ref, cbias_ref, wa_ref, ba_ref, wi_ref, bi_ref, lam_ref)
    for g in range(seqs):
        tail = tail_scr.at[g]
        tail[...] = jnp.zeros((SUBLANES, LRU_WIDTH), F32)
        tail[SUBLANES - (CONV_W - 1):SUBLANES, :] = cb_ref[g]

        def load(rows, cols, g=g):
            return proj_ref[g, rows, cols]

        def store(rows, cols, val, g=g):
            mix_ref[g, rows, cols] = val.astype(F32)

        _run(_even_mixer_tile(chunk, 1, pos0, load, store, cos_ref, sin_ref,
                              s0_ref.at[g], snew_ref.at[g], h0_ref.at[g], hnew_ref.at[g],
                              tail, weights))
        cnew_ref[g] = tail[SUBLANES - (CONV_W - 1):SUBLANES, :]


def _even_mixer(proj, pos0, cos, sin, ret_s, lru_h, conv_buf, w, layer, prev_ret):
    B, T, _ = proj.shape
    i = layer // 2
    n_even = w["w_in_even"].shape[0]
    G = SAMPLE_SEQS
    ret_block = (None, G, RET_HEADS, RET_DK, RET_DV)
    in_specs = [pl.BlockSpec((G, T, EVEN_IN), lambda b: (b, 0, 0)),
                _const_spec((T, RET_KEY)),
                _const_spec((T, RET_KEY)),
                pl.BlockSpec(ret_block, lambda b: (i, b, 0, 0, 0)),
                pl.BlockSpec((None, G, 1, LRU_WIDTH), lambda b: (i, b, 0, 0)),
                pl.BlockSpec((None, G, CONV_W - 1, LRU_WIDTH), lambda b: (i, b, 0, 0)),
                _layer_spec((CONV_W, LRU_WIDTH), i),
                _layer_spec((1, LRU_WIDTH), i),
                _layer_spec((LRU_WIDTH, LRU_WIDTH), i),
                _layer_spec((1, LRU_WIDTH), i),
                _layer_spec((LRU_WIDTH, LRU_WIDTH), i),
                _layer_spec((1, LRU_WIDTH), i),
                _layer_spec((1, LRU_WIDTH), i)]
    args = [proj, cos, sin, ret_s, lru_h, conv_buf, w["conv_w"], w["conv_b"],
            w["wa"], w["ba"], w["wi"], w["bi"], w["lam"]]
    in_specs.append(pl.BlockSpec(memory_space=pl.ANY))
    args.append(prev_ret)
    aliases = {len(args) - 1: 1}
    out_shapes = (jax.ShapeDtypeStruct((B, T, D_MODEL), F32),
                  jax.ShapeDtypeStruct((n_even, B, RET_HEADS, RET_DK, RET_DV), F32),
                  jax.ShapeDtypeStruct((B, 1, LRU_WIDTH), F32),
                  jax.ShapeDtypeStruct((B, CONV_W - 1, LRU_WIDTH), F32))
    return pl.pallas_call(
        functools.partial(_even_mixer_kernel, pos0, T, G),
        grid=(B // G,),
        in_specs=in_specs,
        out_specs=(pl.BlockSpec((G, T, D_MODEL), lambda b: (b, 0, 0)),
                   pl.BlockSpec(ret_block, lambda b: (i, b, 0, 0, 0)),
                   pl.BlockSpec((G, 1, LRU_WIDTH), lambda b: (b, 0, 0)),
                   pl.BlockSpec((G, CONV_W - 1, LRU_WIDTH), lambda b: (b, 0, 0))),
        out_shape=out_shapes,
        scratch_shapes=[pltpu.VMEM((G, SUBLANES, LRU_WIDTH), F32)],
        input_output_aliases=aliases,
        compiler_params=_params("parallel"),
        name="even_mixer",
    )(*args)


def _gla_mixer_kernel(chunk, seqs, *refs):
    proj_ref, s0_ref, nw_ref = refs[:3]
    refs = refs[4:]
    mix_ref, snew_ref = refs
    nw = nw_ref[...]
    for g in range(seqs):
        def load(rows, cols, g=g):
            return proj_ref[g, rows, cols]

        def store(rows, cols, val, g=g):
            mix_ref[g, rows, cols] = val.astype(F32)

        _run(_gla_mixer_tile(chunk, 1, load, store, s0_ref.at[g], snew_ref.at[g], nw))


def _gla_mixer(proj, gla_s, w, layer, prev_gla):
    B, T, _ = proj.shape
    i = layer // 2
    n_odd = w["w_in_odd"].shape[0]
    G = SAMPLE_SEQS
    state = (None, G, GLA_HEADS, GLA_DK, GLA_DV)
    in_specs = [pl.BlockSpec((G, T, ODD_OUT), lambda b: (b, 0, 0)),
                pl.BlockSpec(state, lambda b: (i, b, 0, 0, 0)),
                _layer_spec((1, GLA_DV), i)]
    args = [proj, gla_s, w["gla_norm"]]
    in_specs.append(pl.BlockSpec(memory_space=pl.ANY))
    args.append(prev_gla)
    aliases = {len(args) - 1: 1}
    return pl.pallas_call(
        functools.partial(_gla_mixer_kernel, T, G),
        grid=(B // G,),
        in_specs=in_specs,
        out_specs=(pl.BlockSpec((G, T, D_MODEL), lambda b: (b, 0, 0)),
                   pl.BlockSpec(state, lambda b: (i, b, 0, 0, 0))),
        out_shape=(jax.ShapeDtypeStruct((B, T, D_MODEL), F32),
                   jax.ShapeDtypeStruct((n_odd, B, GLA_HEADS, GLA_DK, GLA_DV), F32)),
        input_output_aliases=aliases,
        compiler_params=_params("parallel"),
        name="gla_mixer",
    )(*args)


def _fused_layer_kernel(even, final, chunk, rows, tiles_per_seq, n_tiles, slot, has_prev, *refs):
    it = iter(refs)
    xlead_ref, xlag_ref, gmix_ref, win_ref = (next(it) for _ in range(4))
    if even:
        cos_ref, sin_ref = next(it), next(it)
        mixer_weights = tuple(next(it) for _ in range(7))
    else:
        wg1_ref, wg2_ref, bg_ref, nw_ref = (next(it) for _ in range(4))
    wo_ref, gmlp_ref, wu_ref, wd_ref, gf_ref = (next(it) for _ in range(5))
    if has_prev:
        next(it)
    out_ref, snew_ref = next(it), next(it)
    if even:
        hnew_ref, cnew_ref = next(it), next(it)
    if not has_prev:
        zeros_ref = next(it)
    proj_scr = (next(it), next(it))
    mix_scr = (next(it), next(it))
    x1_scr, u_scr, s_scr = next(it), next(it), next(it)
    if even:
        h_scr, tail_scr = next(it), next(it)

    s = pl.program_id(0)
    tile_in_seq = lax.rem(jnp.clip(s - 1, 0, n_tiles - 1), tiles_per_seq)

    @pl.when(s == 0)
    def _():
        proj_scr[1][...] = jnp.zeros(proj_scr[1].shape, F32)
        mix_scr[0][...] = jnp.zeros(mix_scr[0].shape, BF16)

    @pl.when((s == 0) | (lax.rem(s + tiles_per_seq - 1, tiles_per_seq) == 0))
    def _():
        s_scr[...] = jnp.zeros(s_scr.shape, F32)
        if even:
            h_scr[...] = jnp.zeros(h_scr.shape, F32)
            tail_scr[...] = jnp.zeros(tail_scr.shape, F32)

    main_cols = EVEN_IN if even else ODD_MAIN
    macs_per_row = D_MODEL * (main_cols + D_MODEL + 2 * D_FF)

    def matrix_stream(proj_a, mix_c, pace):
        P = MXU_PIECE
        h_in = _rmsnorm(xlead_ref[...], gmix_ref[...]).astype(BF16)
        m = mix_c[...]
        for n0 in range(0, D_MODEL, P):
            x1 = xlag_ref[:, n0:n0 + P] + _dot(m, wo_ref[:, n0:n0 + P])
            x1_scr[:, n0:n0 + P] = x1
            pace.after(x1)
            yield D_MODEL * P / macs_per_row
        h_mlp = _rmsnorm(x1_scr[...], gmlp_ref[...]).astype(BF16)
        if not even:
            glr = _dot(h_in, wg1_ref[...]).astype(BF16)
        for n0 in range(0, main_cols, P):
            p = _dot(h_in, win_ref[:, n0:n0 + P])
            proj_a[:, n0:n0 + P] = p
            pace.after(p)
            if not even and n0 == P:
                z = _dot(glr, wg2_ref[...]) + bg_ref[...]
            yield D_MODEL * P / macs_per_row
        if not even:
            proj_a[:, ODD_MAIN:] = _scaled_log_sigmoid(z)
            yield 0.0
        for n0 in range(0, D_FF, P):
            u = jnp.square(jnp.maximum(_dot(h_mlp, wu_ref[:, n0:n0 + P]), 0.0))
            u_scr[:, n0:n0 + P] = u.astype(BF16)
            pace.after(u)
            yield D_MODEL * P / macs_per_row
        for n0 in range(0, D_MODEL, P):
            x2 = x1_scr[:, n0:n0 + P] + _dot(u_scr[...], wd_ref[:, n0:n0 + P])
            if final:
                x1_scr[:, n0:n0 + P] = x2
            else:
                out_ref[:, n0:n0 + P] = x2
            pace.after(x2)
            yield D_FF * P / macs_per_row
        if final:
            out_ref[...] = _rmsnorm(x1_scr[...], gf_ref[...])

    def step(par):
        proj_a, proj_b = proj_scr[par], proj_scr[1 - par]
        mix_b, mix_c = mix_scr[1 - par], mix_scr[par]

        def load(r, c):
            return proj_b[r, c]

        def store(r, c, val):
            mix_b[r, c] = val.astype(BF16)

        pace = _Pace()
        if even:
            vector_stream = _even_mixer_tile(chunk, rows // chunk, tile_in_seq * rows, load,
                                             store, cos_ref, sin_ref, s_scr, s_scr, h_scr, h_scr,
                                             tail_scr, mixer_weights, pace)
        else:
            vector_stream = _gla_mixer_tile(chunk, rows // chunk, load, store, s_scr, s_scr,
                                            nw_ref[...], pace)
        _interleave(matrix_stream(proj_a, mix_c, pace), vector_stream)

    parity = lax.rem(s, 2)
    pl.when(parity == 0)(functools.partial(step, 0))
    pl.when(parity == 1)(functools.partial(step, 1))

    if not has_prev:
        zeros_ref[...] = jnp.zeros(zeros_ref.shape, F32)

    @pl.when((s >= 1) & (s <= n_tiles) & (lax.rem(s, tiles_per_seq) == 0))
    def _():
        if has_prev:
            snew_ref[0] = s_scr[...]
        else:
            for other in range(snew_ref.shape[0]):
                snew_ref[other, 0] = s_scr[...] if other == slot else jnp.zeros(s_scr.shape, F32)
        if even:
            hnew_ref[0] = h_scr[...]
            cnew_ref[0] = tail_scr[SUBLANES - (CONV_W - 1):SUBLANES, :]


def _fused_layer(x, w, layer, cos, sin, prev_state, seq_len, n_short):
    n_rows = x.shape[0]
    B = n_rows // seq_len
    R = FUSED_ROWS
    even = layer % 2 == 0
    final = layer == DEPTH - 1
    i = layer // 2
    C = math.gcd(seq_len, RET_CHUNK if even else GLA_CHUNK)
    tps = seq_len // R
    NT = n_rows // R
    has_prev = prev_state is not None

    def lead(s):
        return (jnp.minimum(s, NT - 1), 0)

    def mid_tile(s):
        return jnp.clip(s - 1, 0, NT - 1)

    def lag(s):
        return (jnp.clip(s - 2, 0, NT - 1), 0)

    def seq3(s):
        return (mid_tile(s) // tps, 0, 0)

    in_specs = [pl.BlockSpec((R, D_MODEL), lead),
                pl.BlockSpec((R, D_MODEL), lag),
                _layer_spec((1, D_MODEL), layer)]
    args = [x, x, w["norm_mix"]]
    if even:
        in_specs += [_layer_spec((D_MODEL, EVEN_IN), i),
                     pl.BlockSpec((R, RET_KEY), lambda s: (mid_tile(s) % tps, 0)),
                     pl.BlockSpec((R, RET_KEY), lambda s: (mid_tile(s) % tps, 0)),
                     _layer_spec((CONV_W, LRU_WIDTH), i),
                     _layer_spec((1, LRU_WIDTH), i),
                     _layer_spec((LRU_WIDTH, LRU_WIDTH), i),
                     _layer_spec((1, LRU_WIDTH), i),
                     _layer_spec((LRU_WIDTH, LRU_WIDTH), i),
                     _layer_spec((1, LRU_WIDTH), i),
                     _layer_spec((1, LRU_WIDTH), i)]
        args += [w["w_in_even"], cos, sin, w["conv_w"], w["conv_b"], w["wa"], w["ba"],
                 w["wi"], w["bi"], w["lam"]]
        w_out = w["w_out_even"]
        state_shape = (RET_HEADS, RET_DK, RET_DV)
        n_stack = w["w_in_even"].shape[0]
        proj_cols = EVEN_IN
    else:
        in_specs += [_layer_spec((D_MODEL, ODD_MAIN), i),
                     _layer_spec((D_MODEL, LANES), i),
                     _layer_spec((LANES, GLA_KEY), i),
                     _layer_spec((1, GLA_KEY), i),
                     _layer_spec((1, GLA_DV), i)]
        args += [w["w_in_odd"], w["wg1"], w["wg2"], w["bg"], w["gla_norm"]]
        w_out = w["w_out_odd"]
        state_shape = (GLA_HEADS, GLA_DK, GLA_DV)
        n_stack = w["w_in_odd"].shape[0]
        proj_cols = ODD_OUT
    in_specs += [_layer_spec((D_MODEL, D_MODEL), i),
                 _layer_spec((1, D_MODEL), layer),
                 _layer_spec((D_MODEL, D_FF), layer),
                 _layer_spec((D_FF, D_MODEL), layer),
                 _const_spec((1, D_MODEL))]
    args += [w_out, w["norm_mlp"], w["w_up"], w["w_down"], w["norm_final"]]
    aliases = {}
    if has_prev:
        in_specs.append(pl.BlockSpec(memory_space=pl.ANY))
        args.append(prev_state)
        aliases = {len(args) - 1: 1}

    if has_prev:
        state_spec = pl.BlockSpec((None, 1) + state_shape,
                                  lambda s: (i, mid_tile(s) // tps, 0, 0, 0))
    else:
        state_spec = pl.BlockSpec((n_stack, 1) + state_shape,
                                  lambda s: (0, mid_tile(s) // tps, 0, 0, 0))
    out_specs = [pl.BlockSpec((R, D_MODEL), lag), state_spec]
    out_shapes = [jax.ShapeDtypeStruct((n_rows, D_MODEL), F32),
                  jax.ShapeDtypeStruct((n_stack, B) + state_shape, F32)]
    scratch = [pltpu.VMEM((R, proj_cols), F32), pltpu.VMEM((R, proj_cols), F32),
               pltpu.VMEM((R, D_MODEL), BF16), pltpu.VMEM((R, D_MODEL), BF16),
               pltpu.VMEM((R, D_MODEL), F32), pltpu.VMEM((R, D_FF), BF16),
               pltpu.VMEM(state_shape, F32)]
    if even:
        out_specs += [pl.BlockSpec((1, 1, LRU_WIDTH), seq3),
                      pl.BlockSpec((1, CONV_W - 1, LRU_WIDTH), seq3)]
        out_shapes += [jax.ShapeDtypeStruct((B, 1, LRU_WIDTH), F32),
                       jax.ShapeDtypeStruct((B, CONV_W - 1, LRU_WIDTH), F32)]
        scratch += [pltpu.VMEM((1, LRU_WIDTH), F32),
                    pltpu.VMEM((SUBLANES, LRU_WIDTH), F32)]
    if not has_prev:
        per_step = n_short // NT
        assert per_step * NT == n_short
        out_specs.append(pl.BlockSpec((n_stack, per_step) + state_shape,
                                      lambda s: (0, jnp.minimum(s, NT - 1), 0, 0, 0)))
        out_shapes.append(jax.ShapeDtypeStruct((n_stack, n_short) + state_shape, F32))
    return pl.pallas_call(
        functools.partial(_fused_layer_kernel, even, final, C, R, tps, NT, i, has_prev),
        grid=(NT + 2,),
        in_specs=in_specs,
        out_specs=tuple(out_specs),
        out_shape=tuple(out_shapes),
        scratch_shapes=scratch,
        input_output_aliases=aliases,
        compiler_params=_params("arbitrary"),
        name="layer_even" if even else "layer_odd",
    )(*args)


def _rope_tables(pos0, T):
    half = RET_DK // 2
    inv = ROPE_BASE ** (-jnp.arange(half, dtype=F32) / half)
    ang = (pos0 + jnp.arange(T, dtype=jnp.int32)).astype(F32)[:, None] * inv[None, :]
    cos = jnp.cos(ang)
    sin = jnp.sin(ang)
    cos_full = jnp.tile(jnp.concatenate([cos, cos], axis=-1), (1, RET_HEADS))
    sin_signed = jnp.tile(jnp.concatenate([-sin, sin], axis=-1), (1, RET_HEADS))
    return cos_full, sin_signed


def _block_diag(w):
    nl, nb, c, d = w.shape
    eye = jnp.eye(nb, dtype=w.dtype)
    return (eye[None, :, None, :, None] * w[:, :, :, None, :]).reshape(nl, nb * c, nb * d)


def _prepare_weights(norm_mix, norm_mlp, norm_final, w_in_even, w_out_even, conv_w, conv_b,
                     lru_w_a, lru_b_a, lru_w_i, lru_b_i, lru_lambda,
                     w_in_odd, gla_w_gate2, gla_b_gate, gla_norm, w_out_odd, w_up, w_down):
    n_even = w_in_even.shape[0]
    n_odd = w_in_odd.shape[0]
    pad1 = LANES - GLA_RANK
    return dict(
        norm_mix=norm_mix.reshape(DEPTH, 1, D_MODEL),
        norm_mlp=norm_mlp.reshape(DEPTH, 1, D_MODEL),
        norm_final=norm_final.reshape(1, D_MODEL),
        w_in_even=w_in_even.astype(BF16),
        w_out_even=w_out_even.astype(BF16),
        conv_w=conv_w,
        conv_b=conv_b.reshape(n_even, 1, LRU_WIDTH),
        wa=_block_diag(lru_w_a).astype(BF16),
        ba=lru_b_a.reshape(n_even, 1, LRU_WIDTH),
        wi=_block_diag(lru_w_i).astype(BF16),
        bi=lru_b_i.reshape(n_even, 1, LRU_WIDTH),
        lam=lru_lambda.reshape(n_even, 1, LRU_WIDTH),
        w_in_odd=w_in_odd.astype(BF16),
        wg1=jnp.pad(w_in_odd[:, :, ODD_MAIN:], ((0, 0), (0, 0), (0, pad1))).astype(BF16),
        wg2=jnp.pad(gla_w_gate2, ((0, 0), (0, pad1), (0, 0))).astype(BF16),
        bg=gla_b_gate.reshape(n_odd, 1, GLA_KEY),
        gla_norm=gla_norm.reshape(n_odd, 1, GLA_DV),
        w_out_odd=w_out_odd.astype(BF16),
        w_up=w_up.astype(BF16),
        w_down=w_down.astype(BF16),
    )


def _prompt_trunk(x, w, n_short):
    B, T, _ = x.shape
    xf = x.reshape(B * T, D_MODEL)
    cos, sin = _rope_tables(0, T)
    ret_new, gla_new = None, None
    lrus, convs, zeros = [], [], []
    for layer in range(DEPTH):
        prev = ret_new if layer % 2 == 0 else gla_new
        outs = _fused_layer(xf, w, layer, cos, sin, prev, T, n_short)
        if prev is None:
            zeros.append(outs[-1])
            outs = outs[:-1]
        if layer % 2 == 0:
            xf, ret_new, lh, cb = outs
            lrus.append(lh.reshape(B, LRU_WIDTH))
            convs.append(cb)
        else:
            xf, gla_new = outs
    states = (ret_new, jnp.stack(lrus), jnp.stack(convs), gla_new)
    return xf.reshape(B, T, D_MODEL), states, zeros


def _sample_trunk(x, pos0, ret_s, lru_h, conv_buf, gla_s, ret_new, gla_new, w):
    B, T, _ = x.shape
    xf = x.reshape(B * T, D_MODEL)
    cos, sin = _rope_tables(pos0, T)
    lru_h = lru_h.reshape(lru_h.shape[0], B, 1, LRU_WIDTH)
    lrus, convs = [], []
    for layer in range(DEPTH):
        if layer % 2 == 0:
            proj = _norm_proj_even(xf, w, layer)
            mix, ret_new, lh, cb = _even_mixer(proj.reshape(B, T, EVEN_IN), pos0, cos, sin,
                                               ret_s, lru_h, conv_buf, w, layer, ret_new)
            lrus.append(lh.reshape(B, LRU_WIDTH))
            convs.append(cb)
        else:
            proj = _norm_proj_odd(xf, w, layer)
            mix, gla_new = _gla_mixer(proj.reshape(B, T, ODD_OUT), gla_s, w, layer, gla_new)
        xf = _out_mlp(xf, mix.reshape(B * T, D_MODEL), w, layer)
    return xf.reshape(B, T, D_MODEL), ret_new, jnp.stack(lrus), jnp.stack(convs), gla_new


def kernel(x_prompt, x_sample, state_ret, state_lru, state_conv, state_gla, norm_mix, norm_mlp, norm_final, w_in_even, w_out_even, conv_w, conv_b, lru_w_a, lru_b_a, lru_w_i, lru_b_i, lru_lambda, w_in_odd, gla_w_gate2, gla_b_gate, gla_norm, w_out_odd, w_up, w_down):
    w = _prepare_weights(norm_mix, norm_mlp, norm_final, w_in_even, w_out_even, conv_w, conv_b,
                         lru_w_a, lru_b_a, lru_w_i, lru_b_i, lru_lambda,
                         w_in_odd, gla_w_gate2, gla_b_gate, gla_norm, w_out_odd, w_up, w_down)
    y_p, (ret_p, lru_p, conv_p, gla_p), (ret_zero, gla_zero) = _prompt_trunk(
        x_prompt, w, x_sample.shape[0])
    y_s, ret_s, lru_s, conv_s, gla_s = _sample_trunk(x_sample, PAST_LEN, state_ret, state_lru,
                                                     state_conv, state_gla, ret_zero, gla_zero, w)
    return (y_p, y_s, ret_p, ret_s, lru_p, lru_s, conv_p, conv_s, gla_p, gla_s)
```

```python
import functools
import math

import jax
import jax.numpy as jnp
from jax import lax
from jax.experimental import pallas as pl
from jax.experimental.pallas import tpu as pltpu

F32 = jnp.float32
BF16 = jnp.bfloat16

D_MODEL = 1024
DEPTH = 4
PAST_LEN = 16384
EPS = 1e-6
ROPE_BASE = 10000.0

RET_HEADS = 4
RET_DK = 64
RET_DV = 128
RET_KEY = RET_HEADS * RET_DK
RET_VALUE = RET_HEADS * RET_DV
RET_CHUNK = 128

LRU_WIDTH = 512
CONV_W = 4
LRU_C = 8.0

GLA_HEADS = 4
GLA_KEY = 512
GLA_VALUE = 1024
GLA_DK = GLA_KEY // GLA_HEADS
GLA_DV = GLA_VALUE // GLA_HEADS
GLA_RANK = 16
GLA_TAU = 16.0
GLA_CHUNK = 64

D_FF = 4 * D_MODEL
FF_CHUNK = 1024

EVEN_IN = 2 * RET_KEY + 2 * RET_VALUE + 2 * LRU_WIDTH
ODD_MAIN = 2 * GLA_KEY + 2 * GLA_VALUE
ODD_OUT = ODD_MAIN + GLA_KEY

LANES = 128
SUBLANES = 8
ROW_TILE = 512
FUSED_ROWS = 256
MXU_PIECE = 256
SAMPLE_SEQS = 8
VMEM_LIMIT = 56 * 1024 * 1024


def _dot(a, b):
    return jnp.dot(a.astype(BF16), b.astype(BF16), preferred_element_type=F32)


def _dot_nt(a, b):
    return lax.dot_general(a.astype(BF16), b.astype(BF16), (((1,), (1,)), ((), ())),
                           preferred_element_type=F32)


def _dot_tn(a, b):
    return lax.dot_general(a.astype(BF16), b.astype(BF16), (((0,), (0,)), ((), ())),
                           preferred_element_type=F32)


def _rmsnorm(x, g):
    return x * lax.rsqrt(jnp.mean(x * x, axis=-1, keepdims=True) + EPS) * g


def _head_rms(x, eps=EPS):
    return x * lax.rsqrt(jnp.mean(x * x, axis=-1, keepdims=True) + eps)


def _shift_rows(x, d, fill):
    row = lax.broadcasted_iota(jnp.int32, x.shape, 0)
    return jnp.where(row >= d, pltpu.roll(x, d, axis=0), fill)


def _layer_spec(shape, layer):
    n = len(shape)
    return pl.BlockSpec((None,) + tuple(shape), lambda *_: (layer,) + (0,) * n,
                        pipeline_mode=pl.Buffered(1))


def _const_spec(shape):
    n = len(shape)
    return pl.BlockSpec(shape, lambda *_: (0,) * n, pipeline_mode=pl.Buffered(1))


def _params(*semantics):
    return pltpu.CompilerParams(dimension_semantics=semantics, vmem_limit_bytes=VMEM_LIMIT)


def _scaled_log_sigmoid(z):
    return (jnp.minimum(z, 0.0) - jnp.log(1.0 + jnp.exp(-jnp.abs(z)))) / GLA_TAU


def _log_gate(h, wg1_ref, wg2_ref, bg_ref):
    glr = _dot(h, wg1_ref[...])
    return _scaled_log_sigmoid(_dot(glr, wg2_ref[...]) + bg_ref[...])


def _norm_proj_even_kernel(x_ref, g_ref, w_ref, o_ref):
    h = _rmsnorm(x_ref[...], g_ref[...])
    o_ref[...] = _dot(h, w_ref[...])


def _norm_proj_odd_kernel(x_ref, g_ref, w_ref, wg1_ref, wg2_ref, bg_ref, o_ref):
    h = _rmsnorm(x_ref[...], g_ref[...]).astype(BF16)
    o_ref[:, :ODD_MAIN] = _dot(h, w_ref[...])
    o_ref[:, ODD_MAIN:] = _log_gate(h, wg1_ref, wg2_ref, bg_ref)


def _out_mlp_kernel(final, x_ref, m_ref, wo_ref, g_ref, wu_ref, wd_ref, gf_ref, o_ref,
                    h_scr, acc_scr):
    c = pl.program_id(0)

    @pl.when(c == 0)
    def _():
        x1 = x_ref[...] + _dot(m_ref[...], wo_ref[...])
        acc_scr[...] = x1
        h_scr[...] = _rmsnorm(x1, g_ref[...]).astype(BF16)

    u = jnp.square(jnp.maximum(_dot(h_scr[...], wu_ref[...]), 0.0))
    acc_scr[...] += _dot(u, wd_ref[...])

    @pl.when(c == pl.num_programs(0) - 1)
    def _():
        x2 = acc_scr[...]
        o_ref[...] = _rmsnorm(x2, gf_ref[...]) if final else x2


def _norm_proj_even(x, w, layer):
    n = x.shape[0]
    tm = min(ROW_TILE, n)
    return pl.pallas_call(
        _norm_proj_even_kernel,
        grid=(n // tm,),
        in_specs=[pl.BlockSpec((tm, D_MODEL), lambda i: (i, 0)),
                  _layer_spec((1, D_MODEL), layer),
                  _layer_spec((D_MODEL, EVEN_IN), layer // 2)],
        out_specs=pl.BlockSpec((tm, EVEN_IN), lambda i: (i, 0)),
        out_shape=jax.ShapeDtypeStruct((n, EVEN_IN), F32),
        compiler_params=_params("parallel"),
        name="norm_proj_even",
    )(x, w["norm_mix"], w["w_in_even"])


def _norm_proj_odd(x, w, layer):
    n = x.shape[0]
    tm = min(ROW_TILE, n)
    i = layer // 2
    return pl.pallas_call(
        _norm_proj_odd_kernel,
        grid=(n // tm,),
        in_specs=[pl.BlockSpec((tm, D_MODEL), lambda i: (i, 0)),
                  _layer_spec((1, D_MODEL), layer),
                  _layer_spec((D_MODEL, ODD_MAIN), i),
                  _layer_spec((D_MODEL, LANES), i),
                  _layer_spec((LANES, GLA_KEY), i),
                  _layer_spec((1, GLA_KEY), i)],
        out_specs=pl.BlockSpec((tm, ODD_OUT), lambda i: (i, 0)),
        out_shape=jax.ShapeDtypeStruct((n, ODD_OUT), F32),
        compiler_params=_params("parallel"),
        name="norm_proj_odd",
    )(x, w["norm_mix"], w["w_in_odd"], w["wg1"], w["wg2"], w["bg"])


def _out_mlp(x, mix, w, layer):
    n = x.shape[0]
    final = layer == DEPTH - 1
    w_out = w["w_out_even"] if layer % 2 == 0 else w["w_out_odd"]
    return pl.pallas_call(
        functools.partial(_out_mlp_kernel, final),
        grid=(D_FF // FF_CHUNK,),
        in_specs=[_const_spec((n, D_MODEL)),
                  _const_spec((n, D_MODEL)),
                  _layer_spec((D_MODEL, D_MODEL), layer // 2),
                  _layer_spec((1, D_MODEL), layer),
                  pl.BlockSpec((None, D_MODEL, FF_CHUNK), lambda c: (layer, 0, c)),
                  pl.BlockSpec((None, FF_CHUNK, D_MODEL), lambda c: (layer, c, 0)),
                  _const_spec((1, D_MODEL))],
        out_specs=pl.BlockSpec((n, D_MODEL), lambda c: (0, 0)),
        out_shape=jax.ShapeDtypeStruct((n, D_MODEL), F32),
        scratch_shapes=[pltpu.VMEM((n, D_MODEL), BF16), pltpu.VMEM((n, D_MODEL), F32)],
        compiler_params=_params("arbitrary"),
        name="out_mlp_final" if final else "out_mlp",
    )(x, mix, w_out, w["norm_mlp"], w["w_up"], w["w_down"], w["norm_final"])


def _run(stream):
    for _ in stream:
        pass


def _interleave(*streams):
    done = [0.0] * len(streams)
    alive = list(range(len(streams)))
    while alive:
        i = min(alive, key=lambda j: done[j])
        try:
            done[i] += next(streams[i])
        except StopIteration:
            alive.remove(i)


class _Pace:
    def __init__(self):
        self.zero = None

    def after(self, value):
        t = value[0:1, 0:1]
        self.zero = jnp.where(t == t, 0.0, t)

    def __call__(self, x):
        return x if self.zero is None else x + self.zero


def _no_pace(x):
    return x


def _even_mixer_tile(C, chunks, pos_base, load, store, cos_ref, sin_ref,
                     s_src, s_dst, h_src, h_dst, tail_ref, weights, pace=_no_pace):
    cw_ref, cbias_ref, wa_ref, ba_ref, wi_ref, bi_ref, lam_ref = weights
    lane = lax.broadcasted_iota(jnp.int32, (C, RET_KEY), 1)
    first_half = (lane % RET_DK) < (RET_DK // 2)
    ri = lax.broadcasted_iota(jnp.int32, (C, C), 0)
    ci = lax.broadcasted_iota(jnp.int32, (C, C), 1)
    rel = (ri - ci).astype(F32)
    row = lax.broadcasted_iota(jnp.int32, (C, 1), 0).astype(F32)
    trow = lax.broadcasted_iota(jnp.int32, (C, LRU_WIDTH), 0)
    row8 = lax.broadcasted_iota(jnp.int32, (SUBLANES, LRU_WIDTH), 0)
    lam = lam_ref[...]
    softplus_neg_lam = jnp.maximum(-lam, 0.0) + jnp.log(1.0 + jnp.exp(-jnp.abs(lam)))
    log_decay = [math.log1p(-(2.0 ** (-5.0 - h))) for h in range(RET_HEADS)]
    decay_mask = [jnp.where(rel >= 0, jnp.exp(jnp.maximum(rel, 0.0) * lg), 0.0)
                  for lg in log_decay]
    q_decay = [jnp.exp((row + 1.0) * lg) for lg in log_decay]
    k_decay = [jnp.exp((C - 1.0 - row) * lg) for lg in log_decay]
    n_scan = max(1, int(math.log2(C)))

    v0 = 2 * RET_KEY
    g0 = v0 + RET_VALUE
    x0 = g0 + RET_VALUE
    xg0 = x0 + LRU_WIDTH
    heads = range(RET_HEADS)
    J = range(chunks)
    rows = [slice(j * C, (j + 1) * C) for j in J]

    def delayed(x, t, n):
        rolled = pltpu.roll(x, n, axis=0)
        head = jnp.where(row8 < n, pltpu.roll(t, n, axis=0), rolled[0:SUBLANES, :])
        if C == SUBLANES:
            return head
        return jnp.concatenate([head, rolled[SUBLANES:, :]], axis=0)

    qh, kh, q_dec, k_dec, vh, xc = ([None] * chunks for _ in range(6))
    tail = tail_ref[...]
    for j in J:
        cos = cos_ref[rows[j], :]
        sin = sin_ref[rows[j], :]

        def rope(x):
            partner = jnp.where(first_half,
                                pltpu.roll(x, RET_KEY - RET_DK // 2, axis=1),
                                pltpu.roll(x, RET_DK // 2, axis=1))
            return x * cos + partner * sin

        q = rope(load(rows[j], slice(0, RET_KEY)))
        k = rope(load(rows[j], slice(RET_KEY, 2 * RET_KEY))) * (RET_DK ** -0.5)
        q = [q[:, h * RET_DK:(h + 1) * RET_DK] for h in heads]
        k = [k[:, h * RET_DK:(h + 1) * RET_DK] for h in heads]
        q_dec[j] = [(q[h] * q_decay[h]).astype(BF16) for h in heads]
        k_dec[j] = [(k[h] * k_decay[h]).astype(BF16) for h in heads]
        qh[j] = [q[h].astype(BF16) for h in heads]
        kh[j] = [k[h].astype(BF16) for h in heads]
        vh[j] = [load(rows[j], slice(v0 + h * RET_DV, v0 + (h + 1) * RET_DV)).astype(BF16)
                 for h in heads]
        yield 0.08 / chunks

        x_new = load(rows[j], slice(x0, x0 + LRU_WIDTH))
        acc = pace(cbias_ref[...])
        for i in range(CONV_W - 1):
            acc = acc + delayed(x_new, tail, CONV_W - 1 - i) * cw_ref[i:i + 1, :]
        xc[j] = acc + x_new * cw_ref[CONV_W - 1:CONV_W, :]
        tail = x_new[C - SUBLANES:C, :]
        yield 0.10 / chunks
    tail_ref[...] = tail

    scores = [[_dot_nt(qh[j][h], kh[j][h]) for h in heads] for j in J]
    kv = [[_dot_tn(k_dec[j][h], vh[j][h]) for h in heads] for j in J]
    xc_all = xc[0] if chunks == 1 else jnp.concatenate(xc, axis=0)
    xc_all = xc_all.astype(BF16)
    diag = [slice(c, c + MXU_PIECE) for c in range(0, LRU_WIDTH, MXU_PIECE)]
    r_all = jnp.concatenate([_dot(xc_all[:, d], wa_ref[d, d]) for d in diag], axis=1)
    i_all = jnp.concatenate([_dot(xc_all[:, d], wi_ref[d, d]) for d in diag], axis=1)
    r_pre = [r_all[rows[j], :] for j in J]
    i_pre = [i_all[rows[j], :] for j in J]
    yield 0.0

    probs, s_in = [None] * chunks, [None] * chunks
    s_cur = [s_src[h] for h in heads]
    for j in J:
        probs[j] = [(scores[j][h] * decay_mask[h]).astype(BF16) for h in heads]
        s_in[j] = [s_cur[h].astype(BF16) for h in heads]
        s_cur = [s_cur[h] * math.exp(C * log_decay[h]) + kv[j][h] for h in heads]
        yield 0.06 / chunks
    for h in heads:
        s_dst[h] = s_cur[h]

    a_cum, b_loc = [None] * chunks, [None] * chunks
    for j in J:
        r = jax.nn.sigmoid(r_pre[j] + pace(ba_ref[...]))
        gate_i = jax.nn.sigmoid(i_pre[j] + bi_ref[...])
        log_a = -LRU_C * r * softplus_neg_lam
        a = jnp.exp(log_a)
        y = -jnp.tanh(log_a) * (a * a + 1.0)
        mult = jnp.where(y > 0.0, y * lax.rsqrt(y), 0.0)
        pos = trow + (pos_base + j * C)
        mult = jnp.where(pos == 0, 1.0, mult)
        b = mult * (gate_i * xc[j])
        yield 0.20 / chunks
        d = 1
        while d < C:
            b = a * _shift_rows(b, d, pace(0.0)) + b
            a = a * _shift_rows(a, d, pace(1.0))
            d *= 2
            yield 0.36 / (chunks * n_scan)
        a_cum[j], b_loc[j] = a, b

    out = [[_dot(probs[j][h], vh[j][h]) + _dot(q_dec[j][h], s_in[j][h]) for h in heads]
           for j in J]
    yield 0.0

    carry = h_src[...]
    for j in J:
        hidden = b_loc[j] + a_cum[j] * carry
        carry = hidden[C - 1:C, :]
        xg = load(rows[j], slice(xg0, xg0 + LRU_WIDTH))
        inner = math.sqrt(2.0 / math.pi) * (xg + 0.044715 * (xg * xg * xg))
        gate = 0.5 * xg.astype(BF16) * (1.0 + jnp.tanh(inner.astype(BF16)))
        store(rows[j], slice(RET_VALUE, RET_VALUE + LRU_WIDTH), hidden.astype(BF16) * gate)
        yield 0.08 / chunks
        for h in heads:
            gh = load(rows[j], slice(g0 + h * RET_DV, g0 + (h + 1) * RET_DV))
            store(rows[j], slice(h * RET_DV, (h + 1) * RET_DV),
                  jax.nn.silu(gh) * _head_rms(out[j][h], pace(EPS)))
            yield 0.03 / chunks
    h_dst[...] = carry


def _gla_mixer_tile(C, chunks, load, store, s_src, s_dst, nw, pace=_no_pace):
    k0 = GLA_KEY
    v0 = 2 * GLA_KEY
    r0 = v0 + GLA_VALUE
    la0 = r0 + GLA_VALUE
    ri = lax.broadcasted_iota(jnp.int32, (C, C), 0)
    ci = lax.broadcasted_iota(jnp.int32, (C, C), 1)
    causal = ri >= ci
    di = lax.broadcasted_iota(jnp.int32, (GLA_DK, GLA_DK), 0)
    dj = lax.broadcasted_iota(jnp.int32, (GLA_DK, GLA_DK), 1)
    eye = di == dj
    heads = range(GLA_HEADS)
    key = [slice(h * GLA_DK, (h + 1) * GLA_DK) for h in heads]
    J = range(chunks)
    rows = [slice(j * C, (j + 1) * C) for j in J]

    q_in, k_in, q_st, k_st, decay_col, vh = ([None] * chunks for _ in range(6))
    for j in J:
        b = load(rows[j], slice(la0, la0 + GLA_KEY))
        d = 1
        while d < C:
            b = b + _shift_rows(b, d, pace(0.0))
            d *= 2
        yield 0.12 / chunks
        b_mid = pace(b[C // 2:C // 2 + 1, :])
        b_last = b[C - 1:C, :]
        q = load(rows[j], slice(0, GLA_KEY)) * (GLA_DK ** -0.5)
        k = load(rows[j], slice(k0, k0 + GLA_KEY))
        q_in[j] = (q * jnp.exp(b - b_mid)).astype(BF16)
        k_in[j] = (k * jnp.exp(b_mid - b)).astype(BF16)
        q_st[j] = (q * jnp.exp(b)).astype(BF16)
        k_st[j] = (k * jnp.exp(b_last - b)).astype(BF16)
        s_decay = jnp.exp(b_last)
        decay_col[j] = [jnp.sum(jnp.where(eye, s_decay[:, key[h]], 0.0), axis=1, keepdims=True)
                        for h in heads]
        vh[j] = [load(rows[j], slice(v0 + h * GLA_DV, v0 + (h + 1) * GLA_DV)).astype(BF16)
                 for h in heads]
        yield 0.28 / chunks

    scores = [[_dot_nt(q_in[j][:, key[h]], k_in[j][:, key[h]]) for h in heads] for j in J]
    kv = [[_dot_tn(k_st[j][:, key[h]], vh[j][h]) for h in heads] for j in J]
    yield 0.0

    probs, s_in = [None] * chunks, [None] * chunks
    s_cur = [s_src[h] for h in heads]
    for j in J:
        probs[j] = [jnp.where(causal, scores[j][h], 0.0).astype(BF16) for h in heads]
        s_in[j] = [s_cur[h].astype(BF16) for h in heads]
        s_cur = [s_cur[h] * decay_col[j][h] + kv[j][h] for h in heads]
        yield 0.16 / chunks
    for h in heads:
        s_dst[h] = s_cur[h]

    out = [[_dot(probs[j][h], vh[j][h]) + _dot(q_st[j][:, key[h]], s_in[j][h]) for h in heads]
           for j in J]
    yield 0.0

    for j in J:
        for h in heads:
            rh = load(rows[j], slice(r0 + h * GLA_DV, r0 + (h + 1) * GLA_DV))
            store(rows[j], slice(h * GLA_DV, (h + 1) * GLA_DV),
                  jax.nn.silu(rh) * (_head_rms(out[j][h], pace(EPS)) * nw))
            yield 0.11 / chunks


def _even_mixer_kernel(pos0, chunk, seqs, *refs):
    (proj_ref, cos_ref, sin_ref, s0_ref, h0_ref, cb_ref,
     cw_ref, cbias_ref, wa_ref, ba_ref, wi_ref, bi_ref, lam_ref) = refs[:13]
    refs = refs[14:]
    mix_ref, snew_ref, hnew_ref, cnew_ref, tail_scr = refs
    weights = (cw_ref, cbias_ref, wa_ref, ba_ref, wi_ref, bi_ref, lam_ref)
    for g in range(seqs):
        tail = tail_scr.at[g]
        tail[...] = jnp.zeros((SUBLANES, LRU_WIDTH), F32)
        tail[SUBLANES - (CONV_W - 1):SUBLANES, :] = cb_ref[g]

        def load(rows, cols, g=g):
            return proj_ref[g, rows, cols]

        def store(rows, cols, val, g=g):
            mix_ref[g, rows, cols] = val.astype(F32)

        _run(_even_mixer_tile(chunk, 1, pos0, load, store, cos_ref, sin_ref,
                              s0_ref.at[g], snew_ref.at[g], h0_ref.at[g], hnew_ref.at[g],
                              tail, weights))
        cnew_ref[g] = tail[SUBLANES - (CONV_W - 1):SUBLANES, :]


def _even_mixer(proj, pos0, cos, sin, ret_s, lru_h, conv_buf, w, layer, prev_ret):
    B, T, _ = proj.shape
    i = layer // 2
    n_even = w["w_in_even"].shape[0]
    G = SAMPLE_SEQS
    ret_block = (None, G, RET_HEADS, RET_DK, RET_DV)
    in_specs = [pl.BlockSpec((G, T, EVEN_IN), lambda b: (b, 0, 0)),
                _const_spec((T, RET_KEY)),
                _const_spec((T, RET_KEY)),
                pl.BlockSpec(ret_block, lambda b: (i, b, 0, 0, 0)),
                pl.BlockSpec((None, G, 1, LRU_WIDTH), lambda b: (i, b, 0, 0)),
                pl.BlockSpec((None, G, CONV_W - 1, LRU_WIDTH), lambda b: (i, b, 0, 0)),
                _layer_spec((CONV_W, LRU_WIDTH), i),
                _layer_spec((1, LRU_WIDTH), i),
                _layer_spec((LRU_WIDTH, LRU_WIDTH), i),
                _layer_spec((1, LRU_WIDTH), i),
                _layer_spec((LRU_WIDTH, LRU_WIDTH), i),
                _layer_spec((1, LRU_WIDTH), i),
                _layer_spec((1, LRU_WIDTH), i)]
    args = [proj, cos, sin, ret_s, lru_h, conv_buf, w["conv_w"], w["conv_b"],
            w["wa"], w["ba"], w["wi"], w["bi"], w["lam"]]
    in_specs.append(pl.BlockSpec(memory_space=pl.ANY))
    args.append(prev_ret)
    aliases = {len(args) - 1: 1}
    out_shapes = (jax.ShapeDtypeStruct((B, T, D_MODEL), F32),
                  jax.ShapeDtypeStruct((n_even, B, RET_HEADS, RET_DK, RET_DV), F32),
                  jax.ShapeDtypeStruct((B, 1, LRU_WIDTH), F32),
                  jax.ShapeDtypeStruct((B, CONV_W - 1, LRU_WIDTH), F32))
    return pl.pallas_call(
        functools.partial(_even_mixer_kernel, pos0, T, G),
        grid=(B // G,),
        in_specs=in_specs,
        out_specs=(pl.BlockSpec((G, T, D_MODEL), lambda b: (b, 0, 0)),
                   pl.BlockSpec(ret_block, lambda b: (i, b, 0, 0, 0)),
                   pl.BlockSpec((G, 1, LRU_WIDTH), lambda b: (b, 0, 0)),
                   pl.BlockSpec((G, CONV_W - 1, LRU_WIDTH), lambda b: (b, 0, 0))),
        out_shape=out_shapes,
        scratch_shapes=[pltpu.VMEM((G, SUBLANES, LRU_WIDTH), F32)],
        input_output_aliases=aliases,
        compiler_params=_params("parallel"),
        name="even_mixer",
    )(*args)


def _gla_mixer_kernel(chunk, seqs, *refs):
    proj_ref, s0_ref, nw_ref = refs[:3]
    refs = refs[4:]
    mix_ref, snew_ref = refs
    nw = nw_ref[...]
    for g in range(seqs):
        def load(rows, cols, g=g):
            return proj_ref[g, rows, cols]

        def store(rows, cols, val, g=g):
            mix_ref[g, rows, cols] = val

        _run(_gla_mixer_tile(chunk, 1, load, store, s0_ref.at[g], snew_ref.at[g], nw))


def _gla_mixer(proj, gla_s, w, layer, prev_gla):
    B, T, _ = proj.shape
    i = layer // 2
    n_odd = w["w_in_odd"].shape[0]
    G = SAMPLE_SEQS
    state = (None, G, GLA_HEADS, GLA_DK, GLA_DV)
    in_specs = [pl.BlockSpec((G, T, ODD_OUT), lambda b: (b, 0, 0)),
                pl.BlockSpec(state, lambda b: (i, b, 0, 0, 0)),
                _layer_spec((1, GLA_DV), i)]
    args = [proj, gla_s, w["gla_norm"]]
    in_specs.append(pl.BlockSpec(memory_space=pl.ANY))
    args.append(prev_gla)
    aliases = {len(args) - 1: 1}
    return pl.pallas_call(
        functools.partial(_gla_mixer_kernel, T, G),
        grid=(B // G,),
        in_specs=in_specs,
        out_specs=(pl.BlockSpec((G, T, D_MODEL), lambda b: (b, 0, 0)),
                   pl.BlockSpec(state, lambda b: (i, b, 0, 0, 0))),
        out_shape=(jax.ShapeDtypeStruct((B, T, D_MODEL), F32),
                   jax.ShapeDtypeStruct((n_odd, B, GLA_HEADS, GLA_DK, GLA_DV), F32)),
        input_output_aliases=aliases,
        compiler_params=_params("parallel"),
        name="gla_mixer",
    )(*args)


def _fused_layer_kernel(even, final, chunk, rows, tiles_per_seq, n_tiles, slot, has_prev, *refs):
    it = iter(refs)
    xlead_ref, xlag_ref, gmix_ref, win_ref = (next(it) for _ in range(4))
    if even:
        cos_ref, sin_ref = next(it), next(it)
        mixer_weights = tuple(next(it) for _ in range(7))
    else:
        wg1_ref, wg2_ref, bg_ref, nw_ref = (next(it) for _ in range(4))
    wo_ref, gmlp_ref, wu_ref, wd_ref, gf_ref = (next(it) for _ in range(5))
    if has_prev:
        next(it)
    out_ref, snew_ref = next(it), next(it)
    if even:
        hnew_ref, cnew_ref = next(it), next(it)
    if not has_prev:
        zeros_ref = next(it)
    proj_scr = (next(it), next(it))
    mix_scr = (next(it), next(it))
    x1_scr, u_scr, s_scr = next(it), next(it), next(it)
    if even:
        h_scr, tail_scr = next(it), next(it)

    s = pl.program_id(0)
    tile_in_seq = lax.rem(jnp.clip(s - 1, 0, n_tiles - 1), tiles_per_seq)

    @pl.when(s == 0)
    def _():
        proj_scr[1][...] = jnp.zeros(proj_scr[1].shape, F32)
        mix_scr[0][...] = jnp.zeros(mix_scr[0].shape, BF16)

    @pl.when((s == 0) | (lax.rem(s + tiles_per_seq - 1, tiles_per_seq) == 0))
    def _():
        s_scr[...] = jnp.zeros(s_scr.shape, F32)
        if even:
            h_scr[...] = jnp.zeros(h_scr.shape, F32)
            tail_scr[...] = jnp.zeros(tail_scr.shape, F32)

    main_cols = EVEN_IN if even else ODD_MAIN
    macs_per_row = D_MODEL * (main_cols + D_MODEL + 2 * D_FF)

    def matrix_stream(proj_a, mix_c, pace):
        P = MXU_PIECE
        h_in = _rmsnorm(xlead_ref[...], gmix_ref[...]).astype(BF16)
        m = mix_c[...]
        for n0 in range(0, D_MODEL, P):
            x1 = xlag_ref[:, n0:n0 + P] + _dot(m, wo_ref[:, n0:n0 + P])
            x1_scr[:, n0:n0 + P] = x1
            pace.after(x1)
            yield D_MODEL * P / macs_per_row
        h_mlp = _rmsnorm(x1_scr[...], gmlp_ref[...]).astype(BF16)
        if not even:
            glr = _dot(h_in, wg1_ref[...]).astype(BF16)
        for n0 in range(0, main_cols, P):
            p = _dot(h_in, win_ref[:, n0:n0 + P])
            proj_a[:, n0:n0 + P] = p
            pace.after(p)
            if not even and n0 == P:
                z = _dot(glr, wg2_ref[...]) + bg_ref[...]
            yield D_MODEL * P / macs_per_row
        if not even:
            proj_a[:, ODD_MAIN:] = _scaled_log_sigmoid(z)
            yield 0.0
        for n0 in range(0, D_FF, P):
            u = jnp.square(jnp.maximum(_dot(h_mlp, wu_ref[:, n0:n0 + P]), 0.0))
            u_scr[:, n0:n0 + P] = u.astype(BF16)
            pace.after(u)
            yield D_MODEL * P / macs_per_row
        for n0 in range(0, D_MODEL, P):
            x2 = x1_scr[:, n0:n0 + P] + _dot(u_scr[...], wd_ref[:, n0:n0 + P])
            if final:
                x1_scr[:, n0:n0 + P] = x2
            else:
                out_ref[:, n0:n0 + P] = x2
            pace.after(x2)
            yield D_FF * P / macs_per_row
        if final:
            out_ref[...] = _rmsnorm(x1_scr[...], gf_ref[...])

    def step(par):
        proj_a, proj_b = proj_scr[par], proj_scr[1 - par]
        mix_b, mix_c = mix_scr[1 - par], mix_scr[par]

        def load(r, c):
            return proj_b[r, c]

        def store(r, c, val):
            mix_b[r, c] = val.astype(BF16)

        pace = _Pace()
        if even:
            vector_stream = _even_mixer_tile(chunk, rows // chunk, tile_in_seq * rows, load,
                                             store, cos_ref, sin_ref, s_scr, s_scr, h_scr, h_scr,
                                             tail_scr, mixer_weights, pace)
        else:
            vector_stream = _gla_mixer_tile(chunk, rows // chunk, load, store, s_scr, s_scr,
                                            nw_ref[...], pace)
        _interleave(matrix_stream(proj_a, mix_c, pace), vector_stream)

    parity = lax.rem(s, 2)
    pl.when(parity == 0)(functools.partial(step, 0))
    pl.when(parity == 1)(functools.partial(step, 1))

    if not has_prev:
        zeros_ref[...] = jnp.zeros(zeros_ref.shape, F32)

    @pl.when((s >= 1) & (s <= n_tiles) & (lax.rem(s, tiles_per_seq) == 0))
    def _():
        if has_prev:
            snew_ref[0] = s_scr[...]
        else:
            for other in range(snew_ref.shape[0]):
                snew_ref[other, 0] = s_scr[...] if other == slot else jnp.zeros(s_scr.shape, F32)
        if even:
            hnew_ref[0] = h_scr[...]
            cnew_ref[0] = tail_scr[SUBLANES - (CONV_W - 1):SUBLANES, :]


def _fused_layer(x, w, layer, cos, sin, prev_state, seq_len, n_short):
    n_rows = x.shape[0]
    B = n_rows // seq_len
    R = FUSED_ROWS
    even = layer % 2 == 0
    final = layer == DEPTH - 1
    i = layer // 2
    C = math.gcd(seq_len, RET_CHUNK if even else GLA_CHUNK)
    tps = seq_len // R
    NT = n_rows // R
    has_prev = prev_state is not None

    def lead(s):
        return (jnp.minimum(s, NT - 1), 0)

    def mid_tile(s):
        return jnp.clip(s - 1, 0, NT - 1)

    def lag(s):
        return (jnp.clip(s - 2, 0, NT - 1), 0)

    def seq3(s):
        return (mid_tile(s) // tps, 0, 0)

    in_specs = [pl.BlockSpec((R, D_MODEL), lead),
                pl.BlockSpec((R, D_MODEL), lag),
                _layer_spec((1, D_MODEL), layer)]
    args = [x, x, w["norm_mix"]]
    if even:
        in_specs += [_layer_spec((D_MODEL, EVEN_IN), i),
                     pl.BlockSpec((R, RET_KEY), lambda s: (mid_tile(s) % tps, 0)),
                     pl.BlockSpec((R, RET_KEY), lambda s: (mid_tile(s) % tps, 0)),
                     _layer_spec((CONV_W, LRU_WIDTH), i),
                     _layer_spec((1, LRU_WIDTH), i),
                     _layer_spec((LRU_WIDTH, LRU_WIDTH), i),
                     _layer_spec((1, LRU_WIDTH), i),
                     _layer_spec((LRU_WIDTH, LRU_WIDTH), i),
                     _layer_spec((1, LRU_WIDTH), i),
                     _layer_spec((1, LRU_WIDTH), i)]
        args += [w["w_in_even"], cos, sin, w["conv_w"], w["conv_b"], w["wa"], w["ba"],
                 w["wi"], w["bi"], w["lam"]]
        w_out = w["w_out_even"]
        state_shape = (RET_HEADS, RET_DK, RET_DV)
        n_stack = w["w_in_even"].shape[0]
        proj_cols = EVEN_IN
    else:
        in_specs += [_layer_spec((D_MODEL, ODD_MAIN), i),
                     _layer_spec((D_MODEL, LANES), i),
                     _layer_spec((LANES, GLA_KEY), i),
                     _layer_spec((1, GLA_KEY), i),
                     _layer_spec((1, GLA_DV), i)]
        args += [w["w_in_odd"], w["wg1"], w["wg2"], w["bg"], w["gla_norm"]]
        w_out = w["w_out_odd"]
        state_shape = (GLA_HEADS, GLA_DK, GLA_DV)
        n_stack = w["w_in_odd"].shape[0]
        proj_cols = ODD_OUT
    in_specs += [_layer_spec((D_MODEL, D_MODEL), i),
                 _layer_spec((1, D_MODEL), layer),
                 _layer_spec((D_MODEL, D_FF), layer),
                 _layer_spec((D_FF, D_MODEL), layer),
                 _const_spec((1, D_MODEL))]
    args += [w_out, w["norm_mlp"], w["w_up"], w["w_down"], w["norm_final"]]
    aliases = {}
    if has_prev:
        in_specs.append(pl.BlockSpec(memory_space=pl.ANY))
        args.append(prev_state)
        aliases = {len(args) - 1: 1}

    if has_prev:
        state_spec = pl.BlockSpec((None, 1) + state_shape,
                                  lambda s: (i, mid_tile(s) // tps, 0, 0, 0))
    else:
        state_spec = pl.BlockSpec((n_stack, 1) + state_shape,
                                  lambda s: (0, mid_tile(s) // tps, 0, 0, 0))
    out_specs = [pl.BlockSpec((R, D_MODEL), lag), state_spec]
    out_shapes = [jax.ShapeDtypeStruct((n_rows, D_MODEL), F32),
                  jax.ShapeDtypeStruct((n_stack, B) + state_shape, F32)]
    scratch = [pltpu.VMEM((R, proj_cols), F32), pltpu.VMEM((R, proj_cols), F32),
               pltpu.VMEM((R, D_MODEL), BF16), pltpu.VMEM((R, D_MODEL), BF16),
               pltpu.VMEM((R, D_MODEL), F32), pltpu.VMEM((R, D_FF), BF16),
               pltpu.VMEM(state_shape, F32)]
    if even:
        out_specs += [pl.BlockSpec((1, 1, LRU_WIDTH), seq3),
                      pl.BlockSpec((1, CONV_W - 1, LRU_WIDTH), seq3)]
        out_shapes += [jax.ShapeDtypeStruct((B, 1, LRU_WIDTH), F32),
                       jax.ShapeDtypeStruct((B, CONV_W - 1, LRU_WIDTH), F32)]
        scratch += [pltpu.VMEM((1, LRU_WIDTH), F32),
                    pltpu.VMEM((SUBLANES, LRU_WIDTH), F32)]
    if not has_prev:
        per_step = n_short // NT
        assert per_step * NT == n_short
        out_specs.append(pl.BlockSpec((n_stack, per_step) + state_shape,
                                      lambda s: (0, jnp.minimum(s, NT - 1), 0, 0, 0)))
        out_shapes.append(jax.ShapeDtypeStruct((n_stack, n_short) + state_shape, F32))
    return pl.pallas_call(
        functools.partial(_fused_layer_kernel, even, final, C, R, tps, NT, i, has_prev),
        grid=(NT + 2,),
        in_specs=in_specs,
        out_specs=tuple(out_specs),
        out_shape=tuple(out_shapes),
        scratch_shapes=scratch,
        input_output_aliases=aliases,
        compiler_params=_params("arbitrary"),
        name="layer_even" if even else "layer_odd",
    )(*args)


def _rope_tables(pos0, T):
    half = RET_DK // 2
    inv = ROPE_BASE ** (-jnp.arange(half, dtype=F32) / half)
    ang = (pos0 + jnp.arange(T, dtype=jnp.int32)).astype(F32)[:, None] * inv[None, :]
    cos = jnp.cos(ang)
    sin = jnp.sin(ang)
    cos_full = jnp.tile(jnp.concatenate([cos, cos], axis=-1), (1, RET_HEADS))
    sin_signed = jnp.tile(jnp.concatenate([-sin, sin], axis=-1), (1, RET_HEADS))
    return cos_full, sin_signed


def _block_diag(w):
    nl, nb, c, d = w.shape
    eye = jnp.eye(nb, dtype=w.dtype)
    return (eye[None, :, None, :, None] * w[:, :, :, None, :]).reshape(nl, nb * c, nb * d)


def _prepare_weights(norm_mix, norm_mlp, norm_final, w_in_even, w_out_even, conv_w, conv_b,
                     lru_w_a, lru_b_a, lru_w_i, lru_b_i, lru_lambda,
                     w_in_odd, gla_w_gate2, gla_b_gate, gla_norm, w_out_odd, w_up, w_down):
    n_even = w_in_even.shape[0]
    n_odd = w_in_odd.shape[0]
    pad1 = LANES - GLA_RANK
    return dict(
        norm_mix=norm_mix.reshape(DEPTH, 1, D_MODEL),
        norm_mlp=norm_mlp.reshape(DEPTH, 1, D_MODEL),
        norm_final=norm_final.reshape(1, D_MODEL),
        w_in_even=w_in_even.astype(BF16),
        w_out_even=w_out_even.astype(BF16),
        conv_w=conv_w,
        conv_b=conv_b.reshape(n_even, 1, LRU_WIDTH),
        wa=_block_diag(lru_w_a).astype(BF16),
        ba=lru_b_a.reshape(n_even, 1, LRU_WIDTH),
        wi=_block_diag(lru_w_i).astype(BF16),
        bi=lru_b_i.reshape(n_even, 1, LRU_WIDTH),
        lam=lru_lambda.reshape(n_even, 1, LRU_WIDTH),
        w_in_odd=w_in_odd.astype(BF16),
        wg1=jnp.pad(w_in_odd[:, :, ODD_MAIN:], ((0, 0), (0, 0), (0, pad1))).astype(BF16),
        wg2=jnp.pad(gla_w_gate2, ((0, 0), (0, pad1), (0, 0))).astype(BF16),
        bg=gla_b_gate.reshape(n_odd, 1, GLA_KEY),
        gla_norm=gla_norm.reshape(n_odd, 1, GLA_DV),
        w_out_odd=w_out_odd.astype(BF16),
        w_up=w_up.astype(BF16),
        w_down=w_down.astype(BF16),
    )


def _prompt_trunk(x, w, n_short):
    B, T, _ = x.shape
    xf = x.reshape(B * T, D_MODEL)
    cos, sin = _rope_tables(0, T)
    ret_new, gla_new = None, None
    lrus, convs, zeros = [], [], []
    for layer in range(DEPTH):
        prev = ret_new if layer % 2 == 0 else gla_new
        outs = _fused_layer(xf, w, layer, cos, sin, prev, T, n_short)
        if prev is None:
            zeros.append(outs[-1])
            outs = outs[:-1]
        if layer % 2 == 0:
            xf, ret_new, lh, cb = outs
            lrus.append(lh.reshape(B, LRU_WIDTH))
            convs.append(cb)
        else:
            xf, gla_new = outs
    states = (ret_new, jnp.stack(lrus), jnp.stack(convs), gla_new)
    return xf.reshape(B, T, D_MODEL), states, zeros


def _sample_trunk(x, pos0, ret_s, lru_h, conv_buf, gla_s, ret_new, gla_new, w):
    B, T, _ = x.shape
    xf = x.reshape(B * T, D_MODEL)
    cos, sin = _rope_tables(pos0, T)
    lru_h = lru_h.reshape(lru_h.shape[0], B, 1, LRU_WIDTH)
    lrus, convs = [], []
    for layer in range(DEPTH):
        if layer % 2 == 0:
            proj = _norm_proj_even(xf, w, layer)
            mix, ret_new, lh, cb = _even_mixer(proj.reshape(B, T, EVEN_IN), pos0, cos, sin,
                                               ret_s, lru_h, conv_buf, w, layer, ret_new)
            lrus.append(lh.reshape(B, LRU_WIDTH))
            convs.append(cb)
        else:
            proj = _norm_proj_odd(xf, w, layer)
            mix, gla_new = _gla_mixer(proj.reshape(B, T, ODD_OUT), gla_s, w, layer, gla_new)
        xf = _out_mlp(xf, mix.reshape(B * T, D_MODEL), w, layer)
    return xf.reshape(B, T, D_MODEL), ret_new, jnp.stack(lrus), jnp.stack(convs), gla_new


def kernel(x_prompt, x_sample, state_ret, state_lru, state_conv, state_gla, norm_mix, norm_mlp, norm_final, w_in_even, w_out_even, conv_w, conv_b, lru_w_a, lru_b_a, lru_w_i, lru_b_i, lru_lambda, w_in_odd, gla_w_gate2, gla_b_gate, gla_norm, w_out_odd, w_up, w_down):
    w = _prepare_weights(norm_mix, norm_mlp, norm_final, w_in_even, w_out_even, conv_w, conv_b,
                         lru_w_a, lru_b_a, lru_w_i, lru_b_i, lru_lambda,
                         w_in_odd, gla_w_gate2, gla_b_gate, gla_norm, w_out_odd, w_up, w_down)
    y_p, (ret_p, lru_p, conv_p, gla_p), (ret_zero, gla_zero) = _prompt_trunk(
        x_prompt, w, x_sample.shape[0])
    y_s, ret_s, lru_s, conv_s, gla_s = _sample_trunk(x_sample, PAST_LEN, state_ret, state_lru,
                                                     state_conv, state_gla, ret_zero, gla_zero, w)
    return (y_p, y_s, ret_p, ret_s, lru_p, lru_s, conv_p, conv_s, gla_p, gla_s)
```
